```python
import jax, jax.numpy as jnp
from jax import lax
import numpy as np

D_MODEL = 2048
BATCH = 4
SEQ = 2048
DEPTH = 1
DEC_BATCH = 128
DEC_SEQ = 8
PAST_LEN = 16384
PAGE_SIZE = 128

MIX_WIDTH = D_MODEL // 2
RWKV_HEAD_DIM = 64
RWKV_HEADS = MIX_WIDTH // RWKV_HEAD_DIM
DECAY_RANK = 64
AAA_RANK = 64
GATE_RANK = 128
SGU_CHUNK = 128
SGU_GROUPS = 8
SGU_GROUP_DIM = MIX_WIDTH // SGU_GROUPS
XATTN_HEADS = 4
XATTN_HEAD_DIM = MIX_WIDTH // XATTN_HEADS
MEM_LEN = 256
N_BRANCH = 3
D_FF = 11 * D_MODEL // 4
CONV_WIDTH = 3
RMS_EPS = 1e-6
LN_EPS = 1e-5
GN_EPS = 64e-5
N_SHIFT = 3 * MIX_WIDTH + DECAY_RANK + AAA_RANK + GATE_RANK
N_IN = N_SHIFT + 2 * MIX_WIDTH + MIX_WIDTH + N_BRANCH * D_MODEL

kernel_name = 'rwkv7_sgu_memxattn_gated_hybrid_step'


def rmsnorm(x, g):
    xf = x.astype(jnp.float32)
    y = xf * lax.rsqrt(jnp.mean(xf * xf, axis=-1, keepdims=True) + RMS_EPS)
    return (y * g.astype(jnp.float32)).astype(x.dtype)


def rwkv7_mix(rw, wkv0, w0, w_decay, a0, w_aaa, w_gate, k_k, k_a, r_k, lnx_w, lnx_b):
    f32 = jnp.float32
    B, T, _ = rw.shape
    W = MIX_WIDTH
    r, k, v, xw, xa, xg = jnp.split(rw.astype(f32), [W, 2 * W, 3 * W, 3 * W + DECAY_RANK, 3 * W + DECAY_RANK + AAA_RANK], axis=-1)
    w_log = -jax.nn.softplus(-(w0.astype(f32) + jnp.tanh(xw) @ w_decay.astype(f32))) - 0.5
    decay = jnp.exp(-jnp.exp(w_log))
    a = jax.nn.sigmoid(a0.astype(f32) + xa @ w_aaa.astype(f32))
    g = jax.nn.sigmoid(xg) @ w_gate.astype(f32)
    kk = k * k_k.astype(f32)
    k = k * (1.0 + (a - 1.0) * k_a.astype(f32))
    heads = lambda t: t.reshape(B, T, RWKV_HEADS, RWKV_HEAD_DIM)
    kk = heads(kk)
    kk = kk / jnp.maximum(jnp.sqrt(jnp.sum(kk * kk, axis=-1, keepdims=True)), 1e-12)
    r_h, k_h, v_h, w_h, a_h = heads(r), heads(k), heads(v), heads(decay), heads(a)
    a_vec = -kk
    b_vec = kk * a_h

    def step(S, inp):
        r_t, w_t, k_t, v_t, av_t, bv_t = inp
        sa = jnp.einsum('bhvk,bhk->bhv', S, av_t)
        S = S * w_t[:, :, None, :] + sa[..., None] * bv_t[:, :, None, :] + v_t[..., None] * k_t[:, :, None, :]
        return S, jnp.einsum('bhvk,bhk->bhv', S, r_t)

    seq = tuple(jnp.moveaxis(t, 1, 0) for t in (r_h, w_h, k_h, v_h, a_vec, b_vec))
    S, ys = lax.scan(step, wkv0.astype(f32), seq)
    y = jnp.moveaxis(ys, 0, 1)
    mu = jnp.mean(y, axis=-1, keepdims=True)
    var = jnp.mean(jnp.square(y - mu), axis=-1, keepdims=True)
    yn = ((y - mu) * lax.rsqrt(var + GN_EPS)).reshape(B, T, W) * lnx_w.astype(f32) + lnx_b.astype(f32)
    bonus = jnp.sum(r_h * k_h * r_k.astype(f32), axis=-1, keepdims=True) * v_h
    out = (yn + bonus.reshape(B, T, W)) * g
    return out.astype(rw.dtype), S.astype(wkv0.dtype)


def sgu_mix(gu, gv, sgu_g, sgu_b, w_s, b_s):
    B, T, W = gv.shape
    u = jax.nn.gelu(gu)
    vf = jax.nn.gelu(gv).astype(jnp.float32)
    mu = jnp.mean(vf, axis=-1, keepdims=True)
    var = jnp.mean(jnp.square(vf - mu), axis=-1, keepdims=True)
    v = (((vf - mu) * lax.rsqrt(var + LN_EPS)) * sgu_g.astype(jnp.float32) + sgu_b.astype(jnp.float32)).astype(gv.dtype)
    n_chunks = -(-T // SGU_CHUNK)
    pad = n_chunks * SGU_CHUNK - T
    vp = jnp.pad(v, ((0, 0), (0, pad), (0, 0))).reshape(B, n_chunks, SGU_CHUNK, SGU_GROUPS, SGU_GROUP_DIM)
    tril = jnp.tril(jnp.ones((SGU_CHUNK, SGU_CHUNK), dtype=bool))
    ws = jnp.where(tril, w_s, jnp.zeros_like(w_s))
    s = jnp.einsum('gts,bnsgc->bntgc', ws, vp) + jnp.transpose(b_s)[None, None, :, :, None]
    s = s.reshape(B, n_chunks * SGU_CHUNK, W)[:, :T]
    return u * s, v


def memory_kv(mem, g_mem, w_mem_k, w_mem_v):
    B, M, _ = mem.shape
    mn = rmsnorm(mem, g_mem)
    k = (mn @ w_mem_k).reshape(B, M, XATTN_HEADS, XATTN_HEAD_DIM)
    v = (mn @ w_mem_v).reshape(B, M, XATTN_HEADS, XATTN_HEAD_DIM)
    return k, v


def mem_xattn(q, mem_k, mem_v):
    B, T, _ = q.shape
    qh = q.reshape(B, T, XATTN_HEADS, XATTN_HEAD_DIM)
    s = jnp.einsum('bthd,bmhd->bhtm', qh, mem_k).astype(jnp.float32) * (XATTN_HEAD_DIM ** -0.5)
    pr = jax.nn.softmax(s, axis=-1).astype(q.dtype)
    return jnp.einsum('bhtm,bmhd->bthd', pr, mem_v).reshape(B, T, MIX_WIDTH)


def conv_ffn(h, conv_prev, w_up, conv_w, conv_b, w_down):
    T = h.shape[1]
    up = h @ w_up
    ext = jnp.concatenate([conv_prev.astype(up.dtype), up], axis=1)
    conv = conv_b + sum(conv_w[j] * ext[:, j:j + T] for j in range(CONV_WIDTH))
    gate, val = jnp.split(conv, 2, axis=-1)
    out = (jax.nn.gelu(gate) * val) @ w_down
    return out, ext[:, -(CONV_WIDTH - 1):]


def layer(x, shift_prev, wkv0, mem_k, mem_v, conv_prev, p):
    B, T, _ = x.shape
    xn = rmsnorm(x, p['g_pre_mix'])
    proj = xn @ p['w_in']
    W = MIX_WIDTH
    rw, gu, gv, q, gt = jnp.split(proj, [N_SHIFT, N_SHIFT + W, N_SHIFT + 2 * W, N_SHIFT + 3 * W], axis=-1)
    prev = jnp.concatenate([shift_prev[:, None, :].astype(rw.dtype), rw[:, :-1]], axis=1)
    rw_s = rw + p['mu_shift'] * (prev - rw)
    a_out, wkv_new = rwkv7_mix(rw_s, wkv0, p['w0'], p['w_decay'], p['a0'], p['w_aaa'], p['w_gate'],
                               p['k_k'], p['k_a'], p['r_k'], p['lnx_w'], p['lnx_b'])
    b_out, v_rows = sgu_mix(gu, gv, p['sgu_g'], p['sgu_b'], p['w_s'], p['b_s'])
    c_out = mem_xattn(q, mem_k, mem_v)
    branches = jnp.stack([a_out, b_out, c_out], axis=2)
    br = jnp.einsum('btnw,nwd->btnd', branches, p['w_branch'])
    gates = jax.nn.sigmoid(gt.reshape(B, T, N_BRANCH, D_MODEL))
    mix = jnp.sum(gates * br, axis=2) @ p['w_out']
    h = x + rmsnorm(mix, p['g_post_mix'])
    f, conv_new = conv_ffn(rmsnorm(h, p['g_pre_ffn']), conv_prev, p['w_up'], p['conv_w'], p['conv_b'], p['w_down'])
    y = h + rmsnorm(f, p['g_post_ffn'])
    return y, rw[:, -1], wkv_new, v_rows, conv_new


def setup_inputs(seed: int = 0) -> dict:
    key = jax.random.key(seed)
    ks = iter(jax.random.split(key, 48))
    f32 = jnp.float32
    L, D, W, F2 = DEPTH, D_MODEL, MIX_WIDTH, 2 * D_FF
    nrm = lambda shape, s: jax.random.normal(next(ks), shape, f32) * s
    gain = lambda shape: 1.0 + nrm(shape, 0.05)
    return {
        'x_prompt': nrm((BATCH, SEQ, D), 1.0),
        'x_sample': nrm((DEC_BATCH, DEC_SEQ, D), 1.0),
        'mem_prompt': nrm((BATCH, MEM_LEN, D), 1.0),
        'state_wkv': nrm((L, DEC_BATCH, RWKV_HEADS, RWKV_HEAD_DIM, RWKV_HEAD_DIM), 0.3),
        'state_shift': nrm((L, DEC_BATCH, N_SHIFT), 1.0),
        'cache_mem_k': nrm((L, DEC_BATCH, MEM_LEN, XATTN_HEADS, XATTN_HEAD_DIM), 1.0),
        'cache_mem_v': nrm((L, DEC_BATCH, MEM_LEN, XATTN_HEADS, XATTN_HEAD_DIM), 1.0),
        'state_ffn_conv': nrm((L, DEC_BATCH, CONV_WIDTH - 1, F2), 1.0),
        'g_pre_mix': gain((L, D)),
        'w_in': nrm((L, D, N_IN), D ** -0.5),
        'mu_shift': jax.random.uniform(next(ks), (L, N_SHIFT), f32),
        'w0': jax.random.uniform(next(ks), (L, W), f32, -5.0, -0.5),
        'w_decay': nrm((L, DECAY_RANK, W), DECAY_RANK ** -0.5),
        'a0': nrm((L, W), 0.5),
        'w_aaa': nrm((L, AAA_RANK, W), AAA_RANK ** -0.5),
        'w_gate': nrm((L, GATE_RANK, W), GATE_RANK ** -0.5),
        'k_k': 0.85 + nrm((L, W), 0.05),
        'k_a': gain((L, W)),
        'r_k': nrm((L, RWKV_HEADS, RWKV_HEAD_DIM), 0.1),
        'lnx_w': gain((L, W)),
        'lnx_b': nrm((L, W), 0.02),
        'sgu_g': gain((L, W)),
        'sgu_b': nrm((L, W), 0.02),
        'w_s': nrm((L, SGU_GROUPS, SGU_CHUNK, SGU_CHUNK), SGU_CHUNK ** -0.5),
        'b_s': 1.0 + nrm((L, SGU_GROUPS, SGU_CHUNK), 0.1),
        'g_mem': gain((L, D)),
        'w_mem_k': nrm((L, D, W), D ** -0.5),
        'w_mem_v': nrm((L, D, W), D ** -0.5),
        'w_branch': nrm((L, N_BRANCH, W, D), W ** -0.5),
        'w_out': nrm((L, D, D), D ** -0.5),
        'g_post_mix': gain((L, D)),
        'g_pre_ffn': gain((L, D)),
        'w_up': nrm((L, D, F2), D ** -0.5),
        'conv_w': nrm((L, CONV_WIDTH, F2), CONV_WIDTH ** -0.5),
        'conv_b': nrm((L, F2), 0.02),
        'w_down': nrm((L, D_FF, D), D_FF ** -0.5),
        'g_post_ffn': gain((L, D)),
    }


def reference(x_prompt, x_sample, mem_prompt, state_wkv, state_shift, cache_mem_k, cache_mem_v, state_ffn_conv,
              g_pre_mix, w_in, mu_shift, w0, w_decay, a0, w_aaa, w_gate, k_k, k_a, r_k, lnx_w, lnx_b,
              sgu_g, sgu_b, w_s, b_s, g_mem, w_mem_k, w_mem_v, w_branch, w_out, g_post_mix, g_pre_ffn,
              w_up, conv_w, conv_b, w_down, g_post_ffn):
    y_p, y_s = x_prompt, x_sample
    B = x_prompt.shape[0]
    wkv_p_l, sh_p_l, mk_p_l, mv_p_l, cv_p_l = [], [], [], [], []
    wkv_s_l, sh_s_l, vr_s_l, cv_s_l = [], [], [], []
    for l in range(DEPTH):
        p = dict(g_pre_mix=g_pre_mix[l], w_in=w_in[l], mu_shift=mu_shift[l], w0=w0[l], w_decay=w_decay[l],
                 a0=a0[l], w_aaa=w_aaa[l], w_gate=w_gate[l], k_k=k_k[l], k_a=k_a[l], r_k=r_k[l],
                 lnx_w=lnx_w[l], lnx_b=lnx_b[l], sgu_g=sgu_g[l], sgu_b=sgu_b[l], w_s=w_s[l], b_s=b_s[l],
                 w_branch=w_branch[l], w_out=w_out[l], g_post_mix=g_post_mix[l], g_pre_ffn=g_pre_ffn[l],
                 w_up=w_up[l], conv_w=conv_w[l], conv_b=conv_b[l], w_down=w_down[l], g_post_ffn=g_post_ffn[l])
        mk_p, mv_p = memory_kv(mem_prompt, g_mem[l], w_mem_k[l], w_mem_v[l])
        shift0 = jnp.zeros((B, N_SHIFT), x_prompt.dtype)
        wkv0 = jnp.zeros((B, RWKV_HEADS, RWKV_HEAD_DIM, RWKV_HEAD_DIM), x_prompt.dtype)
        conv0 = jnp.zeros((B, CONV_WIDTH - 1, 2 * D_FF), x_prompt.dtype)
        y_p, sh_p, wkv_p, _, cv_p = layer(y_p, shift0, wkv0, mk_p, mv_p, conv0, p)
        y_s, sh_s, wkv_s, vr_s, cv_s = layer(y_s, state_shift[l], state_wkv[l], cache_mem_k[l], cache_mem_v[l],
                                             state_ffn_conv[l], p)
        wkv_p_l.append(wkv_p); sh_p_l.append(sh_p); mk_p_l.append(mk_p); mv_p_l.append(mv_p); cv_p_l.append(cv_p)
        wkv_s_l.append(wkv_s); sh_s_l.append(sh_s); vr_s_l.append(vr_s); cv_s_l.append(cv_s)
    new_wkv_prompt = jnp.stack(wkv_p_l)
    new_shift_prompt = jnp.stack(sh_p_l)
    new_mem_k_prompt = jnp.stack(mk_p_l)
    new_mem_v_prompt = jnp.stack(mv_p_l)
    new_ffn_conv_prompt = jnp.stack(cv_p_l)
    new_wkv_sample = jnp.stack(wkv_s_l)
    new_shift_sample = jnp.stack(sh_s_l)
    new_sgu_v_sample = jnp.stack(vr_s_l)
    new_ffn_conv_sample = jnp.stack(cv_s_l)
    return (y_p, y_s, new_wkv_prompt, new_shift_prompt, new_mem_k_prompt, new_mem_v_prompt, new_ffn_conv_prompt,
            new_wkv_sample, new_shift_sample, new_sgu_v_sample, new_ffn_conv_sample)
```

```python
import functools

import jax
import jax.numpy as jnp
from jax import lax
from jax.experimental import pallas as pl
from jax.experimental.pallas import tpu as pltpu

F32 = jnp.float32
BF16 = jnp.bfloat16
HIGHEST = lax.Precision.HIGHEST

LANES = 128
SUBLANES = 8
VMEM_LIMIT_BYTES = 56 * 1024 * 1024

RMS_EPS = 1e-6
LN_EPS = 1e-5
GN_EPS = 64e-5
KK_EPS = 1e-12
RWKV_CHUNK = 64


def _params(*semantics):
    return pltpu.CompilerParams(dimension_semantics=semantics, vmem_limit_bytes=VMEM_LIMIT_BYTES)


def _row_tile(rows, pref):
    t = min(rows, pref)
    while rows % t:
        t -= SUBLANES
    return t


def _col_tile(n, cap):
    best = n
    for t in range(LANES, min(n, cap) + 1, LANES):
        if n % t == 0:
            best = t
    return best


def _bdot(a, b):
    return jnp.dot(a.astype(BF16), b.astype(BF16), preferred_element_type=F32)


def _bdot_nt(a, b):
    return lax.dot_general(a.astype(BF16), b.astype(BF16), (((1,), (1,)), ((), ())),
                           preferred_element_type=F32)


def _fdot(a, b):
    return jnp.dot(a, b, precision=HIGHEST, preferred_element_type=F32)


def _rmsnorm_body(x_ref, g_ref, o_ref):
    x = x_ref[...]
    y = x * lax.rsqrt(jnp.mean(x * x, axis=-1, keepdims=True) + RMS_EPS)
    o_ref[...] = (y * g_ref[...]).astype(o_ref.dtype)


def _rmsnorm(x, g, out_dtype):
    rows, d = x.shape
    tm = _row_tile(rows, 512)
    return pl.pallas_call(
        _rmsnorm_body,
        grid=(rows // tm,),
        in_specs=[pl.BlockSpec((tm, d), lambda i: (i, 0)), pl.BlockSpec((1, d), lambda i: (0, 0))],
        out_specs=pl.BlockSpec((tm, d), lambda i: (i, 0)),
        out_shape=jax.ShapeDtypeStruct((rows, d), out_dtype),
        compiler_params=_params("parallel"),
        name="rmsnorm",
    )(x, g.reshape(1, d))


def _matmul_body(x_ref, w_ref, o_ref):
    o_ref[...] = jnp.dot(x_ref[...], w_ref[...], preferred_element_type=F32).astype(o_ref.dtype)


def _matmul(x, w, out_dtype=F32):
    rows, k = x.shape
    n = w.shape[1]
    tm = _row_tile(rows, 1024)
    tn = _col_tile(n, 2048)
    return pl.pallas_call(
        _matmul_body,
        grid=(rows // tm, n // tn),
        in_specs=[pl.BlockSpec((tm, k), lambda i, j: (i, 0)), pl.BlockSpec((k, tn), lambda i, j: (0, j))],
        out_specs=pl.BlockSpec((tm, tn), lambda i, j: (i, j)),
        out_shape=jax.ShapeDtypeStruct((rows, n), out_dtype),
        compiler_params=_params("parallel", "parallel"),
        name="matmul",
    )(x, w)


def _shift_rows(x, halo, k):
    rolled = pltpu.roll(x, k, 0)
    row = lax.broadcasted_iota(jnp.int32, (SUBLANES, x.shape[1]), 0)
    top = rolled[0:SUBLANES]
    for j in range(k):
        top = jnp.where(row == j, halo[SUBLANES - k + j:SUBLANES - k + j + 1, :], top)
    return jnp.concatenate([top, rolled[SUBLANES:]], axis=0)


def _head_sum(x, ones_ref):
    ones = ones_ref[...]
    hi = x.astype(BF16)
    lo = (x - hi.astype(F32)).astype(BF16)
    outs = []
    for j in range(x.shape[1] // LANES):
        sl = slice(j * LANES, (j + 1) * LANES)
        outs.append(jnp.dot(hi[:, sl], ones, preferred_element_type=F32)
                    + jnp.dot(lo[:, sl], ones, preferred_element_type=F32))
    return jnp.concatenate(outs, axis=1)


def _pad_state_rows(state):
    b, n, f = state.shape
    return jnp.pad(state, ((0, 0), (0, SUBLANES - n), (0, 0))).reshape(b * SUBLANES, f)


def _rwkv_prep_body(rw_ref, halo_ref, sp_ref, mu_ref, w0_ref, wd_ref, a0_ref, wa_ref, wg_ref,
                    kk_ref, ka_ref, rk_ref, ones_ref,
                    r_o, lw_o, k_o, v_o, a_o, b_o, g_o, bonus_o, *, seq_rows, tm, width):
    x = rw_ref[...]
    if seq_rows >= tm:
        is_start = (pl.program_id(0) * tm) % seq_rows == 0
        halo = jnp.where(is_start, pltpu.roll(sp_ref[...], SUBLANES - 1, 0), halo_ref[...])
        prev = _shift_rows(x, halo, 1)
    else:
        t = lax.broadcasted_iota(jnp.int32, x.shape, 0) & (seq_rows - 1)
        prev = jnp.where(t == 0, sp_ref[...], pltpu.roll(x, 1, 0))
    xs = x + mu_ref[...] * (prev - x)

    w = width
    r = xs[:, 0:w]
    k = xs[:, w:2 * w]
    v = xs[:, 2 * w:3 * w]
    x_wa = xs[:, 3 * w:3 * w + LANES]
    x_g = xs[:, 3 * w + LANES:3 * w + 2 * LANES]

    z = -(w0_ref[...] + _fdot(jnp.tanh(x_wa), wd_ref[...]))
    softplus = jnp.maximum(z, 0.0) + jnp.log(1.0 + jnp.exp(-jnp.abs(z)))
    lw = -jnp.exp(-softplus - 0.5)
    a_lr = jax.nn.sigmoid(a0_ref[...] + _fdot(x_wa, wa_ref[...]))
    g = _fdot(jax.nn.sigmoid(x_g), wg_ref[...])

    kk = k * kk_ref[...]
    k_mod = k * (1.0 + (a_lr - 1.0) * ka_ref[...])
    kk = kk / jnp.maximum(jnp.sqrt(_head_sum(kk * kk, ones_ref)), KK_EPS)
    bonus = _head_sum(r * k_mod * rk_ref[...], ones_ref) * v

    r_o[...] = r
    lw_o[...] = lw
    k_o[...] = k_mod
    v_o[...] = v
    a_o[...] = -kk
    b_o[...] = kk * a_lr
    g_o[...] = g
    bonus_o[...] = bonus


def _rwkv_prep(rw, shift_rows, seq_rows, prm):
    rows, n_shift = rw.shape
    w = prm["width"]
    tm = _row_tile(rows, 256)
    if seq_rows >= tm:
        assert seq_rows % tm == 0
        sp_spec = pl.BlockSpec((SUBLANES, n_shift), lambda i: ((i * tm) // seq_rows, 0))
    else:
        assert seq_rows == SUBLANES and tm % seq_rows == 0
        sp_spec = pl.BlockSpec((tm, n_shift), lambda i: (i, 0))
    row_spec = lambda c: pl.BlockSpec((1, c), lambda i: (0, 0))
    full = lambda a: pl.BlockSpec(a.shape, lambda i: (0, 0))
    out_spec = pl.BlockSpec((tm, w), lambda i: (i, 0))
    return pl.pallas_call(
        functools.partial(_rwkv_prep_body, seq_rows=seq_rows, tm=tm, width=w),
        grid=(rows // tm,),
        in_specs=[pl.BlockSpec((tm, n_shift), lambda i: (i, 0)),
                  pl.BlockSpec((SUBLANES, n_shift), lambda i: (jnp.maximum(i * (tm // SUBLANES) - 1, 0), 0)),
                  sp_spec, row_spec(n_shift), row_spec(w), full(prm["wd_pad"]), row_spec(w), full(prm["wa_pad"]),
                  full(prm["w_gate"]), row_spec(w), row_spec(w), row_spec(w), full(prm["ones2"])],
        out_specs=[out_spec] * 8,
        out_shape=[jax.ShapeDtypeStruct((rows, w), F32)] * 8,
        compiler_params=_params("parallel"),
        name="rwkv_prep",
    )(rw, rw, shift_rows, prm["mu_shift"], prm["w0"], prm["wd_pad"], prm["a0"], prm["wa_pad"], prm["w_gate"],
      prm["k_k"], prm["k_a"], prm["r_k"], prm["ones2"])


def _wkv_body(r_ref, lw_ref, k_ref, v_ref, a_ref, b_ref, h0_ref, y_ref, hout_ref, h_scr, *, chunk, pairs, width, hd):
    c = pl.program_id(1)

    @pl.when(c == 0)
    def _():
        h_scr[...] = h0_ref[0]

    cs = chunk
    lanes = LANES * pairs
    heads = 2 * pairs
    n = heads * cs
    log_c = cs.bit_length() - 1
    log_hd = hd.bit_length() - 1

    tri = (lax.broadcasted_iota(jnp.int32, (cs, cs), 0) >= lax.broadcasted_iota(jnp.int32, (cs, cs), 1)).astype(F32)
    row = lax.broadcasted_iota(jnp.int32, (n, lanes), 0)
    lane = lax.broadcasted_iota(jnp.int32, (n, lanes), 1)
    head_mask = (row >> log_c) == (lane >> log_hd)
    ri = lax.broadcasted_iota(jnp.int32, (n, n), 0)
    ci = lax.broadcasted_iota(jnp.int32, (n, n), 1)
    same_head = (ri >> log_c) == (ci >> log_c)
    rt = ri & (cs - 1)
    ct = ci & (cs - 1)
    strict = same_head & (ct < rt)
    incl = same_head & (ct <= rt)
    eye = (ri == ci).astype(F32)
    diag = (lax.broadcasted_iota(jnp.int32, (LANES, LANES), 0)
            == lax.broadcasted_iota(jnp.int32, (LANES, LANES), 1))

    def stack(x):
        return jnp.where(head_mask, jnp.concatenate([x] * heads, axis=0), 0.0)

    def fold(x):
        out = x[0:cs]
        for hidx in range(1, heads):
            out = out + x[hidx * cs:(hidx + 1) * cs]
        return out

    for s in range(width // lanes):
        ls = slice(s * lanes, (s + 1) * lanes)
        r = r_ref[:, ls]
        lw = lw_ref[:, ls]
        k = k_ref[:, ls]
        v = v_ref[:, ls]
        a = a_ref[:, ls]
        b = b_ref[:, ls]

        cum = _fdot(tri, lw)
        total = cum[cs - 1:cs]
        e_cum = jnp.exp(cum)
        e_neg = jnp.exp(-cum)
        e_rest = jnp.exp(total - cum)
        a_st = stack(a * jnp.exp(cum - lw))
        r_st = stack(r * e_cum)
        b_st = stack(b * e_neg)
        k_st = stack(k * e_neg)
        v_st = stack(v)
        bh_st = stack(b * e_rest)
        kh_st = stack(k * e_rest)

        l_ab = jnp.where(strict, _bdot_nt(a_st, b_st), 0.0)
        l_ak = jnp.where(strict, _bdot_nt(a_st, k_st), 0.0)
        m_rb = jnp.where(incl, _bdot_nt(r_st, b_st), 0.0)
        m_rk = jnp.where(incl, _bdot_nt(r_st, k_st), 0.0)

        t_inv = eye + l_ab
        x_pow = l_ab
        for _ in range(log_c - 1):
            x_pow = _bdot(x_pow, x_pow)
            t_inv = t_inv + _bdot(t_inv, x_pow)

        a_bar = _bdot(t_inv, a_st)
        w_vec = _bdot(t_inv, _bdot(l_ak, v_st))
        r_bar = fold(r_st + _bdot(m_rb, a_bar))
        y0 = fold(_bdot(m_rb, w_vec) + _bdot(m_rk, v_st))

        for p in range(pairs):
            pl_ = slice(p * LANES, (p + 1) * LANES)
            out_l = slice(s * lanes + p * LANES, s * lanes + (p + 1) * LANES)
            pi = s * pairs + p
            h = h_scr[pi]
            y_ref[:, out_l] = _bdot(r_bar[:, pl_], h) + y0[:, pl_]
            bh_t = bh_st[:, pl_].T
            kh_t = kh_st[:, pl_].T
            phi = jnp.where(diag, jnp.exp(total[:, pl_]), 0.0) + _bdot(bh_t, a_bar[:, pl_])
            psi = _bdot(bh_t, w_vec[:, pl_]) + _bdot(kh_t, v_st[:, pl_])
            h_scr[pi] = _bdot(phi, h) + psi

    @pl.when(c == pl.num_programs(1) - 1)
    def _():
        hout_ref[0] = h_scr[...]


def _wkv(r, lw, k, v, a, b, h0, batch, seq_len, hd):
    rows, w = r.shape
    chunk = min(seq_len, RWKV_CHUNK)
    assert seq_len % chunk == 0 and chunk % SUBLANES == 0 and LANES % chunk == 0
    pairs = LANES // (2 * chunk)
    n_chunks = seq_len // chunk
    n_pairs = w // LANES
    vec = pl.BlockSpec((chunk, w), lambda bi, ci: (bi * n_chunks + ci, 0))
    st = pl.BlockSpec((1, n_pairs, LANES, LANES), lambda bi, ci: (bi, 0, 0, 0))
    return pl.pallas_call(
        functools.partial(_wkv_body, chunk=chunk, pairs=pairs, width=w, hd=hd),
        grid=(batch, n_chunks),
        in_specs=[vec] * 6 + [st],
        out_specs=[vec, st],
        out_shape=[jax.ShapeDtypeStruct((rows, w), F32),
                   jax.ShapeDtypeStruct((batch, n_pairs, LANES, LANES), F32)],
        scratch_shapes=[pltpu.VMEM((n_pairs, LANES, LANES), F32)],
        compiler_params=_params("parallel", "arbitrary"),
        name="rwkv_chunks",
    )(r, lw, k, v, a, b, h0)


def _state_to_pairs(s):
    b, h, hd, _ = s.shape
    st = jnp.swapaxes(s, -1, -2).reshape(b, h // 2, 2, hd, 1, hd)
    sel = jnp.eye(2, dtype=bool).reshape(1, 1, 2, 1, 2, 1)
    return jnp.where(sel, st, 0.0).reshape(b, h // 2, 2 * hd, 2 * hd)


def _pairs_to_state(hp, hd):
    b, p, _, _ = hp.shape
    x = hp.reshape(b, p, 2, hd, 2, hd)
    blocks = jnp.stack([x[:, :, 0, :, 0, :], x[:, :, 1, :, 1, :]], axis=2)
    return jnp.swapaxes(blocks, -1, -2).reshape(b, 2 * p, hd, hd)


def _rwkv_post_body(y_ref, bonus_ref, g_ref, lw_ref, lb_ref, ones_ref, o_ref, *, hd):
    y = y_ref[...]
    mu = _head_sum(y, ones_ref) * (1.0 / hd)
    d = y - mu
    var = _head_sum(d * d, ones_ref) * (1.0 / hd)
    yn = d * lax.rsqrt(var + GN_EPS) * lw_ref[...] + lb_ref[...]
    o_ref[...] = ((yn + bonus_ref[...]) * g_ref[...]).astype(o_ref.dtype)


def _rwkv_post(y, bonus, g, prm, hd):
    rows, w = y.shape
    tm = _row_tile(rows, 512)
    blk = pl.BlockSpec((tm, w), lambda i: (i, 0))
    row_spec = pl.BlockSpec((1, w), lambda i: (0, 0))
    return pl.pallas_call(
        functools.partial(_rwkv_post_body, hd=hd),
        grid=(rows // tm,),
        in_specs=[blk, blk, blk, row_spec, row_spec, pl.BlockSpec((LANES, LANES), lambda i: (0, 0))],
        out_specs=blk,
        out_shape=jax.ShapeDtypeStruct((rows, w), BF16),
        compiler_params=_params("parallel"),
        name="rwkv_post",
    )(y, bonus, g, prm["lnx_w"], prm["lnx_b"], prm["ones2"])


def _sgu_body(gu_ref, gv_ref, sg_ref, sb_ref, wm_ref, bias_ref, o_ref, v_ref, *, groups):
    u = jax.nn.gelu(gu_ref[...])
    vf = jax.nn.gelu(gv_ref[...])
    mu = jnp.mean(vf, axis=-1, keepdims=True)
    d = vf - mu
    var = jnp.mean(d * d, axis=-1, keepdims=True)
    v = (d * lax.rsqrt(var + LN_EPS)) * sg_ref[...] + sb_ref[...]
    v_ref[...] = v
    rows = v.shape[0]
    gd = v.shape[1] // groups
    causal = (lax.broadcasted_iota(jnp.int32, (rows, rows), 0)
              >= lax.broadcasted_iota(jnp.int32, (rows, rows), 1))
    for g in range(groups):
        sl = slice(g * gd, (g + 1) * gd)
        s = _bdot(jnp.where(causal, wm_ref[g], 0.0), v[:, sl]) + bias_ref[g]
        o_ref[:, sl] = (u[:, sl] * s).astype(o_ref.dtype)


def _sgu(guvq, w_mix, bias, prm):
    rows = guvq.shape[0]
    w = prm["width"]
    groups, cs, _ = w_mix.shape
    row_spec = pl.BlockSpec((1, w), lambda i: (0, 0))
    out = pl.BlockSpec((cs, w), lambda i: (i, 0))
    return pl.pallas_call(
        functools.partial(_sgu_body, groups=groups),
        grid=(rows // cs,),
        in_specs=[pl.BlockSpec((cs, w), lambda i: (i, 0)), pl.BlockSpec((cs, w), lambda i: (i, 1)),
                  row_spec, row_spec,
                  pl.BlockSpec(w_mix.shape, lambda i: (0, 0, 0)), pl.BlockSpec(bias.shape, lambda i: (0, 0, 0))],
        out_specs=[out, out],
        out_shape=[jax.ShapeDtypeStruct((rows, w), BF16), jax.ShapeDtypeStruct((rows, w), F32)],
        compiler_params=_params("parallel"),
        name="sgu",
    )(guvq, guvq, prm["sgu_g"], prm["sgu_b"], w_mix, bias)


def _xattn_body(q_ref, k_ref, v_ref, o_ref, *, heads, seqs, tq, m):
    hd = q_ref.shape[1] // heads
    scale = hd ** -0.5
    for h in range(heads):
        sl = slice(h * hd, (h + 1) * hd)
        outs = []
        for i in range(seqs):
            qs = slice(i * tq, (i + 1) * tq)
            ms = slice(i * m, (i + 1) * m)
            s = _bdot_nt(q_ref[qs, sl], k_ref[ms, sl]) * scale
            e = jnp.exp(s - jnp.max(s, axis=-1, keepdims=True))
            p = e / jnp.sum(e, axis=-1, keepdims=True)
            outs.append(_bdot(p, v_ref[ms, sl]))
        o_ref[:, sl] = jnp.concatenate(outs, axis=0).astype(o_ref.dtype)


def _xattn(guvq, mem_k, mem_v, seq_len, w):
    batch, m, heads, hd = mem_k.shape
    rows = guvq.shape[0]
    tq = _row_tile(seq_len, 512)
    nq = seq_len // tq
    seqs = max(1, 32 // tq)
    assert batch % seqs == 0 and (seqs == 1 or nq == 1)
    kv = pl.BlockSpec((seqs * m, w), lambda bi, qi: (bi, 0))
    qo = lambda col: pl.BlockSpec((seqs * tq, w), lambda bi, qi: (bi * nq + qi, col))
    return pl.pallas_call(
        functools.partial(_xattn_body, heads=heads, seqs=seqs, tq=tq, m=m),
        grid=(batch // seqs, nq),
        in_specs=[qo(2), kv, kv],
        out_specs=qo(0),
        out_shape=jax.ShapeDtypeStruct((rows, w), BF16),
        compiler_params=_params("parallel", "parallel"),
        name="mem_xattn",
    )(guvq, mem_k.reshape(batch * m, w), mem_v.reshape(batch * m, w))


def _branch_body(a_ref, b_ref, c_ref, wb_ref, g0_ref, g1_ref, g2_ref, o_ref):
    acc = jax.nn.sigmoid(g0_ref[...]) * jnp.dot(a_ref[...], wb_ref[0], preferred_element_type=F32)
    acc = acc + jax.nn.sigmoid(g1_ref[...]) * jnp.dot(b_ref[...], wb_ref[1], preferred_element_type=F32)
    acc = acc + jax.nn.sigmoid(g2_ref[...]) * jnp.dot(c_ref[...], wb_ref[2], preferred_element_type=F32)
    o_ref[...] = acc.astype(o_ref.dtype)


def _branch_mix(a_out, b_out, c_out, w_branch, gt):
    rows, w = a_out.shape
    nb, _, d = w_branch.shape
    assert nb == 3
    tm = _row_tile(rows, 512)
    tn = _col_tile(d, 512)
    nj = d // tn
    br = pl.BlockSpec((tm, w), lambda i, j: (i, 0))
    gate = lambda n: pl.BlockSpec((tm, tn), lambda i, j: (i, j + n * nj))
    return pl.pallas_call(
        _branch_body,
        grid=(rows // tm, nj),
        in_specs=[br, br, br, pl.BlockSpec((nb, w, tn), lambda i, j: (0, 0, j)), gate(0), gate(1), gate(2)],
        out_specs=pl.BlockSpec((tm, tn), lambda i, j: (i, j)),
        out_shape=jax.ShapeDtypeStruct((rows, d), BF16),
        compiler_params=_params("parallel", "parallel"),
        name="branch_mix",
    )(a_out, b_out, c_out, w_branch, gt, gt, gt)


def _mm_res_norm_body(x_ref, w_ref, res_ref, g_ref, g2_ref, y_ref, *rest, nk, with_next):
    if with_next:
        yn_ref, acc_ref = rest
    else:
        (acc_ref,) = rest
    kk = pl.program_id(1)
    part = jnp.dot(x_ref[...], w_ref[...], preferred_element_type=F32)

    @pl.when(kk == 0)
    def _():
        acc_ref[...] = part

    @pl.when(kk > 0)
    def _():
        acc_ref[...] += part

    @pl.when(kk == nk - 1)
    def _():
        f = acc_ref[...]
        y = res_ref[...] + (f * lax.rsqrt(jnp.mean(f * f, axis=-1, keepdims=True) + RMS_EPS)) * g_ref[...]
        y_ref[...] = y
        if with_next:
            yn = y * lax.rsqrt(jnp.mean(y * y, axis=-1, keepdims=True) + RMS_EPS)
            yn_ref[...] = (yn * g2_ref[...]).astype(yn_ref.dtype)


def _mm_res_norm(x, w, res, g, g_next=None):
    rows, k = x.shape
    d = w.shape[1]
    with_next = g_next is not None
    tm = _row_tile(rows, 512)
    tk = _col_tile(k, 1536)
    nk = k // tk
    blk = pl.BlockSpec((tm, d), lambda i, kk: (i, 0))
    row_spec = pl.BlockSpec((1, d), lambda i, kk: (0, 0))
    out_specs = [blk, blk] if with_next else [blk]
    out_shape = [jax.ShapeDtypeStruct((rows, d), F32)]
    if with_next:
        out_shape.append(jax.ShapeDtypeStruct((rows, d), BF16))
    g2 = g_next if with_next else g
    outs = pl.pallas_call(
        functools.partial(_mm_res_norm_body, nk=nk, with_next=with_next),
        grid=(rows // tm, nk),
        in_specs=[pl.BlockSpec((tm, tk), lambda i, kk: (i, kk)), pl.BlockSpec((tk, d), lambda i, kk: (kk, 0)),
                  blk, row_spec, row_spec],
        out_specs=out_specs,
        out_shape=out_shape,
        scratch_shapes=[pltpu.VMEM((tm, d), F32)],
        compiler_params=_params("parallel", "arbitrary"),
        name="matmul_res_norm",
    )(x, w, res, g.reshape(1, d), g2.reshape(1, d))
    return outs if with_next else outs[0]


def _conv_gate_body(ug_ref, uv_ref, hg_ref, hv_ref, eg_ref, ev_ref, cwg_ref, cwv_ref, cbg_ref, cbv_ref, o_ref,
                    *, seq_rows, tm, taps):
    def conv(u_ref, halo_ref, e_ref, cw_ref, cb_ref):
        u = u_ref[...]
        cw = cw_ref[...]
        acc = cb_ref[...] + cw[taps - 1:taps] * u
        if seq_rows >= tm:
            is_start = (pl.program_id(0) * tm) % seq_rows == 0
            halo = jnp.where(is_start, pltpu.roll(e_ref[...], SUBLANES - (taps - 1), 0), halo_ref[...])
            for back in range(1, taps):
                acc = acc + cw[taps - 1 - back:taps - back] * _shift_rows(u, halo, back)
        else:
            t = lax.broadcasted_iota(jnp.int32, u.shape, 0) & (seq_rows - 1)
            e = e_ref[...]
            for back in range(1, taps):
                up_by = taps - 1 - back
                state = pltpu.roll(e, tm - up_by, 0) if up_by else e
                prev = jnp.where(t < back, state, pltpu.roll(u, back, 0))
                acc = acc + cw[taps - 1 - back:taps - back] * prev
        return acc

    gate = conv(ug_ref, hg_ref, eg_ref, cwg_ref, cbg_ref)
    val = conv(uv_ref, hv_ref, ev_ref, cwv_ref, cbv_ref)
    o_ref[...] = (jax.nn.gelu(gate) * val).astype(o_ref.dtype)


def _conv_gate(up, state_rows, conv_w, conv_b, seq_rows):
    rows, f2 = up.shape
    dff = f2 // 2
    taps = conv_w.shape[0]
    tm = _row_tile(rows, 512)
    tn = _col_tile(dff, 512)
    nj = dff // tn
    if seq_rows >= tm:
        assert seq_rows % tm == 0
        st = lambda off: pl.BlockSpec((SUBLANES, tn), lambda i, j: ((i * tm) // seq_rows, j + off))
    else:
        assert seq_rows == SUBLANES and tm % seq_rows == 0
        st = lambda off: pl.BlockSpec((tm, tn), lambda i, j: (i, j + off))
    main = lambda off: pl.BlockSpec((tm, tn), lambda i, j: (i, j + off))
    halo = lambda off: pl.BlockSpec((SUBLANES, tn), lambda i, j: (jnp.maximum(i * (tm // SUBLANES) - 1, 0), j + off))
    cw = lambda off: pl.BlockSpec((taps, tn), lambda i, j: (0, j + off))
    cb = lambda off: pl.BlockSpec((1, tn), lambda i, j: (0, j + off))
    return pl.pallas_call(
        functools.partial(_conv_gate_body, seq_rows=seq_rows, tm=tm, taps=taps),
        grid=(rows // tm, nj),
        in_specs=[main(0), main(nj), halo(0), halo(nj), st(0), st(nj), cw(0), cw(nj), cb(0), cb(nj)],
        out_specs=pl.BlockSpec((tm, tn), lambda i, j: (i, j)),
        out_shape=jax.ShapeDtypeStruct((rows, dff), BF16),
        compiler_params=_params("parallel", "parallel"),
        name="conv_gate",
    )(up, up, up, up, state_rows, state_rows, conv_w, conv_w, conv_b.reshape(1, f2), conv_b.reshape(1, f2))


def _layer(x, shift_prev, wkv0, mem_k, mem_v, conv_prev, prm):
    batch, seq_len, d = x.shape
    rows = batch * seq_len
    w = prm["width"]
    hd = prm["head_dim"]
    x2 = x.reshape(rows, d)

    xn = _rmsnorm(x2, prm["g_pre_mix"], BF16)
    rw = _matmul(xn, prm["w_in_rw"])
    guvq = _matmul(xn, prm["w_in_guvq"])
    gt = _matmul(xn, prm["w_in_gt"])
    new_shift = rw.reshape(batch, seq_len, -1)[:, -1]

    r, lw, k, v, a, b, g, bonus = _rwkv_prep(rw, _pad_state_rows(shift_prev[:, None, :]), seq_len, prm)
    y, h_out = _wkv(r, lw, k, v, a, b, _state_to_pairs(wkv0), batch, seq_len, hd)
    new_wkv = _pairs_to_state(h_out, hd)
    a_out = _rwkv_post(y, bonus, g, prm, hd)

    if seq_len % prm["sgu_chunk"] == 0:
        w_mix, bias = prm["w_s"], prm["sgu_bias_full"]
    else:
        assert seq_len == SUBLANES
        w_mix, bias = prm["w_s_short"], prm["sgu_bias_short"]
    b_out, v_rows = _sgu(guvq, w_mix, bias, prm)
    c_out = _xattn(guvq, mem_k, mem_v, seq_len, w)

    mix = _branch_mix(a_out, b_out, c_out, prm["w_branch"], gt)
    h, hn = _mm_res_norm(mix, prm["w_out"], x2, prm["g_post_mix"], prm["g_pre_ffn"])

    up = _matmul(hn, prm["w_up"])
    taps = prm["conv_w"].shape[0]
    conv_new = up.reshape(batch, seq_len, -1)[:, seq_len - (taps - 1):]
    act = _conv_gate(up, _pad_state_rows(conv_prev), prm["conv_w"], prm["conv_b"], seq_len)
    y_out = _mm_res_norm(act, prm["w_down"], h, prm["g_post_ffn"])
    return (y_out.reshape(batch, seq_len, d), new_shift, new_wkv,
            v_rows.reshape(batch, seq_len, w), conv_new)


def _prepare(l, g_pre_mix, w_in, mu_shift, w0, w_decay, a0, w_aaa, w_gate, k_k, k_a, r_k, lnx_w, lnx_b,
             sgu_g, sgu_b, w_s, b_s, w_branch, w_out, g_post_mix, g_pre_ffn, w_up, conv_w, conv_b, w_down,
             g_post_ffn):
    heads, hd = r_k.shape[1], r_k.shape[2]
    w = heads * hd
    d = w_in.shape[1]
    n_shift = mu_shift.shape[1]
    rank_d, rank_a, rank_g = w_decay.shape[1], w_aaa.shape[1], w_gate.shape[1]
    assert rank_d + rank_a == LANES and rank_g == LANES and n_shift == 3 * w + 2 * LANES
    groups, sgu_chunk, _ = w_s.shape[1:]
    row = lambda t: t[l].reshape(1, -1)
    win = w_in[l]
    lane_head = jnp.arange(LANES) // hd
    seqs = 2 * LANES // SUBLANES
    w8 = w_s[l][:, :SUBLANES, :SUBLANES]
    blk = jnp.eye(seqs, dtype=bool)[:, None, :, None]
    w_short = jnp.where(blk[None], w8[:, None, :, None, :], 0.0).reshape(groups, seqs * SUBLANES, seqs * SUBLANES)
    gd = w // groups
    return dict(
        width=w, head_dim=hd, sgu_chunk=sgu_chunk,
        g_pre_mix=g_pre_mix[l],
        w_in_rw=win[:, :n_shift].astype(BF16),
        w_in_guvq=win[:, n_shift:n_shift + 3 * w].astype(BF16),
        w_in_gt=win[:, n_shift + 3 * w:].astype(BF16),
        mu_shift=row(mu_shift), w0=row(w0), a0=row(a0),
        wd_pad=jnp.pad(w_decay[l], ((0, rank_a), (0, 0))),
        wa_pad=jnp.pad(w_aaa[l], ((rank_d, 0), (0, 0))),
        w_gate=w_gate[l],
        k_k=row(k_k), k_a=row(k_a), r_k=row(r_k), lnx_w=row(lnx_w), lnx_b=row(lnx_b),
        ones2=(lane_head[:, None] == lane_head[None, :]).astype(BF16),
        sgu_g=row(sgu_g), sgu_b=row(sgu_b),
        w_s=w_s[l],
        sgu_bias_full=jnp.broadcast_to(b_s[l][:, :, None], (groups, sgu_chunk, gd)),
        w_s_short=w_short,
        sgu_bias_short=jnp.broadcast_to(jnp.tile(b_s[l][:, :SUBLANES], (1, seqs))[:, :, None],
                                        (groups, seqs * SUBLANES, gd)),
        w_branch=w_branch[l].astype(BF16), w_out=w_out[l].astype(BF16),
        g_post_mix=g_post_mix[l], g_pre_ffn=g_pre_ffn[l],
        w_up=w_up[l].astype(BF16), conv_w=conv_w[l], conv_b=conv_b[l], w_down=w_down[l].astype(BF16),
        g_post_ffn=g_post_ffn[l],
    )


def kernel(x_prompt, x_sample, mem_prompt, state_wkv, state_shift, cache_mem_k, cache_mem_v, state_ffn_conv, g_pre_mix, w_in, mu_shift, w0, w_decay, a0, w_aaa, w_gate, k_k, k_a, r_k, lnx_w, lnx_b, sgu_g, sgu_b, w_s, b_s, g_mem, w_mem_k, w_mem_v, w_branch, w_out, g_post_mix, g_pre_ffn, w_up, conv_w, conv_b, w_down, g_post_ffn):
    depth = w_in.shape[0]
    batch = x_prompt.shape[0]
    mem_len, d = mem_prompt.shape[1], mem_prompt.shape[2]
    xh, xhd = cache_mem_k.shape[3], cache_mem_k.shape[4]
    heads, hd = r_k.shape[1], r_k.shape[2]
    n_shift = mu_shift.shape[1]
    f2 = w_up.shape[2]
    taps = conv_w.shape[1]
    y_p, y_s = x_prompt, x_sample
    outs = [[] for _ in range(9)]
    for l in range(depth):
        prm = _prepare(l, g_pre_mix, w_in, mu_shift, w0, w_decay, a0, w_aaa, w_gate, k_k, k_a, r_k, lnx_w, lnx_b,
                       sgu_g, sgu_b, w_s, b_s, w_branch, w_out, g_post_mix, g_pre_ffn, w_up, conv_w, conv_b,
                       w_down, g_post_ffn)
        mn = _rmsnorm(mem_prompt.reshape(batch * mem_len, d), g_mem[l], BF16)
        mk_p = _matmul(mn, w_mem_k[l].astype(BF16)).reshape(batch, mem_len, xh, xhd)
        mv_p = _matmul(mn, w_mem_v[l].astype(BF16)).reshape(batch, mem_len, xh, xhd)
        y_p, sh_p, wkv_p, _, cv_p = _layer(
            y_p, jnp.zeros((batch, n_shift), F32), jnp.zeros((batch, heads, hd, hd), F32), mk_p, mv_p,
            jnp.zeros((batch, taps - 1, f2), F32), prm)
        y_s, sh_s, wkv_s, vr_s, cv_s = _layer(y_s, state_shift[l], state_wkv[l], cache_mem_k[l], cache_mem_v[l],
                                              state_ffn_conv[l], prm)
        for lst, val in zip(outs, (wkv_p, sh_p, mk_p, mv_p, cv_p, wkv_s, sh_s, vr_s, cv_s)):
            lst.append(val)
    return (y_p, y_s) + tuple(jnp.stack(lst) for lst in outs)
```

```python
import functools

import jax
import jax.numpy as jnp
from jax import lax
from jax.experimental import pallas as pl
from jax.experimental.pallas import tpu as pltpu

F32 = jnp.float32
BF16 = jnp.bfloat16
HIGHEST = lax.Precision.HIGHEST

LANES = 128
SUBLANES = 8
VMEM_LIMIT_BYTES = 56 * 1024 * 1024

RMS_EPS = 1e-6
LN_EPS = 1e-5
GN_EPS = 64e-5
KK_EPS = 1e-12
RWKV_CHUNK = 64


def _params(*semantics):
    return pltpu.CompilerParams(dimension_semantics=semantics, vmem_limit_bytes=VMEM_LIMIT_BYTES)


def _row_tile(rows, pref):
    t = min(rows, pref)
    while rows % t:
        t -= SUBLANES
    return t


def _col_tile(n, cap):
    best = n
    for t in range(LANES, min(n, cap) + 1, LANES):
        if n % t == 0:
            best = t
    return best


def _bdot(a, b):
    return jnp.dot(a.astype(BF16), b.astype(BF16), preferred_element_type=F32)


def _bdot_nt(a, b):
    return lax.dot_general(a.astype(BF16), b.astype(BF16), (((1,), (1,)), ((), ())),
                           preferred_element_type=F32)


def _fdot(a, b):
    return jnp.dot(a, b, precision=HIGHEST, preferred_element_type=F32)


def _rmsnorm_body(x_ref, g_ref, o_ref):
    x = x_ref[...]
    y = x * lax.rsqrt(jnp.mean(x * x, axis=-1, keepdims=True) + RMS_EPS)
    o_ref[...] = (y * g_ref[...]).astype(o_ref.dtype)


def _rmsnorm(x, g, out_dtype):
    rows, d = x.shape
    tm = _row_tile(rows, 512)
    return pl.pallas_call(
        _rmsnorm_body,
        grid=(rows // tm,),
        in_specs=[pl.BlockSpec((tm, d), lambda i: (i, 0)), pl.BlockSpec((1, d), lambda i: (0, 0))],
        out_specs=pl.BlockSpec((tm, d), lambda i: (i, 0)),
        out_shape=jax.ShapeDtypeStruct((rows, d), out_dtype),
        compiler_params=_params("parallel"),
        name="rmsnorm",
    )(x, g.reshape(1, d))


def _matmul_body(x_ref, w_ref, o_ref):
    o_ref[...] = jnp.dot(x_ref[...], w_ref[...], preferred_element_type=F32).astype(o_ref.dtype)


def _matmul(x, w, out_dtype=F32):
    rows, k = x.shape
    n = w.shape[1]
    tm = _row_tile(rows, 1024)
    tn = _col_tile(n, 2048)
    return pl.pallas_call(
        _matmul_body,
        grid=(rows // tm, n // tn),
        in_specs=[pl.BlockSpec((tm, k), lambda i, j: (i, 0)), pl.BlockSpec((k, tn), lambda i, j: (0, j))],
        out_specs=pl.BlockSpec((tm, tn), lambda i, j: (i, j)),
        out_shape=jax.ShapeDtypeStruct((rows, n), out_dtype),
        compiler_params=_params("parallel", "parallel"),
        name="matmul",
    )(x, w)


def _shift_rows(x, halo, k):
    rolled = pltpu.roll(x, k, 0)
    row = lax.broadcasted_iota(jnp.int32, (SUBLANES, x.shape[1]), 0)
    top = rolled[0:SUBLANES]
    for j in range(k):
        top = jnp.where(row == j, halo[SUBLANES - k + j:SUBLANES - k + j + 1, :], top)
    return jnp.concatenate([top, rolled[SUBLANES:]], axis=0)


def _head_sum(x, ones_ref):
    ones = ones_ref[...]
    hi = x.astype(BF16)
    lo = (x - hi.astype(F32)).astype(BF16)
    outs = []
    for j in range(x.shape[1] // LANES):
        sl = slice(j * LANES, (j + 1) * LANES)
        outs.append(jnp.dot(hi[:, sl], ones, preferred_element_type=F32)
                    + jnp.dot(lo[:, sl], ones, preferred_element_type=F32))
    return jnp.concatenate(outs, axis=1)


def _pad_state_rows(state):
    b, n, f = state.shape
    return jnp.pad(state, ((0, 0), (0, SUBLANES - n), (0, 0))).reshape(b * SUBLANES, f)


def _rwkv_prep_body(rw_ref, halo_ref, sp_ref, mu_ref, w0_ref, wd_ref, a0_ref, wa_ref, wg_ref,
                    kk_ref, ka_ref, rk_ref, ones_ref,
                    r_o, lw_o, k_o, v_o, a_o, b_o, g_o, bonus_o, *, seq_rows, tm, width):
    x = rw_ref[...]
    if seq_rows >= tm:
        is_start = (pl.program_id(0) * tm) % seq_rows == 0
        halo = jnp.where(is_start, pltpu.roll(sp_ref[...], SUBLANES - 1, 0), halo_ref[...])
        prev = _shift_rows(x, halo, 1)
    else:
        t = lax.broadcasted_iota(jnp.int32, x.shape, 0) & (seq_rows - 1)
        prev = jnp.where(t == 0, sp_ref[...], pltpu.roll(x, 1, 0))
    xs = x + mu_ref[...] * (prev - x)

    w = width
    r = xs[:, 0:w]
    k = xs[:, w:2 * w]
    v = xs[:, 2 * w:3 * w]
    x_wa = xs[:, 3 * w:3 * w + LANES]
    x_g = xs[:, 3 * w + LANES:3 * w + 2 * LANES]

    z = -(w0_ref[...] + _fdot(jnp.tanh(x_wa), wd_ref[...]))
    softplus = jnp.maximum(z, 0.0) + jnp.log(1.0 + jnp.exp(-jnp.abs(z)))
    lw = -jnp.exp(-softplus - 0.5)
    a_lr = jax.nn.sigmoid(a0_ref[...] + _fdot(x_wa, wa_ref[...]))
    g = _fdot(jax.nn.sigmoid(x_g), wg_ref[...])

    kk = k * kk_ref[...]
    k_mod = k * (1.0 + (a_lr - 1.0) * ka_ref[...])
    kk = kk / jnp.maximum(jnp.sqrt(_head_sum(kk * kk, ones_ref)), KK_EPS)
    bonus = _head_sum(r * k_mod * rk_ref[...], ones_ref) * v

    r_o[...] = r
    lw_o[...] = lw
    k_o[...] = k_mod
    v_o[...] = v
    a_o[...] = -kk
    b_o[...] = kk * a_lr
    g_o[...] = g
    bonus_o[...] = bonus


def _rwkv_prep(rw, shift_rows, seq_rows, prm):
    rows, n_shift = rw.shape
    w = prm["width"]
    tm = _row_tile(rows, 256)
    if seq_rows >= tm:
        assert seq_rows % tm == 0
        sp_spec = pl.BlockSpec((SUBLANES, n_shift), lambda i: ((i * tm) // seq_rows, 0))
    else:
        assert seq_rows == SUBLANES and tm % seq_rows == 0
        sp_spec = pl.BlockSpec((tm, n_shift), lambda i: (i, 0))
    row_spec = lambda c: pl.BlockSpec((1, c), lambda i: (0, 0))
    full = lambda a: pl.BlockSpec(a.shape, lambda i: (0, 0))
    out_spec = pl.BlockSpec((tm, w), lambda i: (i, 0))
    return pl.pallas_call(
        functools.partial(_rwkv_prep_body, seq_rows=seq_rows, tm=tm, width=w),
        grid=(rows // tm,),
        in_specs=[pl.BlockSpec((tm, n_shift), lambda i: (i, 0)),
                  pl.BlockSpec((SUBLANES, n_shift), lambda i: (jnp.maximum(i * (tm // SUBLANES) - 1, 0), 0)),
                  sp_spec, row_spec(n_shift), row_spec(w), full(prm["wd_pad"]), row_spec(w), full(prm["wa_pad"]),
                  full(prm["w_gate"]), row_spec(w), row_spec(w), row_spec(w), full(prm["ones2"])],
        out_specs=[out_spec] * 8,
        out_shape=[jax.ShapeDtypeStruct((rows, w), F32)] * 8,
        compiler_params=_params("parallel"),
        name="rwkv_prep",
    )(rw, rw, shift_rows, prm["mu_shift"], prm["w0"], prm["wd_pad"], prm["a0"], prm["wa_pad"], prm["w_gate"],
      prm["k_k"], prm["k_a"], prm["r_k"], prm["ones2"])


def _wkv_body(r_ref, lw_ref, k_ref, v_ref, a_ref, b_ref, h0_ref, y_ref, hout_ref, h_scr,
              *, chunk, pairs, seqs, width, hd):
    c = pl.program_id(1)

    @pl.when(c == 0)
    def _():
        h_scr[...] = h0_ref[...]

    cs = chunk
    lanes = LANES * pairs
    heads = 2 * pairs
    n = heads * cs
    log_c = cs.bit_length() - 1
    log_hd = hd.bit_length() - 1
    n_stacks = width // lanes

    tri = (lax.broadcasted_iota(jnp.int32, (cs, cs), 0) >= lax.broadcasted_iota(jnp.int32, (cs, cs), 1)).astype(F32)
    row = lax.broadcasted_iota(jnp.int32, (n, lanes), 0)
    lane = lax.broadcasted_iota(jnp.int32, (n, lanes), 1)
    head_mask = (row >> log_c) == (lane >> log_hd)
    ri = lax.broadcasted_iota(jnp.int32, (n, n), 0)
    ci = lax.broadcasted_iota(jnp.int32, (n, n), 1)
    same_head = (ri >> log_c) == (ci >> log_c)
    rt = ri & (cs - 1)
    ct = ci & (cs - 1)
    strict = same_head & (ct < rt)
    incl = same_head & (ct <= rt)
    eye = (ri == ci).astype(F32)
    diag = (lax.broadcasted_iota(jnp.int32, (LANES, LANES), 0)
            == lax.broadcasted_iota(jnp.int32, (LANES, LANES), 1))

    def stack(x):
        return jnp.where(head_mask, jnp.concatenate([x] * heads, axis=0), 0.0)

    def fold(x):
        out = x[0:cs]
        for hidx in range(1, heads):
            out = out + x[hidx * cs:(hidx + 1) * cs]
        return out

    scaled = []
    for i in range(seqs):
        rs = slice(i * cs, (i + 1) * cs)
        lw = lw_ref[rs, :]
        cum = _fdot(tri, lw)
        total = cum[cs - 1:cs]
        e_cum = jnp.exp(cum)
        e_neg = jnp.exp(-cum)
        e_rest = jnp.exp(total - cum)
        b = b_ref[rs, :]
        k = k_ref[rs, :]
        scaled.append(dict(a=a_ref[rs, :] * jnp.exp(cum - lw), r=r_ref[rs, :] * e_cum, b=b * e_neg, k=k * e_neg,
                           v=v_ref[rs, :], bh=b * e_rest, kh=k * e_rest, total=total))

    inst = [(i, s) for i in range(seqs) for s in range(n_stacks)]
    jj = range(len(inst))

    def stacked(name):
        return [stack(scaled[i][name][:, s * lanes:(s + 1) * lanes]) for i, s in inst]

    a_st, r_st, b_st, k_st, v_st, bh_st, kh_st = (stacked(nm) for nm in ("a", "r", "b", "k", "v", "bh", "kh"))

    gram = [_bdot_nt(jnp.concatenate([a_st[j], r_st[j]], axis=0), jnp.concatenate([b_st[j], k_st[j]], axis=0))
            for j in jj]
    l_ab = [jnp.where(strict, gram[j][:n, :n], 0.0) for j in jj]
    m_rb = [jnp.where(incl, gram[j][n:, :n], 0.0) for j in jj]
    l_ak_m_rk = [jnp.concatenate([jnp.where(strict, gram[j][:n, n:], 0.0), jnp.where(incl, gram[j][n:, n:], 0.0)],
                                 axis=0) for j in jj]

    t_inv = [eye + l_ab[j] for j in jj]
    x_pow = [_bdot(l_ab[j], l_ab[j]) for j in jj]
    for _ in range(log_c - 2):
        both = [_bdot(jnp.concatenate([x_pow[j], t_inv[j]], axis=0), x_pow[j]) for j in jj]
        t_inv = [t_inv[j] + both[j][n:] for j in jj]
        x_pow = [both[j][:n] for j in jj]
    t_inv = [t_inv[j] + _bdot(t_inv[j], x_pow[j]) for j in jj]

    lv = [_bdot(l_ak_m_rk[j], v_st[j]) for j in jj]
    aw = [_bdot(t_inv[j], jnp.concatenate([a_st[j], lv[j][:n]], axis=1)) for j in jj]
    rb = [_bdot(m_rb[j], aw[j]) for j in jj]
    r_bar = [fold(r_st[j] + rb[j][:, :lanes]) for j in jj]
    y0 = [fold(rb[j][:, lanes:] + lv[j][n:]) for j in jj]

    jp = [(j, p) for j in jj for p in range(pairs)]

    def pair_lanes(x, p, off=0):
        return x[:, off + p * LANES:off + (p + 1) * LANES]

    h = [h_scr[inst[j][0], inst[j][1] * pairs + p] for j, p in jp]
    yv = [_bdot(pair_lanes(r_bar[j], p), h[q]) + pair_lanes(y0[j], p) for q, (j, p) in enumerate(jp)]
    for q, (j, p) in enumerate(jp):
        i, s = inst[j]
        y_ref[i * cs:(i + 1) * cs, s * lanes + p * LANES:s * lanes + (p + 1) * LANES] = yv[q]
    bh_t = [pair_lanes(bh_st[j], p).T for j, p in jp]
    kh_t = [pair_lanes(kh_st[j], p).T for j, p in jp]
    pp = [_bdot(bh_t[q], jnp.concatenate([pair_lanes(aw[j], p), pair_lanes(aw[j], p, lanes)], axis=1))
          for q, (j, p) in enumerate(jp)]
    kv = [_bdot(kh_t[q], pair_lanes(v_st[j], p)) for q, (j, p) in enumerate(jp)]
    for q, (j, p) in enumerate(jp):
        i, s = inst[j]
        decay = pair_lanes(scaled[i]["total"], p, s * lanes)
        phi = jnp.where(diag, jnp.exp(decay), 0.0) + pp[q][:, :LANES]
        h_scr[i, s * pairs + p] = _bdot(phi, h[q]) + (pp[q][:, LANES:] + kv[q])

    @pl.when(c == pl.num_programs(1) - 1)
    def _():
        hout_ref[...] = h_scr[...]


def _wkv(r, lw, k, v, a, b, h0, batch, seq_len, hd):
    rows, w = r.shape
    chunk = min(seq_len, RWKV_CHUNK)
    assert seq_len % chunk == 0 and chunk % SUBLANES == 0 and LANES % chunk == 0
    pairs = LANES // (2 * chunk)
    n_chunks = seq_len // chunk
    n_pairs = w // LANES
    seqs = 2 if (n_chunks == 1 and batch % 2 == 0) else 1
    vec = pl.BlockSpec((seqs * chunk, w), lambda bi, ci: (bi * n_chunks + ci, 0))
    st = pl.BlockSpec((seqs, n_pairs, LANES, LANES), lambda bi, ci: (bi, 0, 0, 0))
    return pl.pallas_call(
        functools.partial(_wkv_body, chunk=chunk, pairs=pairs, seqs=seqs, width=w, hd=hd),
        grid=(batch // seqs, n_chunks),
        in_specs=[vec] * 6 + [st],
        out_specs=[vec, st],
        out_shape=[jax.ShapeDtypeStruct((rows, w), F32),
                   jax.ShapeDtypeStruct((batch, n_pairs, LANES, LANES), F32)],
        scratch_shapes=[pltpu.VMEM((seqs, n_pairs, LANES, LANES), F32)],
        compiler_params=_params("parallel", "arbitrary"),
        name="rwkv_chunks",
    )(r, lw, k, v, a, b, h0)


def _state_to_pairs(s):
    b, h, hd, _ = s.shape
    st = jnp.swapaxes(s, -1, -2).reshape(b, h // 2, 2, hd, 1, hd)
    sel = jnp.eye(2, dtype=bool).reshape(1, 1, 2, 1, 2, 1)
    return jnp.where(sel, st, 0.0).reshape(b, h // 2, 2 * hd, 2 * hd)


def _pairs_to_state(hp, hd):
    b, p, _, _ = hp.shape
    x = hp.reshape(b, p, 2, hd, 2, hd)
    blocks = jnp.stack([x[:, :, 0, :, 0, :], x[:, :, 1, :, 1, :]], axis=2)
    return jnp.swapaxes(blocks, -1, -2).reshape(b, 2 * p, hd, hd)


def _rwkv_post_body(y_ref, bonus_ref, g_ref, lw_ref, lb_ref, ones_ref, o_ref, *, hd):
    y = y_ref[...]
    mu = _head_sum(y, ones_ref) * (1.0 / hd)
    d = y - mu
    var = _head_sum(d * d, ones_ref) * (1.0 / hd)
    yn = d * lax.rsqrt(var + GN_EPS) * lw_ref[...] + lb_ref[...]
    o_ref[...] = ((yn + bonus_ref[...]) * g_ref[...]).astype(o_ref.dtype)


def _rwkv_post(y, bonus, g, prm, hd):
    rows, w = y.shape
    tm = _row_tile(rows, 512)
    blk = pl.BlockSpec((tm, w), lambda i: (i, 0))
    row_spec = pl.BlockSpec((1, w), lambda i: (0, 0))
    return pl.pallas_call(
        functools.partial(_rwkv_post_body, hd=hd),
        grid=(rows // tm,),
        in_specs=[blk, blk, blk, row_spec, row_spec, pl.BlockSpec((LANES, LANES), lambda i: (0, 0))],
        out_specs=blk,
        out_shape=jax.ShapeDtypeStruct((rows, w), BF16),
        compiler_params=_params("parallel"),
        name="rwkv_post",
    )(y, bonus, g, prm["lnx_w"], prm["lnx_b"], prm["ones2"])


def _sgu_body(gu_ref, gv_ref, sg_ref, sb_ref, wm_ref, bias_ref, o_ref, v_ref, *, groups):
    u = jax.nn.gelu(gu_ref[...])
    vf = jax.nn.gelu(gv_ref[...])
    mu = jnp.mean(vf, axis=-1, keepdims=True)
    d = vf - mu
    var = jnp.mean(d * d, axis=-1, keepdims=True)
    v = (d * lax.rsqrt(var + LN_EPS)) * sg_ref[...] + sb_ref[...]
    v_ref[...] = v
    rows = v.shape[0]
    gd = v.shape[1] // groups
    causal = (lax.broadcasted_iota(jnp.int32, (rows, rows), 0)
              >= lax.broadcasted_iota(jnp.int32, (rows, rows), 1))
    for g in range(groups):
        sl = slice(g * gd, (g + 1) * gd)
        s = _bdot(jnp.where(causal, wm_ref[g], 0.0), v[:, sl]) + bias_ref[g]
        o_ref[:, sl] = (u[:, sl] * s).astype(o_ref.dtype)


def _sgu(guvq, w_mix, bias, prm):
    rows = guvq.shape[0]
    w = prm["width"]
    groups, cs, _ = w_mix.shape
    row_spec = pl.BlockSpec((1, w), lambda i: (0, 0))
    out = pl.BlockSpec((cs, w), lambda i: (i, 0))
    return pl.pallas_call(
        functools.partial(_sgu_body, groups=groups),
        grid=(rows // cs,),
        in_specs=[pl.BlockSpec((cs, w), lambda i: (i, 0)), pl.BlockSpec((cs, w), lambda i: (i, 1)),
                  row_spec, row_spec,
                  pl.BlockSpec(w_mix.shape, lambda i: (0, 0, 0)), pl.BlockSpec(bias.shape, lambda i: (0, 0, 0))],
        out_specs=[out, out],
        out_shape=[jax.ShapeDtypeStruct((rows, w), BF16), jax.ShapeDtypeStruct((rows, w), F32)],
        compiler_params=_params("parallel"),
        name="sgu",
    )(guvq, guvq, prm["sgu_g"], prm["sgu_b"], w_mix, bias)


def _xattn_body(q_ref, k_ref, v_ref, o_ref, *, heads, seqs, tq, m):
    hd = q_ref.shape[1] // heads
    scale = hd ** -0.5
    for h in range(heads):
        sl = slice(h * hd, (h + 1) * hd)
        outs = []
        for i in range(seqs):
            qs = slice(i * tq, (i + 1) * tq)
            ms = slice(i * m, (i + 1) * m)
            s = _bdot_nt(q_ref[qs, sl], k_ref[ms, sl]) * scale
            e = jnp.exp(s - jnp.max(s, axis=-1, keepdims=True))
            p = e / jnp.sum(e, axis=-1, keepdims=True)
            outs.append(_bdot(p, v_ref[ms, sl]))
        o_ref[:, sl] = jnp.concatenate(outs, axis=0).astype(o_ref.dtype)


def _xattn(guvq, mem_k, mem_v, seq_len, w):
    batch, m, heads, hd = mem_k.shape
    rows = guvq.shape[0]
    tq = _row_tile(seq_len, 512)
    nq = seq_len // tq
    seqs = max(1, 32 // tq)
    assert batch % seqs == 0 and (seqs == 1 or nq == 1)
    kv = pl.BlockSpec((seqs * m, w), lambda bi, qi: (bi, 0))
    qo = lambda col: pl.BlockSpec((seqs * tq, w), lambda bi, qi: (bi * nq + qi, col))
    return pl.pallas_call(
        functools.partial(_xattn_body, heads=heads, seqs=seqs, tq=tq, m=m),
        grid=(batch // seqs, nq),
        in_specs=[qo(2), kv, kv],
        out_specs=qo(0),
        out_shape=jax.ShapeDtypeStruct((rows, w), BF16),
        compiler_params=_params("parallel", "parallel"),
        name="mem_xattn",
    )(guvq, mem_k.reshape(batch * m, w), mem_v.reshape(batch * m, w))


def _branch_body(a_ref, b_ref, c_ref, wb_ref, g0_ref, g1_ref, g2_ref, o_ref):
    acc = jax.nn.sigmoid(g0_ref[...]) * jnp.dot(a_ref[...], wb_ref[0], preferred_element_type=F32)
    acc = acc + jax.nn.sigmoid(g1_ref[...]) * jnp.dot(b_ref[...], wb_ref[1], preferred_element_type=F32)
    acc = acc + jax.nn.sigmoid(g2_ref[...]) * jnp.dot(c_ref[...], wb_ref[2], preferred_element_type=F32)
    o_ref[...] = acc.astype(o_ref.dtype)


def _branch_mix(a_out, b_out, c_out, w_branch, gt):
    rows, w = a_out.shape
    nb, _, d = w_branch.shape
    assert nb == 3
    tm = _row_tile(rows, 512)
    tn = _col_tile(d, 512)
    nj = d // tn
    br = pl.BlockSpec((tm, w), lambda i, j: (i, 0))
    gate = lambda n: pl.BlockSpec((tm, tn), lambda i, j: (i, j + n * nj))
    return pl.pallas_call(
        _branch_body,
        grid=(rows // tm, nj),
        in_specs=[br, br, br, pl.BlockSpec((nb, w, tn), lambda i, j: (0, 0, j)), gate(0), gate(1), gate(2)],
        out_specs=pl.BlockSpec((tm, tn), lambda i, j: (i, j)),
        out_shape=jax.ShapeDtypeStruct((rows, d), BF16),
        compiler_params=_params("parallel", "parallel"),
        name="branch_mix",
    )(a_out, b_out, c_out, w_branch, gt, gt, gt)


def _mm_res_norm_body(x_ref, w_ref, res_ref, g_ref, g2_ref, y_ref, *rest, nk, with_next):
    if with_next:
        yn_ref, acc_ref = rest
    else:
        (acc_ref,) = rest
    kk = pl.program_id(1)
    part = jnp.dot(x_ref[...], w_ref[...], preferred_element_type=F32)

    @pl.when(kk == 0)
    def _():
        acc_ref[...] = part

    @pl.when(kk > 0)
    def _():
        acc_ref[...] += part

    @pl.when(kk == nk - 1)
    def _():
        f = acc_ref[...]
        y = res_ref[...] + (f * lax.rsqrt(jnp.mean(f * f, axis=-1, keepdims=True) + RMS_EPS)) * g_ref[...]
        y_ref[...] = y
        if with_next:
            yn = y * lax.rsqrt(jnp.mean(y * y, axis=-1, keepdims=True) + RMS_EPS)
            yn_ref[...] = (yn * g2_ref[...]).astype(yn_ref.dtype)


def _mm_res_norm(x, w, res, g, g_next=None):
    rows, k = x.shape
    d = w.shape[1]
    with_next = g_next is not None
    tm = _row_tile(rows, 512)
    tk = _col_tile(k, 1536)
    nk = k // tk
    blk = pl.BlockSpec((tm, d), lambda i, kk: (i, 0))
    row_spec = pl.BlockSpec((1, d), lambda i, kk: (0, 0))
    out_specs = [blk, blk] if with_next else [blk]
    out_shape = [jax.ShapeDtypeStruct((rows, d), F32)]
    if with_next:
        out_shape.append(jax.ShapeDtypeStruct((rows, d), BF16))
    g2 = g_next if with_next else g
    outs = pl.pallas_call(
        functools.partial(_mm_res_norm_body, nk=nk, with_next=with_next),
        grid=(rows // tm, nk),
        in_specs=[pl.BlockSpec((tm, tk), lambda i, kk: (i, kk)), pl.BlockSpec((tk, d), lambda i, kk: (kk, 0)),
                  blk, row_spec, row_spec],
        out_specs=out_specs,
        out_shape=out_shape,
        scratch_shapes=[pltpu.VMEM((tm, d), F32)],
        compiler_params=_params("parallel", "arbitrary"),
        name="matmul_res_norm",
    )(x, w, res, g.reshape(1, d), g2.reshape(1, d))
    return outs if with_next else outs[0]


def _conv_gate_body(ug_ref, uv_ref, hg_ref, hv_ref, eg_ref, ev_ref, cwg_ref, cwv_ref, cbg_ref, cbv_ref, o_ref,
                    *, seq_rows, tm, taps):
    def conv(u_ref, halo_ref, e_ref, cw_ref, cb_ref):
        u = u_ref[...]
        cw = cw_ref[...]
        acc = cb_ref[...] + cw[taps - 1:taps] * u
        if seq_rows >= tm:
            is_start = (pl.program_id(0) * tm) % seq_rows == 0
            halo = jnp.where(is_start, pltpu.roll(e_ref[...], SUBLANES - (taps - 1), 0), halo_ref[...])
            for back in range(1, taps):
                acc = acc + cw[taps - 1 - back:taps - back] * _shift_rows(u, halo, back)
        else:
            t = lax.broadcasted_iota(jnp.int32, u.shape, 0) & (seq_rows - 1)
            e = e_ref[...]
            for back in range(1, taps):
                up_by = taps - 1 - back
                state = pltpu.roll(e, tm - up_by, 0) if up_by else e
                prev = jnp.where(t < back, state, pltpu.roll(u, back, 0))
                acc = acc + cw[taps - 1 - back:taps - back] * prev
        return acc

    gate = conv(ug_ref, hg_ref, eg_ref, cwg_ref, cbg_ref)
    val = conv(uv_ref, hv_ref, ev_ref, cwv_ref, cbv_ref)
    o_ref[...] = (jax.nn.gelu(gate) * val).astype(o_ref.dtype)


def _conv_gate(up, state_rows, conv_w, conv_b, seq_rows):
    rows, f2 = up.shape
    dff = f2 // 2
    taps = conv_w.shape[0]
    tm = _row_tile(rows, 512)
    tn = _col_tile(dff, 512)
    nj = dff // tn
    if seq_rows >= tm:
        assert seq_rows % tm == 0
        st = lambda off: pl.BlockSpec((SUBLANES, tn), lambda i, j: ((i * tm) // seq_rows, j + off))
    else:
        assert seq_rows == SUBLANES and tm % seq_rows == 0
        st = lambda off: pl.BlockSpec((tm, tn), lambda i, j: (i, j + off))
    main = lambda off: pl.BlockSpec((tm, tn), lambda i, j: (i, j + off))
    halo = lambda off: pl.BlockSpec((SUBLANES, tn), lambda i, j: (jnp.maximum(i * (tm // SUBLANES) - 1, 0), j + off))
    cw = lambda off: pl.BlockSpec((taps, tn), lambda i, j: (0, j + off))
    cb = lambda off: pl.BlockSpec((1, tn), lambda i, j: (0, j + off))
    return pl.pallas_call(
        functools.partial(_conv_gate_body, seq_rows=seq_rows, tm=tm, taps=taps),
        grid=(rows // tm, nj),
        in_specs=[main(0), main(nj), halo(0), halo(nj), st(0), st(nj), cw(0), cw(nj), cb(0), cb(nj)],
        out_specs=pl.BlockSpec((tm, tn), lambda i, j: (i, j)),
        out_shape=jax.ShapeDtypeStruct((rows, dff), BF16),
        compiler_params=_params("parallel", "parallel"),
        name="conv_gate",
    )(up, up, up, up, state_rows, state_rows, conv_w, conv_w, conv_b.reshape(1, f2), conv_b.reshape(1, f2))


def _layer(x, shift_prev, wkv0, mem_k, mem_v, conv_prev, prm):
    batch, seq_len, d = x.shape
    rows = batch * seq_len
    w = prm["width"]
    hd = prm["head_dim"]
    x2 = x.reshape(rows, d)

    xn = _rmsnorm(x2, prm["g_pre_mix"], BF16)
    rw = _matmul(xn, prm["w_in_rw"])
    guvq = _matmul(xn, prm["w_in_guvq"])
    gt = _matmul(xn, prm["w_in_gt"])
    new_shift = rw.reshape(batch, seq_len, -1)[:, -1]

    r, lw, k, v, a, b, g, bonus = _rwkv_prep(rw, _pad_state_rows(shift_prev[:, None, :]), seq_len, prm)
    y, h_out = _wkv(r, lw, k, v, a, b, _state_to_pairs(wkv0), batch, seq_len, hd)
    new_wkv = _pairs_to_state(h_out, hd)
    a_out = _rwkv_post(y, bonus, g, prm, hd)

    if seq_len % prm["sgu_chunk"] == 0:
        w_mix, bias = prm["w_s"], prm["sgu_bias_full"]
    else:
        assert seq_len == SUBLANES
        w_mix, bias = prm["w_s_short"], prm["sgu_bias_short"]
    b_out, v_rows = _sgu(guvq, w_mix, bias, prm)
    c_out = _xattn(guvq, mem_k, mem_v, seq_len, w)

    mix = _branch_mix(a_out, b_out, c_out, prm["w_branch"], gt)
    h, hn = _mm_res_norm(mix, prm["w_out"], x2, prm["g_post_mix"], prm["g_pre_ffn"])

    up = _matmul(hn, prm["w_up"])
    taps = prm["conv_w"].shape[0]
    conv_new = up.reshape(batch, seq_len, -1)[:, seq_len - (taps - 1):]
    act = _conv_gate(up, _pad_state_rows(conv_prev), prm["conv_w"], prm["conv_b"], seq_len)
    y_out = _mm_res_norm(act, prm["w_down"], h, prm["g_post_ffn"])
    return (y_out.reshape(batch, seq_len, d), new_shift, new_wkv,
            v_rows.reshape(batch, seq_len, w), conv_new)


def _prepare(l, g_pre_mix, w_in, mu_shift, w0, w_decay, a0, w_aaa, w_gate, k_k, k_a, r_k, lnx_w, lnx_b,
             sgu_g, sgu_b, w_s, b_s, w_branch, w_out, g_post_mix, g_pre_ffn, w_up, conv_w, conv_b, w_down,
             g_post_ffn):
    heads, hd = r_k.shape[1], r_k.shape[2]
    w = heads * hd
    d = w_in.shape[1]
    n_shift = mu_shift.shape[1]
    rank_d, rank_a, rank_g = w_decay.shape[1], w_aaa.shape[1], w_gate.shape[1]
    assert rank_d + rank_a == LANES and rank_g == LANES and n_shift == 3 * w + 2 * LANES
    groups, sgu_chunk, _ = w_s.shape[1:]
    row = lambda t: t[l].reshape(1, -1)
    win = w_in[l]
    lane_head = jnp.arange(LANES) // hd
    seqs = 2 * LANES // SUBLANES
    w8 = w_s[l][:, :SUBLANES, :SUBLANES]
    blk = jnp.eye(seqs, dtype=bool)[:, None, :, None]
    w_short = jnp.where(blk[None], w8[:, None, :, None, :], 0.0).reshape(groups, seqs * SUBLANES, seqs * SUBLANES)
    gd = w // groups
    return dict(
        width=w, head_dim=hd, sgu_chunk=sgu_chunk,
        g_pre_mix=g_pre_mix[l],
        w_in_rw=win[:, :n_shift].astype(BF16),
        w_in_guvq=win[:, n_shift:n_shift + 3 * w].astype(BF16),
        w_in_gt=win[:, n_shift + 3 * w:].astype(BF16),
        mu_shift=row(mu_shift), w0=row(w0), a0=row(a0),
        wd_pad=jnp.pad(w_decay[l], ((0, rank_a), (0, 0))),
        wa_pad=jnp.pad(w_aaa[l], ((rank_d, 0), (0, 0))),
        w_gate=w_gate[l],
        k_k=row(k_k), k_a=row(k_a), r_k=row(r_k), lnx_w=row(lnx_w), lnx_b=row(lnx_b),
        ones2=(lane_head[:, None] == lane_head[None, :]).astype(BF16),
        sgu_g=row(sgu_g), sgu_b=row(sgu_b),
        w_s=w_s[l],
        sgu_bias_full=jnp.broadcast_to(b_s[l][:, :, None], (groups, sgu_chunk, gd)),
        w_s_short=w_short,
        sgu_bias_short=jnp.broadcast_to(jnp.tile(b_s[l][:, :SUBLANES], (1, seqs))[:, :, None],
                                        (groups, seqs * SUBLANES, gd)),
        w_branch=w_branch[l].astype(BF16), w_out=w_out[l].astype(BF16),
        g_post_mix=g_post_mix[l], g_pre_ffn=g_pre_ffn[l],
        w_up=w_up[l].astype(BF16), conv_w=conv_w[l], conv_b=conv_b[l], w_down=w_down[l].astype(BF16),
        g_post_ffn=g_post_ffn[l],
    )


def kernel(x_prompt, x_sample, mem_prompt, state_wkv, state_shift, cache_mem_k, cache_mem_v, state_ffn_conv, g_pre_mix, w_in, mu_shift, w0, w_decay, a0, w_aaa, w_gate, k_k, k_a, r_k, lnx_w, lnx_b, sgu_g, sgu_b, w_s, b_s, g_mem, w_mem_k, w_mem_v, w_branch, w_out, g_post_mix, g_pre_ffn, w_up, conv_w, conv_b, w_down, g_post_ffn):
    depth = w_in.shape[0]
    batch = x_prompt.shape[0]
    mem_len, d = mem_prompt.shape[1], mem_prompt.shape[2]
    xh, xhd = cache_mem_k.shape[3], cache_mem_k.shape[4]
    heads, hd = r_k.shape[1], r_k.shape[2]
    n_shift = mu_shift.shape[1]
    f2 = w_up.shape[2]
    taps = conv_w.shape[1]
    y_p, y_s = x_prompt, x_sample
    outs = [[] for _ in range(9)]
    for l in range(depth):
        prm = _prepare(l, g_pre_mix, w_in, mu_shift, w0, w_decay, a0, w_aaa, w_gate, k_k, k_a, r_k, lnx_w, lnx_b,
                       sgu_g, sgu_b, w_s, b_s, w_branch, w_out, g_post_mix, g_pre_ffn, w_up, conv_w, conv_b,
                       w_down, g_post_ffn)
        mn = _rmsnorm(mem_prompt.reshape(batch * mem_len, d), g_mem[l], BF16)
        mk_p = _matmul(mn, w_mem_k[l].astype(BF16)).reshape(batch, mem_len, xh, xhd)
        mv_p = _matmul(mn, w_mem_v[l].astype(BF16)).reshape(batch, mem_len, xh, xhd)
        y_p, sh_p, wkv_p, _, cv_p = _layer(
            y_p, jnp.zeros((batch, n_shift), F32), jnp.zeros((batch, heads, hd, hd), F32), mk_p, mv_p,
            jnp.zeros((batch, taps - 1, f2), F32), prm)
        y_s, sh_s, wkv_s, vr_s, cv_s = _layer(y_s, state_shift[l], state_wkv[l], cache_mem_k[l], cache_mem_v[l],
                                              state_ffn_conv[l], prm)
        for lst, val in zip(outs, (wkv_p, sh_p, mk_p, mv_p, cv_p, wkv_s, sh_s, vr_s, cv_s)):
            lst.append(val)
    return (y_p, y_s) + tuple(jnp.stack(lst) for lst in outs)
```

```python
import functools

import jax
import jax.numpy as jnp
from jax import lax
from jax.experimental import pallas as pl
from jax.experimental.pallas import tpu as pltpu

F32 = jnp.float32
BF16 = jnp.bfloat16
HIGHEST = lax.Precision.HIGHEST

LANES = 128
SUBLANES = 8
VMEM_LIMIT_BYTES = 56 * 1024 * 1024

RMS_EPS = 1e-6
LN_EPS = 1e-5
GN_EPS = 64e-5
KK_EPS = 1e-12
RWKV_CHUNK = 64


def _params(*semantics):
    return pltpu.CompilerParams(dimension_semantics=semantics, vmem_limit_bytes=VMEM_LIMIT_BYTES)


def _row_tile(rows, pref):
    t = min(rows, pref)
    while rows % t:
        t -= SUBLANES
    return t


def _col_tile(n, cap):
    best = n
    for t in range(LANES, min(n, cap) + 1, LANES):
        if n % t == 0:
            best = t
    return best


def _bdot(a, b):
    return jnp.dot(a.astype(BF16), b.astype(BF16), preferred_element_type=F32)


def _bdot_nt(a, b):
    return lax.dot_general(a.astype(BF16), b.astype(BF16), (((1,), (1,)), ((), ())),
                           preferred_element_type=F32)


def _fdot(a, b):
    return jnp.dot(a, b, precision=HIGHEST, preferred_element_type=F32)


def _rmsnorm_body(x_ref, g_ref, o_ref):
    x = x_ref[...]
    y = x * lax.rsqrt(jnp.mean(x * x, axis=-1, keepdims=True) + RMS_EPS)
    o_ref[...] = (y * g_ref[...]).astype(o_ref.dtype)


def _rmsnorm(x, g, out_dtype):
    rows, d = x.shape
    tm = _row_tile(rows, 512)
    return pl.pallas_call(
        _rmsnorm_body,
        grid=(rows // tm,),
        in_specs=[pl.BlockSpec((tm, d), lambda i: (i, 0)), pl.BlockSpec((1, d), lambda i: (0, 0))],
        out_specs=pl.BlockSpec((tm, d), lambda i: (i, 0)),
        out_shape=jax.ShapeDtypeStruct((rows, d), out_dtype),
        compiler_params=_params("parallel"),
        name="rmsnorm",
    )(x, g.reshape(1, d))


def _matmul_body(x_ref, w_ref, o_ref, w_bf):
    @pl.when(pl.program_id(1) == 0)
    def _():
        w_bf[...] = w_ref[...].astype(BF16)

    o_ref[...] = jnp.dot(x_ref[...], w_bf[...], preferred_element_type=F32).astype(o_ref.dtype)


def _matmul(x, w, col_start=0, n_cols=None, out_dtype=F32):
    rows, k = x.shape
    n = w.shape[1] - col_start if n_cols is None else n_cols
    assert col_start % LANES == 0 and n % LANES == 0
    tn = _col_tile(n, 1792)
    budget = VMEM_LIMIT_BYTES - (8 << 20)
    tm = _row_tile(rows, 1024)
    while 2 * k * tn * 4 + k * tn * 2 + 2 * tm * k * 2 + 2 * tm * tn * 4 > budget and tm % (2 * SUBLANES) == 0:
        tm //= 2
    return pl.pallas_call(
        _matmul_body,
        grid=(n // tn, rows // tm),
        in_specs=[pl.BlockSpec((tm, k), lambda j, i: (i, 0)),
                  pl.BlockSpec((pl.Element(k), pl.Element(tn)),
                               lambda j, i: (0, pl.multiple_of(col_start + j * tn, LANES)))],
        out_specs=pl.BlockSpec((tm, tn), lambda j, i: (i, j)),
        out_shape=jax.ShapeDtypeStruct((rows, n), out_dtype),
        scratch_shapes=[pltpu.VMEM((k, tn), BF16)],
        compiler_params=_params("parallel", "arbitrary"),
        name="matmul",
    )(x, w)


def _shift_rows(x, halo, k):
    rolled = pltpu.roll(x, k, 0)
    row = lax.broadcasted_iota(jnp.int32, (SUBLANES, x.shape[1]), 0)
    top = rolled[0:SUBLANES]
    for j in range(k):
        top = jnp.where(row == j, halo[SUBLANES - k + j:SUBLANES - k + j + 1, :], top)
    return jnp.concatenate([top, rolled[SUBLANES:]], axis=0)


def _head_sum(x, ones_ref):
    ones = ones_ref[...]
    hi = x.astype(BF16)
    lo = (x - hi.astype(F32)).astype(BF16)
    outs = []
    for j in range(x.shape[1] // LANES):
        sl = slice(j * LANES, (j + 1) * LANES)
        outs.append(jnp.dot(hi[:, sl], ones, preferred_element_type=F32)
                    + jnp.dot(lo[:, sl], ones, preferred_element_type=F32))
    return jnp.concatenate(outs, axis=1)


def _pad_state_rows(state):
    b, n, f = state.shape
    return jnp.pad(state, ((0, 0), (0, SUBLANES - n), (0, 0))).reshape(b * SUBLANES, f)


def _rwkv_prep_body(rw_ref, halo_ref, sp_ref, mu_ref, w0_ref, wd_ref, a0_ref, wa_ref, wg_ref,
                    kk_ref, ka_ref, rk_ref, ones_ref,
                    r_o, lw_o, k_o, v_o, a_o, b_o, g_o, bonus_o, *, seq_rows, tm, width):
    x = rw_ref[...]
    if seq_rows >= tm:
        is_start = (pl.program_id(0) * tm) % seq_rows == 0
        halo = jnp.where(is_start, pltpu.roll(sp_ref[...], SUBLANES - 1, 0), halo_ref[...])
        prev = _shift_rows(x, halo, 1)
    else:
        t = lax.broadcasted_iota(jnp.int32, x.shape, 0) & (seq_rows - 1)
        prev = jnp.where(t == 0, sp_ref[...], pltpu.roll(x, 1, 0))
    xs = x + mu_ref[...] * (prev - x)

    w = width
    r = xs[:, 0:w]
    k = xs[:, w:2 * w]
    v = xs[:, 2 * w:3 * w]
    x_wa = xs[:, 3 * w:3 * w + LANES]
    x_g = xs[:, 3 * w + LANES:3 * w + 2 * LANES]

    z = -(w0_ref[...] + _fdot(jnp.tanh(x_wa), wd_ref[...]))
    softplus = jnp.maximum(z, 0.0) + jnp.log(1.0 + jnp.exp(-jnp.abs(z)))
    lw = -jnp.exp(-softplus - 0.5)
    a_lr = jax.nn.sigmoid(a0_ref[...] + _fdot(x_wa, wa_ref[...]))
    g = _fdot(jax.nn.sigmoid(x_g), wg_ref[...])

    kk = k * kk_ref[...]
    k_mod = k * (1.0 + (a_lr - 1.0) * ka_ref[...])
    kk = kk / jnp.maximum(jnp.sqrt(_head_sum(kk * kk, ones_ref)), KK_EPS)
    bonus = _head_sum(r * k_mod * rk_ref[...], ones_ref) * v

    r_o[...] = r
    lw_o[...] = lw
    k_o[...] = k_mod
    v_o[...] = v
    a_o[...] = -kk
    b_o[...] = kk * a_lr
    g_o[...] = g
    bonus_o[...] = bonus


def _rwkv_prep(rw, shift_rows, seq_rows, prm):
    rows, n_shift = rw.shape
    w = prm["width"]
    tm = _row_tile(rows, 256)
    if seq_rows >= tm:
        assert seq_rows % tm == 0
        sp_spec = pl.BlockSpec((SUBLANES, n_shift), lambda i: ((i * tm) // seq_rows, 0))
    else:
        assert seq_rows == SUBLANES and tm % seq_rows == 0
        sp_spec = pl.BlockSpec((tm, n_shift), lambda i: (i, 0))
    row_spec = lambda c: pl.BlockSpec((1, c), lambda i: (0, 0))
    full = lambda a: pl.BlockSpec(a.shape, lambda i: (0, 0))
    out_spec = pl.BlockSpec((tm, w), lambda i: (i, 0))
    return pl.pallas_call(
        functools.partial(_rwkv_prep_body, seq_rows=seq_rows, tm=tm, width=w),
        grid=(rows // tm,),
        in_specs=[pl.BlockSpec((tm, n_shift), lambda i: (i, 0)),
                  pl.BlockSpec((SUBLANES, n_shift), lambda i: (jnp.maximum(i * (tm // SUBLANES) - 1, 0), 0)),
                  sp_spec, row_spec(n_shift), row_spec(w), full(prm["wd_pad"]), row_spec(w), full(prm["wa_pad"]),
                  full(prm["w_gate"]), row_spec(w), row_spec(w), row_spec(w), full(prm["ones2"])],
        out_specs=[out_spec] * 8,
        out_shape=[jax.ShapeDtypeStruct((rows, w), F32)] * 8,
        compiler_params=_params("parallel"),
        name="rwkv_prep",
    )(rw, rw, shift_rows, prm["mu_shift"], prm["w0"], prm["wd_pad"], prm["a0"], prm["wa_pad"], prm["w_gate"],
      prm["k_k"], prm["k_a"], prm["r_k"], prm["ones2"])


def _wkv_body(r_ref, lw_ref, k_ref, v_ref, a_ref, b_ref, h0_ref, y_ref, hout_ref, h_scr,
              *, chunk, pairs, seqs, width, hd):
    c = pl.program_id(1)
    n_pairs = width // LANES
    pair_blocks = ((lax.broadcasted_iota(jnp.int32, (LANES, LANES), 0) >> (hd.bit_length() - 1))
                   == (lax.broadcasted_iota(jnp.int32, (LANES, LANES), 1) >> (hd.bit_length() - 1)))

    @pl.when(c == 0)
    def _():
        dup = (lax.broadcasted_iota(jnp.int32, (hd, LANES), 0)
               == (lax.broadcasted_iota(jnp.int32, (hd, LANES), 1) & (hd - 1))).astype(F32)
        for i in range(seqs):
            for p in range(n_pairs):
                two = jnp.concatenate([h0_ref[i, 2 * p], h0_ref[i, 2 * p + 1]], axis=0)
                h_scr[i, p] = jnp.where(pair_blocks, _fdot(two, dup), 0.0)

    cs = chunk
    lanes = LANES * pairs
    heads = 2 * pairs
    n = heads * cs
    log_c = cs.bit_length() - 1
    log_hd = hd.bit_length() - 1
    n_stacks = width // lanes

    tri = (lax.broadcasted_iota(jnp.int32, (cs, cs), 0) >= lax.broadcasted_iota(jnp.int32, (cs, cs), 1)).astype(F32)
    row = lax.broadcasted_iota(jnp.int32, (n, lanes), 0)
    lane = lax.broadcasted_iota(jnp.int32, (n, lanes), 1)
    head_mask = (row >> log_c) == (lane >> log_hd)
    ri = lax.broadcasted_iota(jnp.int32, (n, n), 0)
    ci = lax.broadcasted_iota(jnp.int32, (n, n), 1)
    same_head = (ri >> log_c) == (ci >> log_c)
    rt = ri & (cs - 1)
    ct = ci & (cs - 1)
    strict = same_head & (ct < rt)
    incl = same_head & (ct <= rt)
    eye = (ri == ci).astype(F32)
    diag = (lax.broadcasted_iota(jnp.int32, (LANES, LANES), 0)
            == lax.broadcasted_iota(jnp.int32, (LANES, LANES), 1))

    def stack(x):
        return jnp.where(head_mask, jnp.concatenate([x] * heads, axis=0), 0.0)

    def fold(x):
        out = x[0:cs]
        for hidx in range(1, heads):
            out = out + x[hidx * cs:(hidx + 1) * cs]
        return out

    scaled = []
    for i in range(seqs):
        rs = slice(i * cs, (i + 1) * cs)
        lw = lw_ref[rs, :]
        cum = _fdot(tri, lw)
        total = cum[cs - 1:cs]
        e_cum = jnp.exp(cum)
        e_neg = jnp.exp(-cum)
        e_rest = jnp.exp(total - cum)
        b = b_ref[rs, :]
        k = k_ref[rs, :]
        scaled.append(dict(a=a_ref[rs, :] * jnp.exp(cum - lw), r=r_ref[rs, :] * e_cum, b=b * e_neg, k=k * e_neg,
                           v=v_ref[rs, :], bh=b * e_rest, kh=k * e_rest, total=total))

    inst = [(i, s) for i in range(seqs) for s in range(n_stacks)]
    jj = range(len(inst))

    def stacked(name):
        return [stack(scaled[i][name][:, s * lanes:(s + 1) * lanes]) for i, s in inst]

    a_st, r_st, b_st, k_st, v_st, bh_st, kh_st = (stacked(nm) for nm in ("a", "r", "b", "k", "v", "bh", "kh"))

    gram = [_bdot_nt(jnp.concatenate([a_st[j], r_st[j]], axis=0), jnp.concatenate([b_st[j], k_st[j]], axis=0))
            for j in jj]
    l_ab = [jnp.where(strict, gram[j][:n, :n], 0.0) for j in jj]
    m_rb = [jnp.where(incl, gram[j][n:, :n], 0.0) for j in jj]
    l_ak_m_rk = [jnp.concatenate([jnp.where(strict, gram[j][:n, n:], 0.0), jnp.where(incl, gram[j][n:, n:], 0.0)],
                                 axis=0) for j in jj]

    t_inv = [eye + l_ab[j] for j in jj]
    x_pow = [_bdot(l_ab[j], l_ab[j]) for j in jj]
    for _ in range(log_c - 2):
        both = [_bdot(jnp.concatenate([x_pow[j], t_inv[j]], axis=0), x_pow[j]) for j in jj]
        t_inv = [t_inv[j] + both[j][n:] for j in jj]
        x_pow = [both[j][:n] for j in jj]
    t_inv = [t_inv[j] + _bdot(t_inv[j], x_pow[j]) for j in jj]

    lv = [_bdot(l_ak_m_rk[j], v_st[j]) for j in jj]
    aw = [_bdot(t_inv[j], jnp.concatenate([a_st[j], lv[j][:n]], axis=1)) for j in jj]
    rb = [_bdot(m_rb[j], aw[j]) for j in jj]
    r_bar = [fold(r_st[j] + rb[j][:, :lanes]) for j in jj]
    y0 = [fold(rb[j][:, lanes:] + lv[j][n:]) for j in jj]

    jp = [(j, p) for j in jj for p in range(pairs)]

    def pair_lanes(x, p, off=0):
        return x[:, off + p * LANES:off + (p + 1) * LANES]

    h = [h_scr[inst[j][0], inst[j][1] * pairs + p] for j, p in jp]
    yv = [_bdot_nt(pair_lanes(r_bar[j], p), h[q]) + pair_lanes(y0[j], p) for q, (j, p) in enumerate(jp)]
    for q, (j, p) in enumerate(jp):
        i, s = inst[j]
        y_ref[i * cs:(i + 1) * cs, s * lanes + p * LANES:s * lanes + (p + 1) * LANES] = yv[q]
    aw_t = [jnp.concatenate([pair_lanes(aw[j], p).T, pair_lanes(aw[j], p, lanes).T], axis=0) for j, p in jp]
    v_t = [pair_lanes(v_st[j], p).T for j, p in jp]
    pq = [_bdot(aw_t[q], pair_lanes(bh_st[j], p)) for q, (j, p) in enumerate(jp)]
    kv = [_bdot(v_t[q], pair_lanes(kh_st[j], p)) for q, (j, p) in enumerate(jp)]
    for q, (j, p) in enumerate(jp):
        i, s = inst[j]
        decay = pair_lanes(scaled[i]["total"], p, s * lanes)
        phi_t = jnp.where(diag, jnp.exp(decay), 0.0) + pq[q][:LANES]
        h_scr[i, s * pairs + p] = _bdot(h[q], phi_t) + (pq[q][LANES:] + kv[q])

    @pl.when(c == pl.num_programs(1) - 1)
    def _():
        merge = ((lax.broadcasted_iota(jnp.int32, (LANES, hd), 0) & (hd - 1))
                 == lax.broadcasted_iota(jnp.int32, (LANES, hd), 1)).astype(F32)
        for i in range(seqs):
            for p in range(n_pairs):
                two = _fdot(h_scr[i, p], merge)
                hout_ref[i, 2 * p] = two[:hd]
                hout_ref[i, 2 * p + 1] = two[hd:]


def _wkv(r, lw, k, v, a, b, h0, batch, seq_len, hd):
    rows, w = r.shape
    chunk = min(seq_len, RWKV_CHUNK)
    assert seq_len % chunk == 0 and chunk % SUBLANES == 0 and LANES % chunk == 0
    pairs = LANES // (2 * chunk)
    n_chunks = seq_len // chunk
    n_pairs = w // LANES
    seqs = 2 if (n_chunks == 1 and batch % 2 == 0) else 1
    vec = pl.BlockSpec((seqs * chunk, w), lambda bi, ci: (bi * n_chunks + ci, 0))
    st = pl.BlockSpec((seqs, w // hd, hd, hd), lambda bi, ci: (bi, 0, 0, 0))
    return pl.pallas_call(
        functools.partial(_wkv_body, chunk=chunk, pairs=pairs, seqs=seqs, width=w, hd=hd),
        grid=(batch // seqs, n_chunks),
        in_specs=[vec] * 6 + [st],
        out_specs=[vec, st],
        out_shape=[jax.ShapeDtypeStruct((rows, w), F32),
                   jax.ShapeDtypeStruct((batch, w // hd, hd, hd), F32)],
        scratch_shapes=[pltpu.VMEM((seqs, n_pairs, LANES, LANES), F32)],
        compiler_params=_params("parallel", "arbitrary"),
        name="rwkv_chunks",
    )(r, lw, k, v, a, b, h0)


def _rwkv_post_body(y_ref, bonus_ref, g_ref, lw_ref, lb_ref, ones_ref, o_ref, *, hd):
    y = y_ref[...]
    mu = _head_sum(y, ones_ref) * (1.0 / hd)
    d = y - mu
    var = _head_sum(d * d, ones_ref) * (1.0 / hd)
    yn = d * lax.rsqrt(var + GN_EPS) * lw_ref[...] + lb_ref[...]
    o_ref[...] = ((yn + bonus_ref[...]) * g_ref[...]).astype(o_ref.dtype)


def _rwkv_post(y, bonus, g, prm, hd):
    rows, w = y.shape
    tm = _row_tile(rows, 512)
    blk = pl.BlockSpec((tm, w), lambda i: (i, 0))
    row_spec = pl.BlockSpec((1, w), lambda i: (0, 0))
    return pl.pallas_call(
        functools.partial(_rwkv_post_body, hd=hd),
        grid=(rows // tm,),
        in_specs=[blk, blk, blk, row_spec, row_spec, pl.BlockSpec((LANES, LANES), lambda i: (0, 0))],
        out_specs=blk,
        out_shape=jax.ShapeDtypeStruct((rows, w), BF16),
        compiler_params=_params("parallel"),
        name="rwkv_post",
    )(y, bonus, g, prm["lnx_w"], prm["lnx_b"], prm["ones2"])


def _sgu_body(gu_ref, gv_ref, sg_ref, sb_ref, wm_ref, bias_ref, o_ref, v_ref, *, groups):
    u = jax.nn.gelu(gu_ref[...])
    vf = jax.nn.gelu(gv_ref[...])
    mu = jnp.mean(vf, axis=-1, keepdims=True)
    d = vf - mu
    var = jnp.mean(d * d, axis=-1, keepdims=True)
    v = (d * lax.rsqrt(var + LN_EPS)) * sg_ref[...] + sb_ref[...]
    v_ref[...] = v
    rows = v.shape[0]
    gd = v.shape[1] // groups
    causal = (lax.broadcasted_iota(jnp.int32, (rows, rows), 0)
              >= lax.broadcasted_iota(jnp.int32, (rows, rows), 1))
    for g in range(groups):
        sl = slice(g * gd, (g + 1) * gd)
        s = _bdot(jnp.where(causal, wm_ref[g], 0.0), v[:, sl]) + bias_ref[g]
        o_ref[:, sl] = (u[:, sl] * s).astype(o_ref.dtype)


def _sgu(guvq, w_mix, bias, prm):
    rows = guvq.shape[0]
    w = prm["width"]
    groups, cs, _ = w_mix.shape
    row_spec = pl.BlockSpec((1, w), lambda i: (0, 0))
    out = pl.BlockSpec((cs, w), lambda i: (i, 0))
    return pl.pallas_call(
        functools.partial(_sgu_body, groups=groups),
        grid=(rows // cs,),
        in_specs=[pl.BlockSpec((cs, w), lambda i: (i, 0)), pl.BlockSpec((cs, w), lambda i: (i, 1)),
                  row_spec, row_spec,
                  pl.BlockSpec(w_mix.shape, lambda i: (0, 0, 0)), pl.BlockSpec(bias.shape, lambda i: (0, 0, 0))],
        out_specs=[out, out],
        out_shape=[jax.ShapeDtypeStruct((rows, w), BF16), jax.ShapeDtypeStruct((rows, w), F32)],
        compiler_params=_params("parallel"),
        name="sgu",
    )(guvq, guvq, prm["sgu_g"], prm["sgu_b"], w_mix, bias)


def _xattn_body(q_ref, k_ref, v_ref, o_ref, *, heads, seqs, tq, m):
    hd = q_ref.shape[1] // heads
    scale = hd ** -0.5
    for h in range(heads):
        sl = slice(h * hd, (h + 1) * hd)
        outs = []
        for i in range(seqs):
            qs = slice(i * tq, (i + 1) * tq)
            ms = slice(i * m, (i + 1) * m)
            s = _bdot_nt(q_ref[qs, sl], k_ref[ms, sl]) * scale
            e = jnp.exp(s - jnp.max(s, axis=-1, keepdims=True))
            p = e / jnp.sum(e, axis=-1, keepdims=True)
            outs.append(_bdot(p, v_ref[ms, sl]))
        o_ref[:, sl] = jnp.concatenate(outs, axis=0).astype(o_ref.dtype)


def _xattn(guvq, mem_k, mem_v, seq_len, w):
    batch, m, heads, hd = mem_k.shape
    rows = guvq.shape[0]
    tq = _row_tile(seq_len, 512)
    nq = seq_len // tq
    seqs = max(1, 32 // tq)
    assert batch % seqs == 0 and (seqs == 1 or nq == 1)
    kv = pl.BlockSpec((seqs * m, w), lambda bi, qi: (bi, 0))
    qo = lambda col: pl.BlockSpec((seqs * tq, w), lambda bi, qi: (bi * nq + qi, col))
    return pl.pallas_call(
        functools.partial(_xattn_body, heads=heads, seqs=seqs, tq=tq, m=m),
        grid=(batch // seqs, nq),
        in_specs=[qo(2), kv, kv],
        out_specs=qo(0),
        out_shape=jax.ShapeDtypeStruct((rows, w), BF16),
        compiler_params=_params("parallel", "parallel"),
        name="mem_xattn",
    )(guvq, mem_k.reshape(batch * m, w), mem_v.reshape(batch * m, w))


def _branch_body(a_ref, b_ref, c_ref, wb_ref, g0_ref, g1_ref, g2_ref, o_ref, wb_bf):
    @pl.when(pl.program_id(1) == 0)
    def _():
        wb_bf[...] = wb_ref[...].astype(BF16)

    acc = jax.nn.sigmoid(g0_ref[...]) * jnp.dot(a_ref[...], wb_bf[0], preferred_element_type=F32)
    acc = acc + jax.nn.sigmoid(g1_ref[...]) * jnp.dot(b_ref[...], wb_bf[1], preferred_element_type=F32)
    acc = acc + jax.nn.sigmoid(g2_ref[...]) * jnp.dot(c_ref[...], wb_bf[2], preferred_element_type=F32)
    o_ref[...] = acc.astype(o_ref.dtype)


def _branch_mix(a_out, b_out, c_out, w_branch, gt):
    rows, w = a_out.shape
    nb, _, d = w_branch.shape
    assert nb == 3
    tm = _row_tile(rows, 1024)
    tn = _col_tile(d, 512)
    nj = d // tn
    br = pl.BlockSpec((tm, w), lambda j, i: (i, 0))
    gate = lambda n: pl.BlockSpec((tm, tn), lambda j, i: (i, j + n * nj))
    return pl.pallas_call(
        _branch_body,
        grid=(nj, rows // tm),
        in_specs=[br, br, br, pl.BlockSpec((nb, w, tn), lambda j, i: (0, 0, j)), gate(0), gate(1), gate(2)],
        out_specs=pl.BlockSpec((tm, tn), lambda j, i: (i, j)),
        out_shape=jax.ShapeDtypeStruct((rows, d), BF16),
        scratch_shapes=[pltpu.VMEM((nb, w, tn), BF16)],
        compiler_params=_params("parallel", "arbitrary"),
        name="branch_mix",
    )(a_out, b_out, c_out, w_branch, gt, gt, gt)


def _mm_res_norm_body(x_ref, w_ref, res_ref, g_ref, g2_ref, y_ref, *rest, nk, with_next):
    if with_next:
        yn_ref, acc_ref = rest
    else:
        (acc_ref,) = rest
    kk = pl.program_id(1)
    part = jnp.dot(x_ref[...], w_ref[...], preferred_element_type=F32)

    @pl.when(kk == 0)
    def _():
        acc_ref[...] = part

    @pl.when(kk > 0)
    def _():
        acc_ref[...] += part

    @pl.when(kk == nk - 1)
    def _():
        f = acc_ref[...]
        y = res_ref[...] + (f * lax.rsqrt(jnp.mean(f * f, axis=-1, keepdims=True) + RMS_EPS)) * g_ref[...]
        y_ref[...] = y
        if with_next:
            yn = y * lax.rsqrt(jnp.mean(y * y, axis=-1, keepdims=True) + RMS_EPS)
            yn_ref[...] = (yn * g2_ref[...]).astype(yn_ref.dtype)


def _mm_res_norm(x, w, res, g, g_next=None):
    rows, k = x.shape
    d = w.shape[1]
    with_next = g_next is not None
    tm = _row_tile(rows, 512)
    tk = _col_tile(k, 1536)
    nk = k // tk
    blk = pl.BlockSpec((tm, d), lambda i, kk: (i, 0))
    row_spec = pl.BlockSpec((1, d), lambda i, kk: (0, 0))
    out_specs = [blk, blk] if with_next else [blk]
    out_shape = [jax.ShapeDtypeStruct((rows, d), F32)]
    if with_next:
        out_shape.append(jax.ShapeDtypeStruct((rows, d), BF16))
    g2 = g_next if with_next else g
    outs = pl.pallas_call(
        functools.partial(_mm_res_norm_body, nk=nk, with_next=with_next),
        grid=(rows // tm, nk),
        in_specs=[pl.BlockSpec((tm, tk), lambda i, kk: (i, kk)), pl.BlockSpec((tk, d), lambda i, kk: (kk, 0)),
                  blk, row_spec, row_spec],
        out_specs=out_specs,
        out_shape=out_shape,
        scratch_shapes=[pltpu.VMEM((tm, d), F32)],
        compiler_params=_params("parallel", "arbitrary"),
        name="matmul_res_norm",
    )(x, w, res, g.reshape(1, d), g2.reshape(1, d))
    return outs if with_next else outs[0]


def _up_conv_gate_body(x_ref, wg_ref, wv_ref, eg_ref, ev_ref, cwg_ref, cwv_ref, cbg_ref, cbv_ref,
                       o_ref, tg_ref, tv_ref, wg_bf, wv_bf, carry_g, carry_v, *, seq_rows, tm, taps):
    i = pl.program_id(1)

    @pl.when(i == 0)
    def _():
        wg_bf[...] = wg_ref[...].astype(BF16)
        wv_bf[...] = wv_ref[...].astype(BF16)
        carry_g[...] = jnp.zeros_like(carry_g)
        carry_v[...] = jnp.zeros_like(carry_v)

    x = x_ref[...]

    def conv(w_bf, e_ref, cw_ref, cb_ref, carry, tail_ref):
        u = jnp.dot(x, w_bf[...], preferred_element_type=F32)
        cw = cw_ref[...]
        acc = cb_ref[...] + cw[taps - 1:taps] * u
        if seq_rows >= tm:
            is_start = (i * tm) % seq_rows == 0
            halo = jnp.where(is_start, pltpu.roll(e_ref[...], SUBLANES - (taps - 1), 0), carry[...])
            for back in range(1, taps):
                acc = acc + cw[taps - 1 - back:taps - back] * _shift_rows(u, halo, back)
            carry[...] = u[tm - SUBLANES:tm]
            tail_ref[...] = u[tm - SUBLANES:tm]
        else:
            t = lax.broadcasted_iota(jnp.int32, u.shape, 0) & (seq_rows - 1)
            e = e_ref[...]
            for back in range(1, taps):
                up_by = taps - 1 - back
                state = pltpu.roll(e, tm - up_by, 0) if up_by else e
                prev = jnp.where(t < back, state, pltpu.roll(u, back, 0))
                acc = acc + cw[taps - 1 - back:taps - back] * prev
            tail_ref[...] = u
        return acc

    gate = conv(wg_bf, eg_ref, cwg_ref, cbg_ref, carry_g, tg_ref)
    val = conv(wv_bf, ev_ref, cwv_ref, cbv_ref, carry_v, tv_ref)
    o_ref[...] = (jax.nn.gelu(gate) * val).astype(o_ref.dtype)


def _up_conv_gate(x, w_up, state_rows, conv_w, conv_b, seq_rows):
    rows, d = x.shape
    f2 = w_up.shape[1]
    dff = f2 // 2
    taps = conv_w.shape[0]
    tm = _row_tile(rows if seq_rows == SUBLANES else seq_rows, 1024)
    tn = _col_tile(dff, 512)
    nj = dff // tn
    if seq_rows >= tm:
        assert seq_rows % tm == 0
        per_seq = lambda off: pl.BlockSpec((SUBLANES, tn), lambda j, i: ((i * tm) // seq_rows, j + off))
        st, tail = per_seq, per_seq(0)
        tail_rows = (rows // seq_rows) * SUBLANES
    else:
        assert seq_rows == SUBLANES and tm % seq_rows == 0
        st = lambda off: pl.BlockSpec((tm, tn), lambda j, i: (i, j + off))
        tail = st(0)
        tail_rows = rows
    wt = lambda off: pl.BlockSpec((d, tn), lambda j, i: (0, j + off))
    cw = lambda off: pl.BlockSpec((taps, tn), lambda j, i: (0, j + off))
    cb = lambda off: pl.BlockSpec((1, tn), lambda j, i: (0, j + off))
    act, tail_g, tail_v = pl.pallas_call(
        functools.partial(_up_conv_gate_body, seq_rows=seq_rows, tm=tm, taps=taps),
        grid=(nj, rows // tm),
        in_specs=[pl.BlockSpec((tm, d), lambda j, i: (i, 0)), wt(0), wt(nj), st(0), st(nj),
                  cw(0), cw(nj), cb(0), cb(nj)],
        out_specs=[pl.BlockSpec((tm, tn), lambda j, i: (i, j)), tail, tail],
        out_shape=[jax.ShapeDtypeStruct((rows, dff), BF16), jax.ShapeDtypeStruct((tail_rows, dff), F32),
                   jax.ShapeDtypeStruct((tail_rows, dff), F32)],
        scratch_shapes=[pltpu.VMEM((d, tn), BF16), pltpu.VMEM((d, tn), BF16),
                        pltpu.VMEM((SUBLANES, tn), F32), pltpu.VMEM((SUBLANES, tn), F32)],
        compiler_params=_params("parallel", "arbitrary"),
        name="up_conv_gate",
    )(x, w_up, w_up, state_rows, state_rows, conv_w, conv_w, conv_b.reshape(1, f2), conv_b.reshape(1, f2))
    return act, jnp.concatenate([tail_g, tail_v], axis=1)


def _layer(x, shift_prev, wkv0, mem_k, mem_v, conv_prev, prm):
    batch, seq_len, d = x.shape
    rows = batch * seq_len
    w = prm["width"]
    hd = prm["head_dim"]
    x2 = x.reshape(rows, d)

    xn = _rmsnorm(x2, prm["g_pre_mix"], BF16)
    n_shift = prm["mu_shift"].shape[1]
    rw = _matmul(xn, prm["w_in"], 0, n_shift)
    guvq = _matmul(xn, prm["w_in"], n_shift, 3 * w)
    gt = _matmul(xn, prm["w_in"], n_shift + 3 * w)
    new_shift = rw.reshape(batch, seq_len, -1)[:, -1]

    r, lw, k, v, a, b, g, bonus = _rwkv_prep(rw, _pad_state_rows(shift_prev[:, None, :]), seq_len, prm)
    y, new_wkv = _wkv(r, lw, k, v, a, b, wkv0, batch, seq_len, hd)
    a_out = _rwkv_post(y, bonus, g, prm, hd)

    if seq_len % prm["sgu_chunk"] == 0:
        w_mix, bias = prm["w_s"], prm["sgu_bias_full"]
    else:
        assert seq_len == SUBLANES
        w_mix, bias = prm["w_s_short"], prm["sgu_bias_short"]
    b_out, v_rows = _sgu(guvq, w_mix, bias, prm)
    c_out = _xattn(guvq, mem_k, mem_v, seq_len, w)

    mix = _branch_mix(a_out, b_out, c_out, prm["w_branch"], gt)
    h, hn = _mm_res_norm(mix, prm["w_out"], x2, prm["g_post_mix"], prm["g_pre_ffn"])

    taps = prm["conv_w"].shape[0]
    act, tail = _up_conv_gate(hn, prm["w_up"], _pad_state_rows(conv_prev), prm["conv_w"], prm["conv_b"], seq_len)
    conv_new = tail.reshape(batch, SUBLANES, -1)[:, SUBLANES - (taps - 1):]
    y_out = _mm_res_norm(act, prm["w_down"], h, prm["g_post_ffn"])
    return (y_out.reshape(batch, seq_len, d), new_shift, new_wkv,
            v_rows.reshape(batch, seq_len, w), conv_new)


def _prepare(l, g_pre_mix, w_in, mu_shift, w0, w_decay, a0, w_aaa, w_gate, k_k, k_a, r_k, lnx_w, lnx_b,
             sgu_g, sgu_b, w_s, b_s, w_branch, w_out, g_post_mix, g_pre_ffn, w_up, conv_w, conv_b, w_down,
             g_post_ffn):
    heads, hd = r_k.shape[1], r_k.shape[2]
    w = heads * hd
    d = w_in.shape[1]
    n_shift = mu_shift.shape[1]
    rank_d, rank_a, rank_g = w_decay.shape[1], w_aaa.shape[1], w_gate.shape[1]
    assert rank_d + rank_a == LANES and rank_g == LANES and n_shift == 3 * w + 2 * LANES
    groups, sgu_chunk, _ = w_s.shape[1:]
    row = lambda t: t[l].reshape(1, -1)
    lane_head = jnp.arange(LANES) // hd
    seqs = 2 * LANES // SUBLANES
    seq_of = jnp.arange(seqs * SUBLANES) // SUBLANES
    w_short = jnp.where(seq_of[:, None] == seq_of[None, :],
                        jnp.tile(w_s[l][:, :SUBLANES, :SUBLANES], (1, seqs, seqs)), 0.0)
    gd = w // groups
    return dict(
        width=w, head_dim=hd, sgu_chunk=sgu_chunk,
        g_pre_mix=g_pre_mix[l],
        w_in=w_in[l],
        mu_shift=row(mu_shift), w0=row(w0), a0=row(a0),
        wd_pad=jnp.pad(w_decay[l], ((0, rank_a), (0, 0))),
        wa_pad=jnp.pad(w_aaa[l], ((rank_d, 0), (0, 0))),
        w_gate=w_gate[l],
        k_k=row(k_k), k_a=row(k_a), r_k=row(r_k), lnx_w=row(lnx_w), lnx_b=row(lnx_b),
        ones2=(lane_head[:, None] == lane_head[None, :]).astype(BF16),
        sgu_g=row(sgu_g), sgu_b=row(sgu_b),
        w_s=w_s[l],
        sgu_bias_full=jnp.broadcast_to(b_s[l][:, :, None], (groups, sgu_chunk, gd)),
        w_s_short=w_short,
        sgu_bias_short=jnp.broadcast_to(jnp.tile(b_s[l][:, :SUBLANES], (1, seqs))[:, :, None],
                                        (groups, seqs * SUBLANES, gd)),
        w_branch=w_branch[l], w_out=w_out[l].astype(BF16),
        g_post_mix=g_post_mix[l], g_pre_ffn=g_pre_ffn[l],
        w_up=w_up[l], conv_w=conv_w[l], conv_b=conv_b[l], w_down=w_down[l].astype(BF16),
        g_post_ffn=g_post_ffn[l],
    )


def kernel(x_prompt, x_sample, mem_prompt, state_wkv, state_shift, cache_mem_k, cache_mem_v, state_ffn_conv, g_pre_mix, w_in, mu_shift, w0, w_decay, a0, w_aaa, w_gate, k_k, k_a, r_k, lnx_w, lnx_b, sgu_g, sgu_b, w_s, b_s, g_mem, w_mem_k, w_mem_v, w_branch, w_out, g_post_mix, g_pre_ffn, w_up, conv_w, conv_b, w_down, g_post_ffn):
    depth = w_in.shape[0]
    batch = x_prompt.shape[0]
    mem_len, d = mem_prompt.shape[1], mem_prompt.shape[2]
    xh, xhd = cache_mem_k.shape[3], cache_mem_k.shape[4]
    heads, hd = r_k.shape[1], r_k.shape[2]
    n_shift = mu_shift.shape[1]
    f2 = w_up.shape[2]
    taps = conv_w.shape[1]
    y_p, y_s = x_prompt, x_sample
    outs = [[] for _ in range(9)]
    for l in range(depth):
        prm = _prepare(l, g_pre_mix, w_in, mu_shift, w0, w_decay, a0, w_aaa, w_gate, k_k, k_a, r_k, lnx_w, lnx_b,
                       sgu_g, sgu_b, w_s, b_s, w_branch, w_out, g_post_mix, g_pre_ffn, w_up, conv_w, conv_b,
                       w_down, g_post_ffn)
        mn = _rmsnorm(mem_prompt.reshape(batch * mem_len, d), g_mem[l], BF16)
        mk_p = _matmul(mn, w_mem_k[l]).reshape(batch, mem_len, xh, xhd)
        mv_p = _matmul(mn, w_mem_v[l]).reshape(batch, mem_len, xh, xhd)
        y_p, sh_p, wkv_p, _, cv_p = _layer(
            y_p, jnp.zeros((batch, n_shift), F32), jnp.zeros((batch, heads, hd, hd), F32), mk_p, mv_p,
            jnp.zeros((batch, taps - 1, f2), F32), prm)
        y_s, sh_s, wkv_s, vr_s, cv_s = _layer(y_s, state_shift[l], state_wkv[l], cache_mem_k[l], cache_mem_v[l],
                                              state_ffn_conv[l], prm)
        for lst, val in zip(outs, (wkv_p, sh_p, mk_p, mv_p, cv_p, wkv_s, sh_s, vr_s, cv_s)):
            lst.append(val)
    return (y_p, y_s) + tuple(jnp.stack(lst) for lst in outs)
```

```python
import functools

import jax
import jax.numpy as jnp
from jax import lax
from jax.experimental import pallas as pl
from jax.experimental.pallas import tpu as pltpu

F32 = jnp.float32
BF16 = jnp.bfloat16
HIGHEST = lax.Precision.HIGHEST

LANES = 128
SUBLANES = 8
VMEM_LIMIT_BYTES = 56 * 1024 * 1024

RMS_EPS = 1e-6
LN_EPS = 1e-5
GN_EPS = 64e-5
KK_EPS = 1e-12
RWKV_CHUNK = 64


def _params(*semantics):
    return pltpu.CompilerParams(dimension_semantics=semantics, vmem_limit_bytes=VMEM_LIMIT_BYTES)


def _row_tile(rows, pref):
    t = min(rows, pref)
    while rows % t:
        t -= SUBLANES
    return t


def _col_tile(n, cap):
    best = n
    for t in range(LANES, min(n, cap) + 1, LANES):
        if n % t == 0:
            best = t
    return best


def _bdot(a, b):
    return jnp.dot(a.astype(BF16), b.astype(BF16), preferred_element_type=F32)


def _bdot_nt(a, b):
    return lax.dot_general(a.astype(BF16), b.astype(BF16), (((1,), (1,)), ((), ())),
                           preferred_element_type=F32)


def _fdot(a, b):
    return jnp.dot(a, b, precision=HIGHEST, preferred_element_type=F32)


def _dot3(a, b):
    a_hi = a.astype(BF16)
    b_hi = b.astype(BF16)
    a_lo = (a - a_hi.astype(F32)).astype(BF16)
    b_lo = (b - b_hi.astype(F32)).astype(BF16)
    dot = lambda x, y: jnp.dot(x, y, preferred_element_type=F32)
    return dot(a_hi, b_hi) + (dot(a_hi, b_lo) + dot(a_lo, b_hi))


def _rmsnorm_body(x_ref, g_ref, o_ref):
    x = x_ref[...]
    y = x * lax.rsqrt(jnp.mean(x * x, axis=-1, keepdims=True) + RMS_EPS)
    o_ref[...] = (y * g_ref[...]).astype(o_ref.dtype)


def _rmsnorm(x, g, out_dtype):
    rows, d = x.shape
    tm = _row_tile(rows, 512)
    return pl.pallas_call(
        _rmsnorm_body,
        grid=(rows // tm,),
        in_specs=[pl.BlockSpec((tm, d), lambda i: (i, 0)), pl.BlockSpec((1, d), lambda i: (0, 0))],
        out_specs=pl.BlockSpec((tm, d), lambda i: (i, 0)),
        out_shape=jax.ShapeDtypeStruct((rows, d), out_dtype),
        compiler_params=_params("parallel"),
        name="rmsnorm",
    )(x, g.reshape(1, d))


def _matmul_body(x_ref, w_ref, o_ref, w_bf):
    @pl.when(pl.program_id(1) == 0)
    def _():
        w_bf[...] = w_ref[...].astype(BF16)

    o_ref[...] = jnp.dot(x_ref[...], w_bf[...], preferred_element_type=F32).astype(o_ref.dtype)


def _matmul(x, w, col_start=0, n_cols=None, out_dtype=F32):
    rows, k = x.shape
    n = w.shape[1] - col_start if n_cols is None else n_cols
    assert col_start % LANES == 0 and n % LANES == 0
    tn = _col_tile(n, 1792)
    budget = VMEM_LIMIT_BYTES - (8 << 20)
    tm = _row_tile(rows, 1024)
    while 2 * k * tn * 4 + k * tn * 2 + 2 * tm * k * 2 + 2 * tm * tn * 4 > budget and tm % (2 * SUBLANES) == 0:
        tm //= 2
    return pl.pallas_call(
        _matmul_body,
        grid=(n // tn, rows // tm),
        in_specs=[pl.BlockSpec((tm, k), lambda j, i: (i, 0)),
                  pl.BlockSpec((pl.Element(k), pl.Element(tn)),
                               lambda j, i: (0, pl.multiple_of(col_start + j * tn, LANES)))],
        out_specs=pl.BlockSpec((tm, tn), lambda j, i: (i, j)),
        out_shape=jax.ShapeDtypeStruct((rows, n), out_dtype),
        scratch_shapes=[pltpu.VMEM((k, tn), BF16)],
        compiler_params=_params("parallel", "arbitrary"),
        name="matmul",
    )(x, w)


def _shift_rows(x, halo, k):
    rolled = pltpu.roll(x, k, 0)
    row = lax.broadcasted_iota(jnp.int32, (SUBLANES, x.shape[1]), 0)
    top = rolled[0:SUBLANES]
    for j in range(k):
        top = jnp.where(row == j, halo[SUBLANES - k + j:SUBLANES - k + j + 1, :], top)
    return jnp.concatenate([top, rolled[SUBLANES:]], axis=0)


def _head_sum(x, ones_ref):
    ones = ones_ref[...]
    hi = x.astype(BF16)
    lo = (x - hi.astype(F32)).astype(BF16)
    outs = []
    for j in range(x.shape[1] // LANES):
        sl = slice(j * LANES, (j + 1) * LANES)
        outs.append(jnp.dot(hi[:, sl], ones, preferred_element_type=F32)
                    + jnp.dot(lo[:, sl], ones, preferred_element_type=F32))
    return jnp.concatenate(outs, axis=1)


def _pad_state_rows(state):
    b, n, f = state.shape
    return jnp.pad(state, ((0, 0), (0, SUBLANES - n), (0, 0))).reshape(b * SUBLANES, f)


def _rwkv_prep_body(rw_ref, halo_ref, sp_ref, mu_ref, w0_ref, wd_ref, a0_ref, wa_ref, wg_ref,
                    kk_ref, ka_ref, rk_ref, ones_ref,
                    r_o, lw_o, k_o, v_o, a_o, b_o, g_o, bonus_o, *, seq_rows, tm, width):
    x = rw_ref[...]
    if seq_rows >= tm:
        is_start = (pl.program_id(0) * tm) % seq_rows == 0
        halo = jnp.where(is_start, pltpu.roll(sp_ref[...], SUBLANES - 1, 0), halo_ref[...])
        prev = _shift_rows(x, halo, 1)
    else:
        t = lax.broadcasted_iota(jnp.int32, x.shape, 0) & (seq_rows - 1)
        prev = jnp.where(t == 0, sp_ref[...], pltpu.roll(x, 1, 0))
    xs = x + mu_ref[...] * (prev - x)

    w = width
    r = xs[:, 0:w]
    k = xs[:, w:2 * w]
    v = xs[:, 2 * w:3 * w]
    x_wa = xs[:, 3 * w:3 * w + LANES]
    x_g = xs[:, 3 * w + LANES:3 * w + 2 * LANES]

    z = -(w0_ref[...] + _dot3(jnp.tanh(x_wa), wd_ref[...]))
    softplus = jnp.maximum(z, 0.0) + jnp.log(1.0 + jnp.exp(-jnp.abs(z)))
    lw = -jnp.exp(-softplus - 0.5)
    a_lr = jax.nn.sigmoid(a0_ref[...] + _bdot(x_wa, wa_ref[...]))
    g = _bdot(jax.nn.sigmoid(x_g), wg_ref[...])

    kk = k * kk_ref[...]
    k_mod = k * (1.0 + (a_lr - 1.0) * ka_ref[...])
    kk = kk / jnp.maximum(jnp.sqrt(_head_sum(kk * kk, ones_ref)), KK_EPS)
    bonus = _head_sum(r * k_mod * rk_ref[...], ones_ref) * v

    r_o[...] = r
    lw_o[...] = lw
    k_o[...] = k_mod
    v_o[...] = v
    a_o[...] = -kk
    b_o[...] = kk * a_lr
    g_o[...] = g
    bonus_o[...] = bonus


def _rwkv_prep(rw, shift_rows, seq_rows, prm):
    rows, n_shift = rw.shape
    w = prm["width"]
    tm = _row_tile(rows, 256)
    if seq_rows >= tm:
        assert seq_rows % tm == 0
        sp_spec = pl.BlockSpec((SUBLANES, n_shift), lambda i: ((i * tm) // seq_rows, 0))
    else:
        assert seq_rows == SUBLANES and tm % seq_rows == 0
        sp_spec = pl.BlockSpec((tm, n_shift), lambda i: (i, 0))
    row_spec = lambda c: pl.BlockSpec((1, c), lambda i: (0, 0))
    full = lambda a: pl.BlockSpec(a.shape, lambda i: (0, 0))
    out_spec = pl.BlockSpec((tm, w), lambda i: (i, 0))
    return pl.pallas_call(
        functools.partial(_rwkv_prep_body, seq_rows=seq_rows, tm=tm, width=w),
        grid=(rows // tm,),
        in_specs=[pl.BlockSpec((tm, n_shift), lambda i: (i, 0)),
                  pl.BlockSpec((SUBLANES, n_shift), lambda i: (jnp.maximum(i * (tm // SUBLANES) - 1, 0), 0)),
                  sp_spec, row_spec(n_shift), row_spec(w), full(prm["wd_pad"]), row_spec(w), full(prm["wa_pad"]),
                  full(prm["w_gate"]), row_spec(w), row_spec(w), row_spec(w), full(prm["ones2"])],
        out_specs=[out_spec] * 8,
        out_shape=[jax.ShapeDtypeStruct((rows, w), F32)] * 8,
        compiler_params=_params("parallel"),
        name="rwkv_prep",
    )(rw, rw, shift_rows, prm["mu_shift"], prm["w0"], prm["wd_pad"], prm["a0"], prm["wa_pad"], prm["w_gate"],
      prm["k_k"], prm["k_a"], prm["r_k"], prm["ones2"])


def _wkv_body(r_ref, lw_ref, k_ref, v_ref, a_ref, b_ref, h0_ref, y_ref, hout_ref, h_scr,
              *, chunk, pairs, seqs, width, hd):
    c = pl.program_id(1)
    n_pairs = width // LANES

    @pl.when(c == 0)
    def _():
        zero = jnp.zeros((hd, hd), F32)
        for i in range(seqs):
            for p in range(n_pairs):
                h_scr[i, p] = jnp.concatenate(
                    [jnp.concatenate([h0_ref[i, 2 * p], zero], axis=1),
                     jnp.concatenate([zero, h0_ref[i, 2 * p + 1]], axis=1)], axis=0)

    cs = chunk
    lanes = LANES * pairs
    heads = 2 * pairs
    n = heads * cs
    log_c = cs.bit_length() - 1
    log_hd = hd.bit_length() - 1
    n_stacks = width // lanes

    tri = (lax.broadcasted_iota(jnp.int32, (cs, cs), 0) >= lax.broadcasted_iota(jnp.int32, (cs, cs), 1)).astype(F32)
    row = lax.broadcasted_iota(jnp.int32, (n, lanes), 0)
    lane = lax.broadcasted_iota(jnp.int32, (n, lanes), 1)
    head_mask = (row >> log_c) == (lane >> log_hd)
    ri = lax.broadcasted_iota(jnp.int32, (n, n), 0)
    ci = lax.broadcasted_iota(jnp.int32, (n, n), 1)
    same_head = (ri >> log_c) == (ci >> log_c)
    rt = ri & (cs - 1)
    ct = ci & (cs - 1)
    strict = same_head & (ct < rt)
    incl = same_head & (ct <= rt)
    eye = (ri == ci).astype(F32)
    diag = (lax.broadcasted_iota(jnp.int32, (LANES, LANES), 0)
            == lax.broadcasted_iota(jnp.int32, (LANES, LANES), 1))

    def stack(x):
        return jnp.where(head_mask, jnp.concatenate([x] * heads, axis=0), 0.0)

    def fold(x):
        out = x[0:cs]
        for hidx in range(1, heads):
            out = out + x[hidx * cs:(hidx + 1) * cs]
        return out

    scaled = []
    for i in range(seqs):
        rs = slice(i * cs, (i + 1) * cs)
        lw = lw_ref[rs, :]
        cum = _fdot(tri, lw)
        total = cum[cs - 1:cs]
        e_cum = jnp.exp(cum)
        e_neg = jnp.exp(-cum)
        e_rest = jnp.exp(total - cum)
        b = b_ref[rs, :]
        k = k_ref[rs, :]
        scaled.append(dict(a=a_ref[rs, :] * jnp.exp(cum - lw), r=r_ref[rs, :] * e_cum, b=b * e_neg, k=k * e_neg,
                           v=v_ref[rs, :], bh=b * e_rest, kh=k * e_rest, total=total))

    inst = [(i, s) for i in range(seqs) for s in range(n_stacks)]
    jj = range(len(inst))

    def stacked(name):
        return [stack(scaled[i][name][:, s * lanes:(s + 1) * lanes]) for i, s in inst]

    a_st, r_st, b_st, k_st, v_st, bh_st, kh_st = (stacked(nm) for nm in ("a", "r", "b", "k", "v", "bh", "kh"))

    gram = [_bdot_nt(jnp.concatenate([a_st[j], r_st[j]], axis=0), jnp.concatenate([b_st[j], k_st[j]], axis=0))
            for j in jj]
    l_ab = [jnp.where(strict, gram[j][:n, :n], 0.0) for j in jj]
    m_rb = [jnp.where(incl, gram[j][n:, :n], 0.0) for j in jj]
    l_ak_m_rk = [jnp.concatenate([jnp.where(strict, gram[j][:n, n:], 0.0), jnp.where(incl, gram[j][n:, n:], 0.0)],
                                 axis=0) for j in jj]

    t_inv = [eye + l_ab[j] for j in jj]
    x_pow = [_bdot(l_ab[j], l_ab[j]) for j in jj]
    for _ in range(log_c - 2):
        both = [_bdot(jnp.concatenate([x_pow[j], t_inv[j]], axis=0), x_pow[j]) for j in jj]
        t_inv = [t_inv[j] + both[j][n:] for j in jj]
        x_pow = [both[j][:n] for j in jj]
    t_inv = [t_inv[j] + _bdot(t_inv[j], x_pow[j]) for j in jj]

    lv = [_bdot(l_ak_m_rk[j], v_st[j]) for j in jj]
    aw = [_bdot(t_inv[j], jnp.concatenate([a_st[j], lv[j][:n]], axis=1)) for j in jj]
    rb = [_bdot(m_rb[j], aw[j]) for j in jj]
    r_bar = [fold(r_st[j] + rb[j][:, :lanes]) for j in jj]
    y0 = [fold(rb[j][:, lanes:] + lv[j][n:]) for j in jj]

    jp = [(j, p) for j in jj for p in range(pairs)]

    def pair_lanes(x, p, off=0):
        return x[:, off + p * LANES:off + (p + 1) * LANES]

    h = [h_scr[inst[j][0], inst[j][1] * pairs + p] for j, p in jp]
    yv = [_bdot_nt(pair_lanes(r_bar[j], p), h[q]) + pair_lanes(y0[j], p) for q, (j, p) in enumerate(jp)]
    for q, (j, p) in enumerate(jp):
        i, s = inst[j]
        y_ref[i * cs:(i + 1) * cs, s * lanes + p * LANES:s * lanes + (p + 1) * LANES] = yv[q]
    aw_t = [jnp.concatenate([pair_lanes(aw[j], p).T, pair_lanes(aw[j], p, lanes).T], axis=0) for j, p in jp]
    v_t = [pair_lanes(v_st[j], p).T for j, p in jp]
    pq = [_bdot(aw_t[q], pair_lanes(bh_st[j], p)) for q, (j, p) in enumerate(jp)]
    kv = [_bdot(v_t[q], pair_lanes(kh_st[j], p)) for q, (j, p) in enumerate(jp)]
    for q, (j, p) in enumerate(jp):
        i, s = inst[j]
        decay = pair_lanes(scaled[i]["total"], p, s * lanes)
        phi_t = jnp.where(diag, jnp.exp(decay), 0.0) + pq[q][:LANES]
        h_scr[i, s * pairs + p] = _bdot(h[q], phi_t) + (pq[q][LANES:] + kv[q])

    @pl.when(c == pl.num_programs(1) - 1)
    def _():
        for i in range(seqs):
            for p in range(n_pairs):
                both = h_scr[i, p]
                hout_ref[i, 2 * p] = both[:hd, :hd]
                hout_ref[i, 2 * p + 1] = both[hd:, hd:]


def _wkv(r, lw, k, v, a, b, h0, batch, seq_len, hd):
    rows, w = r.shape
    chunk = min(seq_len, RWKV_CHUNK)
    assert seq_len % chunk == 0 and chunk % SUBLANES == 0 and LANES % chunk == 0
    pairs = LANES // (2 * chunk)
    n_chunks = seq_len // chunk
    n_pairs = w // LANES
    seqs = 2 if (n_chunks == 1 and batch % 2 == 0) else 1
    vec = pl.BlockSpec((seqs * chunk, w), lambda bi, ci: (bi * n_chunks + ci, 0))
    st = pl.BlockSpec((seqs, w // hd, hd, hd), lambda bi, ci: (bi, 0, 0, 0))
    return pl.pallas_call(
        functools.partial(_wkv_body, chunk=chunk, pairs=pairs, seqs=seqs, width=w, hd=hd),
        grid=(batch // seqs, n_chunks),
        in_specs=[vec] * 6 + [st],
        out_specs=[vec, st],
        out_shape=[jax.ShapeDtypeStruct((rows, w), F32),
                   jax.ShapeDtypeStruct((batch, w // hd, hd, hd), F32)],
        scratch_shapes=[pltpu.VMEM((seqs, n_pairs, LANES, LANES), F32)],
        compiler_params=_params("parallel", "arbitrary"),
        name="rwkv_chunks",
    )(r, lw, k, v, a, b, h0)


def _rwkv_post_body(y_ref, bonus_ref, g_ref, lw_ref, lb_ref, ones_ref, o_ref, *, hd):
    y = y_ref[...]
    mu = _head_sum(y, ones_ref) * (1.0 / hd)
    d = y - mu
    var = _head_sum(d * d, ones_ref) * (1.0 / hd)
    yn = d * lax.rsqrt(var + GN_EPS) * lw_ref[...] + lb_ref[...]
    o_ref[...] = ((yn + bonus_ref[...]) * g_ref[...]).astype(o_ref.dtype)


def _rwkv_post(y, bonus, g, prm, hd):
    rows, w = y.shape
    tm = _row_tile(rows, 512)
    blk = pl.BlockSpec((tm, w), lambda i: (i, 0))
    row_spec = pl.BlockSpec((1, w), lambda i: (0, 0))
    return pl.pallas_call(
        functools.partial(_rwkv_post_body, hd=hd),
        grid=(rows // tm,),
        in_specs=[blk, blk, blk, row_spec, row_spec, pl.BlockSpec((LANES, LANES), lambda i: (0, 0))],
        out_specs=blk,
        out_shape=jax.ShapeDtypeStruct((rows, w), BF16),
        compiler_params=_params("parallel"),
        name="rwkv_post",
    )(y, bonus, g, prm["lnx_w"], prm["lnx_b"], prm["ones2"])


def _sgu_body(gu_ref, gv_ref, sg_ref, sb_ref, wm_ref, bias_ref, o_ref, v_ref, *, groups, seq_rows):
    u = jax.nn.gelu(gu_ref[...])
    vf = jax.nn.gelu(gv_ref[...])
    mu = jnp.mean(vf, axis=-1, keepdims=True)
    d = vf - mu
    var = jnp.mean(d * d, axis=-1, keepdims=True)
    v = (d * lax.rsqrt(var + LN_EPS)) * sg_ref[...] + sb_ref[...]
    v_ref[...] = v
    rows = v.shape[0]
    gd = v.shape[1] // groups
    ri = lax.broadcasted_iota(jnp.int32, (rows, rows), 0)
    ci = lax.broadcasted_iota(jnp.int32, (rows, rows), 1)
    causal = ri >= ci
    if seq_rows is not None:
        shift = seq_rows.bit_length() - 1
        causal = causal & ((ri >> shift) == (ci >> shift))
        pos = ((lax.broadcasted_iota(jnp.int32, (rows, seq_rows), 0) & (seq_rows - 1))
               == lax.broadcasted_iota(jnp.int32, (rows, seq_rows), 1)).astype(F32)
    for g in range(groups):
        sl = slice(g * gd, (g + 1) * gd)
        if seq_rows is None:
            w_full, bias = wm_ref[g], bias_ref[g]
        else:
            w_full = lax.dot_general(_fdot(pos, wm_ref[g][:seq_rows, :seq_rows]), pos, (((1,), (1,)), ((), ())),
                                     precision=HIGHEST, preferred_element_type=F32)
            bias = _fdot(pos, bias_ref[g])
        s = _bdot(jnp.where(causal, w_full, 0.0), v[:, sl]) + bias
        o_ref[:, sl] = (u[:, sl] * s).astype(o_ref.dtype)


def _sgu(guvq, w_mix, bias, prm, seq_rows):
    rows = guvq.shape[0]
    w = prm["width"]
    groups, chunk, _ = w_mix.shape
    cs = chunk if seq_rows is None else _row_tile(rows, 2 * LANES)
    row_spec = pl.BlockSpec((1, w), lambda i: (0, 0))
    out = pl.BlockSpec((cs, w), lambda i: (i, 0))
    return pl.pallas_call(
        functools.partial(_sgu_body, groups=groups, seq_rows=seq_rows),
        grid=(rows // cs,),
        in_specs=[pl.BlockSpec((cs, w), lambda i: (i, 0)), pl.BlockSpec((cs, w), lambda i: (i, 1)),
                  row_spec, row_spec,
                  pl.BlockSpec(w_mix.shape, lambda i: (0, 0, 0)), pl.BlockSpec(bias.shape, lambda i: (0, 0, 0))],
        out_specs=[out, out],
        out_shape=[jax.ShapeDtypeStruct((rows, w), BF16), jax.ShapeDtypeStruct((rows, w), F32)],
        compiler_params=_params("parallel"),
        name="sgu",
    )(guvq, guvq, prm["sgu_g"], prm["sgu_b"], w_mix, bias)


def _xattn_body(q_ref, k_ref, v_ref, o_ref, *, heads, seqs, tq, m):
    hd = q_ref.shape[1] // heads
    scale = hd ** -0.5
    cases = [(h, i) for h in range(heads) for i in range(seqs)]
    cols = lambda h: slice(h * hd, (h + 1) * hd)
    mem = lambda i: slice(i * m, (i + 1) * m)
    s = [_bdot_nt(q_ref[i * tq:(i + 1) * tq, cols(h)], k_ref[mem(i), cols(h)]) * scale for h, i in cases]
    e = [jnp.exp(x - jnp.max(x, axis=-1, keepdims=True)) for x in s]
    p = [x / jnp.sum(x, axis=-1, keepdims=True) for x in e]
    o = [_bdot(p[c], v_ref[mem(i), cols(h)]) for c, (h, i) in enumerate(cases)]
    for h in range(heads):
        o_ref[:, cols(h)] = jnp.concatenate(o[h * seqs:(h + 1) * seqs], axis=0).astype(o_ref.dtype)


def _xattn(guvq, mem_k, mem_v, seq_len, w):
    batch, m, heads, hd = mem_k.shape
    rows = guvq.shape[0]
    tq = _row_tile(seq_len, 512)
    nq = seq_len // tq
    seqs = max(1, 32 // tq)
    assert batch % seqs == 0 and (seqs == 1 or nq == 1)
    kv = pl.BlockSpec((seqs * m, w), lambda bi, qi: (bi, 0))
    qo = lambda col: pl.BlockSpec((seqs * tq, w), lambda bi, qi: (bi * nq + qi, col))
    return pl.pallas_call(
        functools.partial(_xattn_body, heads=heads, seqs=seqs, tq=tq, m=m),
        grid=(batch // seqs, nq),
        in_specs=[qo(2), kv, kv],
        out_specs=qo(0),
        out_shape=jax.ShapeDtypeStruct((rows, w), BF16),
        compiler_params=_params("parallel", "parallel"),
        name="mem_xattn",
    )(guvq, mem_k.reshape(batch * m, w), mem_v.reshape(batch * m, w))


def _branch_body(a_ref, b_ref, c_ref, wb_ref, g0_ref, g1_ref, g2_ref, o_ref, wb_bf):
    @pl.when(pl.program_id(1) == 0)
    def _():
        wb_bf[...] = wb_ref[...].astype(BF16)

    acc = jax.nn.sigmoid(g0_ref[...]) * jnp.dot(a_ref[...], wb_bf[0], preferred_element_type=F32)
    acc = acc + jax.nn.sigmoid(g1_ref[...]) * jnp.dot(b_ref[...], wb_bf[1], preferred_element_type=F32)
    acc = acc + jax.nn.sigmoid(g2_ref[...]) * jnp.dot(c_ref[...], wb_bf[2], preferred_element_type=F32)
    o_ref[...] = acc.astype(o_ref.dtype)


def _branch_mix(a_out, b_out, c_out, w_branch, gt):
    rows, w = a_out.shape
    nb, _, d = w_branch.shape
    assert nb == 3
    tm = _row_tile(rows, 1024)
    tn = _col_tile(d, 512)
    nj = d // tn
    br = pl.BlockSpec((tm, w), lambda j, i: (i, 0))
    gate = lambda n: pl.BlockSpec((tm, tn), lambda j, i: (i, j + n * nj))
    return pl.pallas_call(
        _branch_body,
        grid=(nj, rows // tm),
        in_specs=[br, br, br, pl.BlockSpec((nb, w, tn), lambda j, i: (0, 0, j)), gate(0), gate(1), gate(2)],
        out_specs=pl.BlockSpec((tm, tn), lambda j, i: (i, j)),
        out_shape=jax.ShapeDtypeStruct((rows, d), BF16),
        scratch_shapes=[pltpu.VMEM((nb, w, tn), BF16)],
        compiler_params=_params("parallel", "arbitrary"),
        name="branch_mix",
    )(a_out, b_out, c_out, w_branch, gt, gt, gt)


def _mm_res_norm_body(x_ref, w_ref, res_ref, g_ref, g2_ref, y_ref, *rest, nk, with_next):
    if with_next:
        yn_ref, acc_ref = rest
    else:
        (acc_ref,) = rest
    kk = pl.program_id(1)
    part = jnp.dot(x_ref[...], w_ref[...], preferred_element_type=F32)

    @pl.when(kk == 0)
    def _():
        acc_ref[...] = part

    @pl.when(kk > 0)
    def _():
        acc_ref[...] += part

    @pl.when(kk == nk - 1)
    def _():
        f = acc_ref[...]
        y = res_ref[...] + (f * lax.rsqrt(jnp.mean(f * f, axis=-1, keepdims=True) + RMS_EPS)) * g_ref[...]
        y_ref[...] = y
        if with_next:
            yn = y * lax.rsqrt(jnp.mean(y * y, axis=-1, keepdims=True) + RMS_EPS)
            yn_ref[...] = (yn * g2_ref[...]).astype(yn_ref.dtype)


def _mm_res_norm(x, w, res, g, g_next=None):
    rows, k = x.shape
    d = w.shape[1]
    with_next = g_next is not None
    tm = _row_tile(rows, 512)
    tk = _col_tile(k, 1536)
    nk = k // tk
    blk = pl.BlockSpec((tm, d), lambda i, kk: (i, 0))
    row_spec = pl.BlockSpec((1, d), lambda i, kk: (0, 0))
    out_specs = [blk, blk] if with_next else [blk]
    out_shape = [jax.ShapeDtypeStruct((rows, d), F32)]
    if with_next:
        out_shape.append(jax.ShapeDtypeStruct((rows, d), BF16))
    g2 = g_next if with_next else g
    outs = pl.pallas_call(
        functools.partial(_mm_res_norm_body, nk=nk, with_next=with_next),
        grid=(rows // tm, nk),
        in_specs=[pl.BlockSpec((tm, tk), lambda i, kk: (i, kk)), pl.BlockSpec((tk, d), lambda i, kk: (kk, 0)),
                  blk, row_spec, row_spec],
        out_specs=out_specs,
        out_shape=out_shape,
        scratch_shapes=[pltpu.VMEM((tm, d), F32)],
        compiler_params=_params("parallel", "arbitrary"),
        name="matmul_res_norm",
    )(x, w, res, g.reshape(1, d), g2.reshape(1, d))
    return outs if with_next else outs[0]


def _up_conv_gate_body(x_ref, wg_ref, wv_ref, eg_ref, ev_ref, cwg_ref, cwv_ref, cbg_ref, cbv_ref,
                       o_ref, tg_ref, tv_ref, wg_bf, wv_bf, keep_g, keep_v, *, seq_rows, tm, taps):
    i = pl.program_id(1)

    @pl.when(i == 0)
    def _():
        wg_bf[...] = wg_ref[...].astype(BF16)
        wv_bf[...] = wv_ref[...].astype(BF16)
        keep_g[...] = jnp.zeros_like(keep_g)
        keep_v[...] = jnp.zeros_like(keep_v)

    x = x_ref[...]

    def conv(w_bf, e_ref, cw_ref, cb_ref, keep, tail_ref):
        u = jnp.dot(x, w_bf[...], preferred_element_type=F32)
        cw = cw_ref[...]
        acc = cb_ref[...] + cw[taps - 1:taps] * u
        if seq_rows >= tm:
            is_start = (i * tm) % seq_rows == 0
            halo = jnp.where(is_start, pltpu.roll(e_ref[...], SUBLANES - (taps - 1), 0), keep[...])
            for back in range(1, taps):
                acc = acc + cw[taps - 1 - back:taps - back] * _shift_rows(u, halo, back)
            keep[...] = u[tm - SUBLANES:tm]
            tail_ref[...] = u[tm - SUBLANES:tm]
        else:
            t = lax.broadcasted_iota(jnp.int32, u.shape, 0) & (seq_rows - 1)
            e = e_ref[...]
            for back in range(1, taps):
                up_by = taps - 1 - back
                state = pltpu.roll(e, tm - up_by, 0) if up_by else e
                prev = jnp.where(t < back, state, pltpu.roll(u, back, 0))
                acc = acc + cw[taps - 1 - back:taps - back] * prev
            n_lane_tiles = u.shape[1] // LANES
            for c in range(n_lane_tiles):
                keep[c] = u[:, c * LANES:(c + 1) * LANES]
            for idx in range(taps - 1):
                rows_t = pl.ds(seq_rows - (taps - 1) + idx, tm // seq_rows, stride=seq_rows)
                tail_ref[idx] = jnp.concatenate([keep[c, rows_t, :] for c in range(n_lane_tiles)], axis=1)
        return acc

    gate = conv(wg_bf, eg_ref, cwg_ref, cbg_ref, keep_g, tg_ref)
    val = conv(wv_bf, ev_ref, cwv_ref, cbv_ref, keep_v, tv_ref)
    o_ref[...] = (jax.nn.gelu(gate) * val).astype(o_ref.dtype)


def _up_conv_gate(x, w_up, state_rows, conv_w, conv_b, seq_rows):
    rows, d = x.shape
    f2 = w_up.shape[1]
    dff = f2 // 2
    taps = conv_w.shape[0]
    tm = _row_tile(rows if seq_rows == SUBLANES else seq_rows, 1024)
    tn = _col_tile(dff, 512)
    nj = dff // tn
    if seq_rows >= tm:
        assert seq_rows % tm == 0
        per_seq = lambda off: pl.BlockSpec((SUBLANES, tn), lambda j, i: ((i * tm) // seq_rows, j + off))
        st, tail = per_seq, per_seq(0)
        tail_shape = ((rows // seq_rows) * SUBLANES, dff)
        keep_shape = (SUBLANES, tn)
    else:
        assert seq_rows == SUBLANES and tm % seq_rows == 0
        st = lambda off: pl.BlockSpec((tm, tn), lambda j, i: (i, j + off))
        tail = pl.BlockSpec((taps - 1, tm // seq_rows, tn), lambda j, i: (0, i, j))
        tail_shape = (taps - 1, rows // seq_rows, dff)
        keep_shape = (tn // LANES, tm, LANES)
    wt = lambda off: pl.BlockSpec((d, tn), lambda j, i: (0, j + off))
    cw = lambda off: pl.BlockSpec((taps, tn), lambda j, i: (0, j + off))
    cb = lambda off: pl.BlockSpec((1, tn), lambda j, i: (0, j + off))
    act, tail_g, tail_v = pl.pallas_call(
        functools.partial(_up_conv_gate_body, seq_rows=seq_rows, tm=tm, taps=taps),
        grid=(nj, rows // tm),
        in_specs=[pl.BlockSpec((tm, d), lambda j, i: (i, 0)), wt(0), wt(nj), st(0), st(nj),
                  cw(0), cw(nj), cb(0), cb(nj)],
        out_specs=[pl.BlockSpec((tm, tn), lambda j, i: (i, j)), tail, tail],
        out_shape=[jax.ShapeDtypeStruct((rows, dff), BF16), jax.ShapeDtypeStruct(tail_shape, F32),
                   jax.ShapeDtypeStruct(tail_shape, F32)],
        scratch_shapes=[pltpu.VMEM((d, tn), BF16), pltpu.VMEM((d, tn), BF16),
                        pltpu.VMEM(keep_shape, F32), pltpu.VMEM(keep_shape, F32)],
        compiler_params=_params("parallel", "arbitrary"),
        name="up_conv_gate",
    )(x, w_up, w_up, state_rows, state_rows, conv_w, conv_w, conv_b.reshape(1, f2), conv_b.reshape(1, f2))
    tail = jnp.concatenate([tail_g, tail_v], axis=-1)
    if seq_rows >= tm:
        return act, tail.reshape(rows // seq_rows, SUBLANES, f2)[:, SUBLANES - (taps - 1):]
    return act, jnp.swapaxes(tail, 0, 1)


def _layer(x, shift_prev, wkv0, mem_k, mem_v, conv_prev, prm):
    batch, seq_len, d = x.shape
    rows = batch * seq_len
    w = prm["width"]
    hd = prm["head_dim"]
    x2 = x.reshape(rows, d)

    xn = _rmsnorm(x2, prm["g_pre_mix"], BF16)
    n_shift = prm["mu_shift"].shape[1]
    rw = _matmul(xn, prm["w_in"], 0, n_shift)
    guvq = _matmul(xn, prm["w_in"], n_shift, 3 * w)
    gt = _matmul(xn, prm["w_in"], n_shift + 3 * w)
    new_shift = rw.reshape(batch, seq_len, -1)[:, -1]

    r, lw, k, v, a, b, g, bonus = _rwkv_prep(rw, _pad_state_rows(shift_prev[:, None, :]), seq_len, prm)
    y, new_wkv = _wkv(r, lw, k, v, a, b, wkv0, batch, seq_len, hd)
    a_out = _rwkv_post(y, bonus, g, prm, hd)

    if seq_len % prm["sgu_chunk"] == 0:
        b_out, v_rows = _sgu(guvq, prm["w_s"], prm["sgu_bias"], prm, None)
    else:
        assert seq_len == SUBLANES
        b_out, v_rows = _sgu(guvq, prm["w_s"], prm["sgu_bias"][:, :SUBLANES], prm, SUBLANES)
    c_out = _xattn(guvq, mem_k, mem_v, seq_len, w)

    mix = _branch_mix(a_out, b_out, c_out, prm["w_branch"], gt)
    h, hn = _mm_res_norm(mix, prm["w_out"], x2, prm["g_post_mix"], prm["g_pre_ffn"])

    taps = prm["conv_w"].shape[0]
    act, conv_new = _up_conv_gate(hn, prm["w_up"], _pad_state_rows(conv_prev), prm["conv_w"], prm["conv_b"],
                                  seq_len)
    y_out = _mm_res_norm(act, prm["w_down"], h, prm["g_post_ffn"])
    return (y_out.reshape(batch, seq_len, d), new_shift, new_wkv,
            v_rows.reshape(batch, seq_len, w), conv_new)


def _prepare(l, g_pre_mix, w_in, mu_shift, w0, w_decay, a0, w_aaa, w_gate, k_k, k_a, r_k, lnx_w, lnx_b,
             sgu_g, sgu_b, w_s, b_s, w_branch, w_out, g_post_mix, g_pre_ffn, w_up, conv_w, conv_b, w_down,
             g_post_ffn):
    heads, hd = r_k.shape[1], r_k.shape[2]
    w = heads * hd
    d = w_in.shape[1]
    n_shift = mu_shift.shape[1]
    rank_d, rank_a, rank_g = w_decay.shape[1], w_aaa.shape[1], w_gate.shape[1]
    assert rank_d + rank_a == LANES and rank_g == LANES and n_shift == 3 * w + 2 * LANES
    groups, sgu_chunk, _ = w_s.shape[1:]
    row = lambda t: t[l].reshape(1, -1)
    lane_head = jnp.arange(LANES) // hd
    gd = w // groups
    return dict(
        width=w, head_dim=hd, sgu_chunk=sgu_chunk,
        g_pre_mix=g_pre_mix[l],
        w_in=w_in[l],
        mu_shift=row(mu_shift), w0=row(w0), a0=row(a0),
        wd_pad=jnp.pad(w_decay[l], ((0, rank_a), (0, 0))),
        wa_pad=jnp.pad(w_aaa[l], ((rank_d, 0), (0, 0))),
        w_gate=w_gate[l],
        k_k=row(k_k), k_a=row(k_a), r_k=row(r_k), lnx_w=row(lnx_w), lnx_b=row(lnx_b),
        ones2=(lane_head[:, None] == lane_head[None, :]).astype(BF16),
        sgu_g=row(sgu_g), sgu_b=row(sgu_b),
        w_s=w_s[l],
        sgu_bias=jnp.broadcast_to(b_s[l][:, :, None], (groups, sgu_chunk, gd)),
        w_branch=w_branch[l], w_out=w_out[l].astype(BF16),
        g_post_mix=g_post_mix[l], g_pre_ffn=g_pre_ffn[l],
        w_up=w_up[l], conv_w=conv_w[l], conv_b=conv_b[l], w_down=w_down[l].astype(BF16),
        g_post_ffn=g_post_ffn[l],
    )


def kernel(x_prompt, x_sample, mem_prompt, state_wkv, state_shift, cache_mem_k, cache_mem_v, state_ffn_conv, g_pre_mix, w_in, mu_shift, w0, w_decay, a0, w_aaa, w_gate, k_k, k_a, r_k, lnx_w, lnx_b, sgu_g, sgu_b, w_s, b_s, g_mem, w_mem_k, w_mem_v, w_branch, w_out, g_post_mix, g_pre_ffn, w_up, conv_w, conv_b, w_down, g_post_ffn):
    depth = w_in.shape[0]
    batch = x_prompt.shape[0]
    mem_len, d = mem_prompt.shape[1], mem_prompt.shape[2]
    xh, xhd = cache_mem_k.shape[3], cache_mem_k.shape[4]
    heads, hd = r_k.shape[1], r_k.shape[2]
    n_shift = mu_shift.shape[1]
    f2 = w_up.shape[2]
    taps = conv_w.shape[1]
    y_p, y_s = x_prompt, x_sample
    outs = [[] for _ in range(9)]
    for l in range(depth):
        prm = _prepare(l, g_pre_mix, w_in, mu_shift, w0, w_decay, a0, w_aaa, w_gate, k_k, k_a, r_k, lnx_w, lnx_b,
                       sgu_g, sgu_b, w_s, b_s, w_branch, w_out, g_post_mix, g_pre_ffn, w_up, conv_w, conv_b,
                       w_down, g_post_ffn)
        mn = _rmsnorm(mem_prompt.reshape(batch * mem_len, d), g_mem[l], BF16)
        mk_p = _matmul(mn, w_mem_k[l]).reshape(batch, mem_len, xh, xhd)
        mv_p = _matmul(mn, w_mem_v[l]).reshape(batch, mem_len, xh, xhd)
        y_p, sh_p, wkv_p, _, cv_p = _layer(
            y_p, jnp.zeros((batch, n_shift), F32), jnp.zeros((batch, heads, hd, hd), F32), mk_p, mv_p,
            jnp.zeros((batch, taps - 1, f2), F32), prm)
        y_s, sh_s, wkv_s, vr_s, cv_s = _layer(y_s, state_shift[l], state_wkv[l], cache_mem_k[l], cache_mem_v[l],
                                              state_ffn_conv[l], prm)
        for lst, val in zip(outs, (wkv_p, sh_p, mk_p, mv_p, cv_p, wkv_s, sh_s, vr_s, cv_s)):
            lst.append(val)
    return (y_p, y_s) + tuple(jnp.stack(lst) for lst in outs)
```

```python
import functools

import jax
import jax.numpy as jnp
from jax import lax
from jax.experimental import pallas as pl
from jax.experimental.pallas import tpu as pltpu

F32 = jnp.float32
BF16 = jnp.bfloat16
HIGHEST = lax.Precision.HIGHEST

LANES = 128
SUBLANES = 8
VMEM_LIMIT_BYTES = 56 * 1024 * 1024

RMS_EPS = 1e-6
LN_EPS = 1e-5
GN_EPS = 64e-5
KK_EPS = 1e-12
RWKV_CHUNK = 64


def _params(*semantics):
    return pltpu.CompilerParams(dimension_semantics=semantics, vmem_limit_bytes=VMEM_LIMIT_BYTES)


def _row_tile(rows, pref):
    t = min(rows, pref)
    while rows % t:
        t -= SUBLANES
    return t


def _col_tile(n, cap):
    best = n
    for t in range(LANES, min(n, cap) + 1, LANES):
        if n % t == 0:
            best = t
    return best


def _bdot(a, b):
    return jnp.dot(a.astype(BF16), b.astype(BF16), preferred_element_type=F32)


def _bdot_nt(a, b):
    return lax.dot_general(a.astype(BF16), b.astype(BF16), (((1,), (1,)), ((), ())),
                           preferred_element_type=F32)


def _fdot(a, b):
    return jnp.dot(a, b, precision=HIGHEST, preferred_element_type=F32)


def _dot3(a, b):
    a_hi = a.astype(BF16)
    b_hi = b.astype(BF16)
    a_lo = (a - a_hi.astype(F32)).astype(BF16)
    b_lo = (b - b_hi.astype(F32)).astype(BF16)
    dot = lambda x, y: jnp.dot(x, y, preferred_element_type=F32)
    return dot(a_hi, b_hi) + (dot(a_hi, b_lo) + dot(a_lo, b_hi))


def _rmsnorm_body(x_ref, g_ref, o_ref):
    x = x_ref[...]
    y = x * lax.rsqrt(jnp.mean(x * x, axis=-1, keepdims=True) + RMS_EPS)
    o_ref[...] = (y * g_ref[...]).astype(o_ref.dtype)


def _rmsnorm(x, g, out_dtype):
    rows, d = x.shape
    tm = _row_tile(rows, 512)
    return pl.pallas_call(
        _rmsnorm_body,
        grid=(rows // tm,),
        in_specs=[pl.BlockSpec((tm, d), lambda i: (i, 0)), pl.BlockSpec((1, d), lambda i: (0, 0))],
        out_specs=pl.BlockSpec((tm, d), lambda i: (i, 0)),
        out_shape=jax.ShapeDtypeStruct((rows, d), out_dtype),
        compiler_params=_params("parallel"),
        name="rmsnorm",
    )(x, g.reshape(1, d))


def _matmul_round_body(x_ref, w_ref, o_ref, wb_ref):
    @pl.when(pl.program_id(1) == 0)
    def _():
        wb_ref[...] = w_ref[...].astype(BF16)

    o_ref[...] = jnp.dot(x_ref[...], wb_ref[...], preferred_element_type=F32).astype(o_ref.dtype)


def _matmul_body(x_ref, w_ref, o_ref):
    o_ref[...] = jnp.dot(x_ref[...], w_ref[...], preferred_element_type=F32).astype(o_ref.dtype)


def _matmul(x, w, col_start=0, n_cols=None, out_dtype=F32):
    rows, k = x.shape
    n = w.shape[1] - col_start if n_cols is None else n_cols
    assert col_start % LANES == 0 and n % LANES == 0
    rounding = w.dtype != BF16
    tn = _col_tile(n, 1792)
    budget = VMEM_LIMIT_BYTES - (4 << 20)
    tm = _row_tile(rows, 1024)
    w_bytes = 2 * k * tn * (4 + 2) if rounding else 2 * k * tn * 2
    while w_bytes + 2 * tm * k * 2 + 2 * tm * tn * 4 > budget and tm % (2 * SUBLANES) == 0:
        tm //= 2
    w_spec = pl.BlockSpec((pl.Element(k), pl.Element(tn)),
                          lambda j, i: (0, pl.multiple_of(col_start + j * tn, LANES)))
    out_spec = pl.BlockSpec((tm, tn), lambda j, i: (i, j))
    out_shape = jax.ShapeDtypeStruct((rows, n), out_dtype)
    if rounding:
        out_spec = [out_spec, pl.BlockSpec((k, tn), lambda j, i: (0, j))]
        out_shape = [out_shape, jax.ShapeDtypeStruct((k, n), BF16)]
    return pl.pallas_call(
        _matmul_round_body if rounding else _matmul_body,
        grid=(n // tn, rows // tm),
        in_specs=[pl.BlockSpec((tm, k), lambda j, i: (i, 0)), w_spec],
        out_specs=out_spec,
        out_shape=out_shape,
        compiler_params=_params("parallel", "arbitrary"),
        name="matmul",
    )(x, w)


def _matmul_cached(x, prm, name, col_start=0, n_cols=None):
    key = (name, col_start)
    if key in prm:
        return _matmul(x, prm[key])
    out, prm[key] = _matmul(x, prm[name], col_start, n_cols)
    return out


def _shift_rows(x, halo, k):
    rolled = pltpu.roll(x, k, 0)
    row = lax.broadcasted_iota(jnp.int32, (SUBLANES, x.shape[1]), 0)
    top = rolled[0:SUBLANES]
    for j in range(k):
        top = jnp.where(row == j, halo[SUBLANES - k + j:SUBLANES - k + j + 1, :], top)
    return jnp.concatenate([top, rolled[SUBLANES:]], axis=0)


def _head_sum(x, ones_ref):
    ones = ones_ref[...]
    hi = x.astype(BF16)
    lo = (x - hi.astype(F32)).astype(BF16)
    outs = []
    for j in range(x.shape[1] // LANES):
        sl = slice(j * LANES, (j + 1) * LANES)
        outs.append(jnp.dot(hi[:, sl], ones, preferred_element_type=F32)
                    + jnp.dot(lo[:, sl], ones, preferred_element_type=F32))
    return jnp.concatenate(outs, axis=1)


def _pad_state_rows(state):
    b, n, f = state.shape
    return jnp.pad(state, ((0, 0), (0, SUBLANES - n), (0, 0))).reshape(b * SUBLANES, f)


def _rwkv_prep_body(rw_ref, halo_ref, sp_ref, mu_ref, w0_ref, wd_ref, a0_ref, wa_ref, wg_ref,
                    kk_ref, ka_ref, rk_ref, ones_ref,
                    r_o, lw_o, k_o, v_o, a_o, b_o, g_o, bonus_o, *, seq_rows, tm, width):
    x = rw_ref[...]
    if seq_rows >= tm:
        is_start = (pl.program_id(0) * tm) % seq_rows == 0
        halo = jnp.where(is_start, pltpu.roll(sp_ref[...], SUBLANES - 1, 0), halo_ref[...])
        prev = _shift_rows(x, halo, 1)
    else:
        t = lax.broadcasted_iota(jnp.int32, x.shape, 0) & (seq_rows - 1)
        prev = jnp.where(t == 0, sp_ref[...], pltpu.roll(x, 1, 0))
    xs = x + mu_ref[...] * (prev - x)

    w = width
    r = xs[:, 0:w]
    k = xs[:, w:2 * w]
    v = xs[:, 2 * w:3 * w]
    x_wa = xs[:, 3 * w:3 * w + LANES]
    x_g = xs[:, 3 * w + LANES:3 * w + 2 * LANES]

    z = -(w0_ref[...] + _dot3(jnp.tanh(x_wa), wd_ref[...]))
    softplus = jnp.maximum(z, 0.0) + jnp.log(1.0 + jnp.exp(-jnp.abs(z)))
    lw = -jnp.exp(-softplus - 0.5)
    a_lr = jax.nn.sigmoid(a0_ref[...] + _bdot(x_wa, wa_ref[...]))
    g = _bdot(jax.nn.sigmoid(x_g), wg_ref[...])

    kk = k * kk_ref[...]
    k_mod = k * (1.0 + (a_lr - 1.0) * ka_ref[...])
    kk = kk / jnp.maximum(jnp.sqrt(_head_sum(kk * kk, ones_ref)), KK_EPS)
    bonus = _head_sum(r * k_mod * rk_ref[...], ones_ref) * v

    r_o[...] = r
    lw_o[...] = lw
    k_o[...] = k_mod
    v_o[...] = v.astype(v_o.dtype)
    a_o[...] = -kk
    b_o[...] = kk * a_lr
    g_o[...] = g
    bonus_o[...] = bonus


def _rwkv_prep(rw, shift_rows, seq_rows, prm):
    rows, n_shift = rw.shape
    w = prm["width"]
    tm = _row_tile(rows, 256)
    if seq_rows >= tm:
        assert seq_rows % tm == 0
        sp_spec = pl.BlockSpec((SUBLANES, n_shift), lambda i: ((i * tm) // seq_rows, 0))
    else:
        assert seq_rows == SUBLANES and tm % seq_rows == 0
        sp_spec = pl.BlockSpec((tm, n_shift), lambda i: (i, 0))
    row_spec = lambda c: pl.BlockSpec((1, c), lambda i: (0, 0))
    full = lambda a: pl.BlockSpec(a.shape, lambda i: (0, 0))
    out_spec = pl.BlockSpec((tm, w), lambda i: (i, 0))
    return pl.pallas_call(
        functools.partial(_rwkv_prep_body, seq_rows=seq_rows, tm=tm, width=w),
        grid=(rows // tm,),
        in_specs=[pl.BlockSpec((tm, n_shift), lambda i: (i, 0)),
                  pl.BlockSpec((SUBLANES, n_shift), lambda i: (jnp.maximum(i * (tm // SUBLANES) - 1, 0), 0)),
                  sp_spec, row_spec(n_shift), row_spec(w), full(prm["wd_pad"]), row_spec(w), full(prm["wa_pad"]),
                  full(prm["w_gate"]), row_spec(w), row_spec(w), row_spec(w), full(prm["ones2"])],
        out_specs=[out_spec] * 8,
        out_shape=[jax.ShapeDtypeStruct((rows, w), BF16 if i == 3 else F32) for i in range(8)],
        compiler_params=_params("parallel"),
        name="rwkv_prep",
    )(rw, rw, shift_rows, prm["mu_shift"], prm["w0"], prm["wd_pad"], prm["a0"], prm["wa_pad"], prm["w_gate"],
      prm["k_k"], prm["k_a"], prm["r_k"], prm["ones2"])


def _wkv_body(r_ref, lw_ref, k_ref, v_ref, a_ref, b_ref, h0_ref, y_ref, hout_ref, h_scr,
              *, chunk, pairs, seqs, width, hd):
    c = pl.program_id(1)
    n_pairs = width // LANES

    @pl.when(c == 0)
    def _():
        zero = jnp.zeros((hd, hd), F32)
        for i in range(seqs):
            for p in range(n_pairs):
                h_scr[i, p] = jnp.concatenate(
                    [jnp.concatenate([h0_ref[i, 2 * p], zero], axis=1),
                     jnp.concatenate([zero, h0_ref[i, 2 * p + 1]], axis=1)], axis=0)

    cs = chunk
    lanes = LANES * pairs
    heads = 2 * pairs
    n = heads * cs
    log_c = cs.bit_length() - 1
    log_hd = hd.bit_length() - 1
    n_stacks = width // lanes

    tri = (lax.broadcasted_iota(jnp.int32, (cs, cs), 0) >= lax.broadcasted_iota(jnp.int32, (cs, cs), 1)).astype(F32)
    row = lax.broadcasted_iota(jnp.int32, (n, lanes), 0)
    lane = lax.broadcasted_iota(jnp.int32, (n, lanes), 1)
    head_mask = (row >> log_c) == (lane >> log_hd)
    ri = lax.broadcasted_iota(jnp.int32, (n, n), 0)
    ci = lax.broadcasted_iota(jnp.int32, (n, n), 1)
    same_head = (ri >> log_c) == (ci >> log_c)
    rt = ri & (cs - 1)
    ct = ci & (cs - 1)
    strict = same_head & (ct < rt)
    incl = same_head & (ct <= rt)
    eye = (ri == ci).astype(F32)
    diag = (lax.broadcasted_iota(jnp.int32, (LANES, LANES), 0)
            == lax.broadcasted_iota(jnp.int32, (LANES, LANES), 1))

    def stack(x):
        return jnp.where(head_mask, jnp.concatenate([x] * heads, axis=0), 0.0)

    def fold(x):
        out = x[0:cs]
        for hidx in range(1, heads):
            out = out + x[hidx * cs:(hidx + 1) * cs]
        return out

    scaled = []
    for i in range(seqs):
        rs = slice(i * cs, (i + 1) * cs)
        lw = lw_ref[rs, :]
        cum = _fdot(tri, lw)
        total = cum[cs - 1:cs]
        e_cum = jnp.exp(cum)
        e_neg = jnp.exp(-cum)
        e_rest = jnp.exp(total - cum)
        b = b_ref[rs, :]
        k = k_ref[rs, :]
        scaled.append(dict(a=a_ref[rs, :] * jnp.exp(cum - lw), r=r_ref[rs, :] * e_cum, b=b * e_neg, k=k * e_neg,
                           v=v_ref[rs, :].astype(F32), bh=b * e_rest, kh=k * e_rest, total=total))

    inst = [(i, s) for i in range(seqs) for s in range(n_stacks)]
    jj = range(len(inst))

    def stacked(name):
        return [stack(scaled[i][name][:, s * lanes:(s + 1) * lanes]) for i, s in inst]

    a_st, r_st, b_st, k_st, v_st, bh_st, kh_st = (stacked(nm) for nm in ("a", "r", "b", "k", "v", "bh", "kh"))

    gram = [_bdot_nt(jnp.concatenate([a_st[j], r_st[j]], axis=0), jnp.concatenate([b_st[j], k_st[j]], axis=0))
            for j in jj]
    l_ab = [jnp.where(strict, gram[j][:n, :n], 0.0) for j in jj]
    m_rb = [jnp.where(incl, gram[j][n:, :n], 0.0) for j in jj]
    l_ak_m_rk = [jnp.concatenate([jnp.where(strict, gram[j][:n, n:], 0.0), jnp.where(incl, gram[j][n:, n:], 0.0)],
                                 axis=0) for j in jj]

    t_inv = [eye + l_ab[j] for j in jj]
    x_pow = [_bdot(l_ab[j], l_ab[j]) for j in jj]
    for _ in range(log_c - 2):
        both = [_bdot(jnp.concatenate([x_pow[j], t_inv[j]], axis=0), x_pow[j]) for j in jj]
        t_inv = [t_inv[j] + both[j][n:] for j in jj]
        x_pow = [both[j][:n] for j in jj]
    t_inv = [t_inv[j] + _bdot(t_inv[j], x_pow[j]) for j in jj]

    lv = [_bdot(l_ak_m_rk[j], v_st[j]) for j in jj]
    aw = [_bdot(t_inv[j], jnp.concatenate([a_st[j], lv[j][:n]], axis=1)) for j in jj]
    rb = [_bdot(m_rb[j], aw[j]) for j in jj]
    r_bar = [fold(r_st[j] + rb[j][:, :lanes]) for j in jj]
    y0 = [fold(rb[j][:, lanes:] + lv[j][n:]) for j in jj]

    jp = [(j, p) for j in jj for p in range(pairs)]

    def pair_lanes(x, p, off=0):
        return x[:, off + p * LANES:off + (p + 1) * LANES]

    h = [h_scr[inst[j][0], inst[j][1] * pairs + p] for j, p in jp]
    yv = [_bdot_nt(pair_lanes(r_bar[j], p), h[q]) + pair_lanes(y0[j], p) for q, (j, p) in enumerate(jp)]
    for q, (j, p) in enumerate(jp):
        i, s = inst[j]
        y_ref[i * cs:(i + 1) * cs, s * lanes + p * LANES:s * lanes + (p + 1) * LANES] = yv[q]
    aw_t = [jnp.concatenate([pair_lanes(aw[j], p).T, pair_lanes(aw[j], p, lanes).T], axis=0) for j, p in jp]
    v_t = [pair_lanes(v_st[j], p).T for j, p in jp]
    pq = [_bdot(aw_t[q], pair_lanes(bh_st[j], p)) for q, (j, p) in enumerate(jp)]
    kv = [_bdot(v_t[q], pair_lanes(kh_st[j], p)) for q, (j, p) in enumerate(jp)]
    for q, (j, p) in enumerate(jp):
        i, s = inst[j]
        decay = pair_lanes(scaled[i]["total"], p, s * lanes)
        phi_t = jnp.where(diag, jnp.exp(decay), 0.0) + pq[q][:LANES]
        h_scr[i, s * pairs + p] = _bdot(h[q], phi_t) + (pq[q][LANES:] + kv[q])

    @pl.when(c == pl.num_programs(1) - 1)
    def _():
        for i in range(seqs):
            for p in range(n_pairs):
                both = h_scr[i, p]
                hout_ref[i, 2 * p] = both[:hd, :hd]
                hout_ref[i, 2 * p + 1] = both[hd:, hd:]


def _wkv(r, lw, k, v, a, b, h0, batch, seq_len, hd):
    rows, w = r.shape
    chunk = min(seq_len, RWKV_CHUNK)
    assert seq_len % chunk == 0 and chunk % SUBLANES == 0 and LANES % chunk == 0
    pairs = LANES // (2 * chunk)
    n_chunks = seq_len // chunk
    n_pairs = w // LANES
    seqs = 2 if (n_chunks == 1 and batch % 2 == 0) else 1
    vec = pl.BlockSpec((seqs * chunk, w), lambda bi, ci: (bi * n_chunks + ci, 0))
    st = pl.BlockSpec((seqs, w // hd, hd, hd), lambda bi, ci: (bi, 0, 0, 0))
    return pl.pallas_call(
        functools.partial(_wkv_body, chunk=chunk, pairs=pairs, seqs=seqs, width=w, hd=hd),
        grid=(batch // seqs, n_chunks),
        in_specs=[vec] * 6 + [st],
        out_specs=[vec, st],
        out_shape=[jax.ShapeDtypeStruct((rows, w), F32),
                   jax.ShapeDtypeStruct((batch, w // hd, hd, hd), F32)],
        scratch_shapes=[pltpu.VMEM((seqs, n_pairs, LANES, LANES), F32)],
        compiler_params=_params("parallel", "arbitrary"),
        name="rwkv_chunks",
    )(r, lw, k, v, a, b, h0)


def _rwkv_post_body(y_ref, bonus_ref, g_ref, lw_ref, lb_ref, ones_ref, o_ref, *, hd):
    y = y_ref[...]
    mu = _head_sum(y, ones_ref) * (1.0 / hd)
    d = y - mu
    var = _head_sum(d * d, ones_ref) * (1.0 / hd)
    yn = d * lax.rsqrt(var + GN_EPS) * lw_ref[...] + lb_ref[...]
    o_ref[...] = ((yn + bonus_ref[...]) * g_ref[...]).astype(o_ref.dtype)


def _rwkv_post(y, bonus, g, prm, hd):
    rows, w = y.shape
    tm = _row_tile(rows, 512)
    blk = pl.BlockSpec((tm, w), lambda i: (i, 0))
    row_spec = pl.BlockSpec((1, w), lambda i: (0, 0))
    return pl.pallas_call(
        functools.partial(_rwkv_post_body, hd=hd),
        grid=(rows // tm,),
        in_specs=[blk, blk, blk, row_spec, row_spec, pl.BlockSpec((LANES, LANES), lambda i: (0, 0))],
        out_specs=blk,
        out_shape=jax.ShapeDtypeStruct((rows, w), BF16),
        compiler_params=_params("parallel"),
        name="rwkv_post",
    )(y, bonus, g, prm["lnx_w"], prm["lnx_b"], prm["ones2"])


def _sgu_body(gu_ref, gv_ref, sg_ref, sb_ref, wm_ref, bias_ref, o_ref, *maybe_v, groups, seq_rows):
    u = jax.nn.gelu(gu_ref[...])
    vf = jax.nn.gelu(gv_ref[...])
    mu = jnp.mean(vf, axis=-1, keepdims=True)
    d = vf - mu
    var = jnp.mean(d * d, axis=-1, keepdims=True)
    v = (d * lax.rsqrt(var + LN_EPS)) * sg_ref[...] + sb_ref[...]
    if maybe_v:
        maybe_v[0][...] = v
    rows = v.shape[0]
    gd = v.shape[1] // groups
    ri = lax.broadcasted_iota(jnp.int32, (rows, rows), 0)
    ci = lax.broadcasted_iota(jnp.int32, (rows, rows), 1)
    causal = ri >= ci
    if seq_rows is not None:
        shift = seq_rows.bit_length() - 1
        causal = causal & ((ri >> shift) == (ci >> shift))
        pos = ((lax.broadcasted_iota(jnp.int32, (rows, seq_rows), 0) & (seq_rows - 1))
               == lax.broadcasted_iota(jnp.int32, (rows, seq_rows), 1)).astype(F32)
    for g in range(groups):
        sl = slice(g * gd, (g + 1) * gd)
        if seq_rows is None:
            w_full, bias = wm_ref[g], bias_ref[g]
        else:
            w_full = lax.dot_general(_fdot(pos, wm_ref[g][:seq_rows, :seq_rows]), pos, (((1,), (1,)), ((), ())),
                                     precision=HIGHEST, preferred_element_type=F32)
            bias = _fdot(pos, bias_ref[g])
        s = _bdot(jnp.where(causal, w_full, 0.0), v[:, sl]) + bias
        o_ref[:, sl] = (u[:, sl] * s).astype(o_ref.dtype)


def _sgu(guvq, w_mix, bias, prm, seq_rows, want_v_rows):
    rows = guvq.shape[0]
    w = prm["width"]
    groups, chunk, _ = w_mix.shape
    cs = chunk if seq_rows is None else _row_tile(rows, 2 * LANES)
    row_spec = pl.BlockSpec((1, w), lambda i: (0, 0))
    out = pl.BlockSpec((cs, w), lambda i: (i, 0))
    outs = pl.pallas_call(
        functools.partial(_sgu_body, groups=groups, seq_rows=seq_rows),
        grid=(rows // cs,),
        in_specs=[pl.BlockSpec((cs, w), lambda i: (i, 0)), pl.BlockSpec((cs, w), lambda i: (i, 1)),
                  row_spec, row_spec,
                  pl.BlockSpec(w_mix.shape, lambda i: (0, 0, 0)), pl.BlockSpec(bias.shape, lambda i: (0, 0, 0))],
        out_specs=[out, out] if want_v_rows else [out],
        out_shape=[jax.ShapeDtypeStruct((rows, w), BF16)] + ([jax.ShapeDtypeStruct((rows, w), F32)]
                                                           if want_v_rows else []),
        compiler_params=_params("parallel"),
        name="sgu",
    )(guvq, guvq, prm["sgu_g"], prm["sgu_b"], w_mix, bias)
    return (outs[0], outs[1]) if want_v_rows else (outs[0], None)


def _xattn_body(q_ref, k_ref, v_ref, o_ref, *, heads, seqs, tq, m):
    hd = q_ref.shape[1] // heads
    scale = hd ** -0.5
    cases = [(h, i) for h in range(heads) for i in range(seqs)]
    cols = lambda h: slice(h * hd, (h + 1) * hd)
    mem = lambda i: slice(i * m, (i + 1) * m)
    s = [_bdot_nt(q_ref[i * tq:(i + 1) * tq, cols(h)], k_ref[mem(i), cols(h)]) * scale for h, i in cases]
    e = [jnp.exp(x - jnp.max(x, axis=-1, keepdims=True)) for x in s]
    p = [x / jnp.sum(x, axis=-1, keepdims=True) for x in e]
    o = [_bdot(p[c], v_ref[mem(i), cols(h)]) for c, (h, i) in enumerate(cases)]
    for h in range(heads):
        o_ref[:, cols(h)] = jnp.concatenate(o[h * seqs:(h + 1) * seqs], axis=0).astype(o_ref.dtype)


def _xattn(guvq, mem_k, mem_v, seq_len, w):
    batch, m, heads, hd = mem_k.shape
    rows = guvq.shape[0]
    tq = _row_tile(seq_len, 512)
    nq = seq_len // tq
    seqs = max(1, 32 // tq)
    assert batch % seqs == 0 and (seqs == 1 or nq == 1)
    kv = pl.BlockSpec((seqs * m, w), lambda bi, qi: (bi, 0))
    qo = lambda col: pl.BlockSpec((seqs * tq, w), lambda bi, qi: (bi * nq + qi, col))
    return pl.pallas_call(
        functools.partial(_xattn_body, heads=heads, seqs=seqs, tq=tq, m=m),
        grid=(batch // seqs, nq),
        in_specs=[qo(2), kv, kv],
        out_specs=qo(0),
        out_shape=jax.ShapeDtypeStruct((rows, w), BF16),
        compiler_params=_params("parallel", "parallel"),
        name="mem_xattn",
    )(guvq, mem_k.reshape(batch * m, w), mem_v.reshape(batch * m, w))


def _branch_body(a_ref, b_ref, c_ref, wb_ref, g0_ref, g1_ref, g2_ref, o_ref, *maybe_wb, rounding):
    if rounding:
        wb_bf = maybe_wb[0]

        @pl.when(pl.program_id(1) == 0)
        def _():
            wb_bf[...] = wb_ref[...].astype(BF16)
    else:
        wb_bf = wb_ref

    acc = jax.nn.sigmoid(g0_ref[...]) * jnp.dot(a_ref[...], wb_bf[0], preferred_element_type=F32)
    acc = acc + jax.nn.sigmoid(g1_ref[...]) * jnp.dot(b_ref[...], wb_bf[1], preferred_element_type=F32)
    acc = acc + jax.nn.sigmoid(g2_ref[...]) * jnp.dot(c_ref[...], wb_bf[2], preferred_element_type=F32)
    o_ref[...] = acc.astype(o_ref.dtype)


def _branch_mix(a_out, b_out, c_out, w_branch, gt):
    rows, w = a_out.shape
    nb, _, d = w_branch.shape
    assert nb == 3
    rounding = w_branch.dtype != BF16
    tm = _row_tile(rows, 1024)
    tn = _col_tile(d, 512)
    nj = d // tn
    br = pl.BlockSpec((tm, w), lambda j, i: (i, 0))
    gate = lambda n: pl.BlockSpec((tm, tn), lambda j, i: (i, j + n * nj))
    w_spec = pl.BlockSpec((nb, w, tn), lambda j, i: (0, 0, j))
    out_specs = [pl.BlockSpec((tm, tn), lambda j, i: (i, j))]
    out_shape = [jax.ShapeDtypeStruct((rows, d), BF16)]
    if rounding:
        out_specs.append(w_spec)
        out_shape.append(jax.ShapeDtypeStruct(w_branch.shape, BF16))
    outs = pl.pallas_call(
        functools.partial(_branch_body, rounding=rounding),
        grid=(nj, rows // tm),
        in_specs=[br, br, br, w_spec, gate(0), gate(1), gate(2)],
        out_specs=out_specs,
        out_shape=out_shape,
        compiler_params=_params("parallel", "arbitrary"),
        name="branch_mix",
    )(a_out, b_out, c_out, w_branch, gt, gt, gt)
    return outs[0], (outs[1] if rounding else w_branch)


def _mm_res_norm_body(x_ref, w_ref, res_ref, g_ref, g2_ref, y_ref, *maybe_next, with_next):
    f = jnp.dot(x_ref[...], w_ref[...], preferred_element_type=F32)
    y = res_ref[...] + (f * lax.rsqrt(jnp.mean(f * f, axis=-1, keepdims=True) + RMS_EPS)) * g_ref[...]
    y_ref[...] = y
    if with_next:
        yn = y * lax.rsqrt(jnp.mean(y * y, axis=-1, keepdims=True) + RMS_EPS)
        maybe_next[0][...] = (yn * g2_ref[...]).astype(maybe_next[0].dtype)


def _mm_res_norm(x, w, res, g, g_next=None):
    rows, k = x.shape
    d = w.shape[1]
    with_next = g_next is not None
    tm = _row_tile(rows, 512 if k * d * 2 <= (12 << 20) else 256)
    blk = pl.BlockSpec((tm, d), lambda i: (i, 0))
    row_spec = pl.BlockSpec((1, d), lambda i: (0, 0))
    out_specs = [blk, blk] if with_next else [blk]
    out_shape = [jax.ShapeDtypeStruct((rows, d), F32)]
    if with_next:
        out_shape.append(jax.ShapeDtypeStruct((rows, d), BF16))
    g2 = g_next if with_next else g
    outs = pl.pallas_call(
        functools.partial(_mm_res_norm_body, with_next=with_next),
        grid=(rows // tm,),
        in_specs=[pl.BlockSpec((tm, k), lambda i: (i, 0)),
                  pl.BlockSpec((k, d), lambda i: (0, 0), pipeline_mode=pl.Buffered(1)),
                  blk, row_spec, row_spec],
        out_specs=out_specs,
        out_shape=out_shape,
        compiler_params=_params("arbitrary"),
        name="matmul_res_norm",
    )(x, w, res, g.reshape(1, d), g2.reshape(1, d))
    return outs if with_next else outs[0]


def _up_conv_gate_body(x_ref, wg_ref, wv_ref, eg_ref, ev_ref, cwg_ref, cwv_ref, cbg_ref, cbv_ref,
                       o_ref, tg_ref, tv_ref, *rest, seq_rows, tm, taps, rounding):
    i = pl.program_id(1)
    if rounding:
        wg_bf, wv_bf, keep_g, keep_v = rest
    else:
        keep_g, keep_v = rest
        wg_bf, wv_bf = wg_ref, wv_ref

    @pl.when(i == 0)
    def _():
        if rounding:
            wg_bf[...] = wg_ref[...].astype(BF16)
            wv_bf[...] = wv_ref[...].astype(BF16)
        keep_g[...] = jnp.zeros_like(keep_g)
        keep_v[...] = jnp.zeros_like(keep_v)

    x = x_ref[...]

    def conv(w_bf, e_ref, cw_ref, cb_ref, keep, tail_ref):
        u = jnp.dot(x, w_bf[...], preferred_element_type=F32)
        cw = cw_ref[...]
        acc = cb_ref[...] + cw[taps - 1:taps] * u
        if seq_rows >= tm:
            is_start = (i * tm) % seq_rows == 0
            halo = jnp.where(is_start, pltpu.roll(e_ref[...], SUBLANES - (taps - 1), 0), keep[...])
            for back in range(1, taps):
                acc = acc + cw[taps - 1 - back:taps - back] * _shift_rows(u, halo, back)
            keep[...] = u[tm - SUBLANES:tm]
            tail_ref[...] = u[tm - SUBLANES:tm]
        else:
            t = lax.broadcasted_iota(jnp.int32, u.shape, 0) & (seq_rows - 1)
            e = e_ref[...]
            for back in range(1, taps):
                up_by = taps - 1 - back
                state = pltpu.roll(e, tm - up_by, 0) if up_by else e
                prev = jnp.where(t < back, state, pltpu.roll(u, back, 0))
                acc = acc + cw[taps - 1 - back:taps - back] * prev
            n_lane_tiles = u.shape[1] // LANES
            for c in range(n_lane_tiles):
                keep[c] = u[:, c * LANES:(c + 1) * LANES]
            for idx in range(taps - 1):
                rows_t = pl.ds(seq_rows - (taps - 1) + idx, tm // seq_rows, stride=seq_rows)
                tail_ref[idx] = jnp.concatenate([keep[c, rows_t, :] for c in range(n_lane_tiles)], axis=1)
        return acc

    gate = conv(wg_bf, eg_ref, cwg_ref, cbg_ref, keep_g, tg_ref)
    val = conv(wv_bf, ev_ref, cwv_ref, cbv_ref, keep_v, tv_ref)
    o_ref[...] = (jax.nn.gelu(gate) * val).astype(o_ref.dtype)


def _up_conv_gate(x, w_up, state_rows, conv_w, conv_b, seq_rows):
    rows, d = x.shape
    rounding = not isinstance(w_up, tuple)
    f2 = conv_w.shape[1]
    dff = f2 // 2
    taps = conv_w.shape[0]
    tm = _row_tile(rows if seq_rows == SUBLANES else seq_rows, 1024)
    tn = _col_tile(dff, 512)
    nj = dff // tn
    if seq_rows >= tm:
        assert seq_rows % tm == 0
        per_seq = lambda off: pl.BlockSpec((SUBLANES, tn), lambda j, i: ((i * tm) // seq_rows, j + off))
        st, tail = per_seq, per_seq(0)
        tail_shape = ((rows // seq_rows) * SUBLANES, dff)
        keep_shape = (SUBLANES, tn)
    else:
        assert seq_rows == SUBLANES and tm % seq_rows == 0
        st = lambda off: pl.BlockSpec((tm, tn), lambda j, i: (i, j + off))
        tail = pl.BlockSpec((taps - 1, tm // seq_rows, tn), lambda j, i: (0, i, j))
        tail_shape = (taps - 1, rows // seq_rows, dff)
        keep_shape = (tn // LANES, tm, LANES)
    wt = lambda off: pl.BlockSpec((d, tn), lambda j, i: (0, j + off))
    cw = lambda off: pl.BlockSpec((taps, tn), lambda j, i: (0, j + off))
    cb = lambda off: pl.BlockSpec((1, tn), lambda j, i: (0, j + off))
    out_specs = [pl.BlockSpec((tm, tn), lambda j, i: (i, j)), tail, tail]
    out_shape = [jax.ShapeDtypeStruct((rows, dff), BF16), jax.ShapeDtypeStruct(tail_shape, F32),
                 jax.ShapeDtypeStruct(tail_shape, F32)]
    if rounding:
        w_gate, w_val, w_specs = w_up, w_up, [wt(0), wt(nj)]
        out_specs += [wt(0), wt(0)]
        out_shape += [jax.ShapeDtypeStruct((d, dff), BF16)] * 2
    else:
        (w_gate, w_val), w_specs = w_up, [wt(0), wt(0)]
    outs = pl.pallas_call(
        functools.partial(_up_conv_gate_body, seq_rows=seq_rows, tm=tm, taps=taps, rounding=rounding),
        grid=(nj, rows // tm),
        in_specs=[pl.BlockSpec((tm, d), lambda j, i: (i, 0))] + w_specs + [st(0), st(nj),
                                                                             cw(0), cw(nj), cb(0), cb(nj)],
        out_specs=out_specs,
        out_shape=out_shape,
        scratch_shapes=[pltpu.VMEM(keep_shape, F32), pltpu.VMEM(keep_shape, F32)],
        compiler_params=_params("parallel", "arbitrary"),
        name="up_conv_gate",
    )(x, w_gate, w_val, state_rows, state_rows, conv_w, conv_w, conv_b.reshape(1, f2), conv_b.reshape(1, f2))
    act, tail_g, tail_v = outs[:3]
    w_bf = tuple(outs[3:]) if rounding else w_up
    tail = jnp.concatenate([tail_g, tail_v], axis=-1)
    if seq_rows >= tm:
        return act, tail.reshape(rows // seq_rows, SUBLANES, f2)[:, SUBLANES - (taps - 1):], w_bf
    return act, jnp.swapaxes(tail, 0, 1), w_bf


def _layer(x, shift_prev, wkv0, mem_k, mem_v, conv_prev, prm, want_v_rows):
    batch, seq_len, d = x.shape
    rows = batch * seq_len
    w = prm["width"]
    hd = prm["head_dim"]
    x2 = x.reshape(rows, d)

    xn = _rmsnorm(x2, prm["g_pre_mix"], BF16)
    n_shift = prm["mu_shift"].shape[1]
    rw = _matmul_cached(xn, prm, "w_in", 0, n_shift)
    guvq = _matmul_cached(xn, prm, "w_in", n_shift, 3 * w)
    gt = _matmul_cached(xn, prm, "w_in", n_shift + 3 * w)
    new_shift = rw.reshape(batch, seq_len, -1)[:, -1]

    r, lw, k, v, a, b, g, bonus = _rwkv_prep(rw, _pad_state_rows(shift_prev[:, None, :]), seq_len, prm)
    y, new_wkv = _wkv(r, lw, k, v, a, b, wkv0, batch, seq_len, hd)
    a_out = _rwkv_post(y, bonus, g, prm, hd)

    if seq_len % prm["sgu_chunk"] == 0:
        b_out, v_rows = _sgu(guvq, prm["w_s"], prm["sgu_bias"], prm, None, want_v_rows)
    else:
        assert seq_len == SUBLANES
        b_out, v_rows = _sgu(guvq, prm["w_s"], prm["sgu_bias"][:, :SUBLANES], prm, SUBLANES, want_v_rows)
    c_out = _xattn(guvq, mem_k, mem_v, seq_len, w)

    mix, prm["w_branch"] = _branch_mix(a_out, b_out, c_out, prm["w_branch"], gt)
    h, hn = _mm_res_norm(mix, prm["w_out"], x2, prm["g_post_mix"], prm["g_pre_ffn"])

    act, conv_new, prm["w_up"] = _up_conv_gate(hn, prm["w_up"], _pad_state_rows(conv_prev), prm["conv_w"],
                                               prm["conv_b"], seq_len)
    y_out = _mm_res_norm(act, prm["w_down"], h, prm["g_post_ffn"])
    if want_v_rows:
        v_rows = v_rows.reshape(batch, seq_len, w)
    return y_out.reshape(batch, seq_len, d), new_shift, new_wkv, v_rows, conv_new


def _prepare(l, g_pre_mix, w_in, mu_shift, w0, w_decay, a0, w_aaa, w_gate, k_k, k_a, r_k, lnx_w, lnx_b,
             sgu_g, sgu_b, w_s, b_s, w_branch, w_out, g_post_mix, g_pre_ffn, w_up, conv_w, conv_b, w_down,
             g_post_ffn):
    heads, hd = r_k.shape[1], r_k.shape[2]
    w = heads * hd
    d = w_in.shape[1]
    n_shift = mu_shift.shape[1]
    rank_d, rank_a, rank_g = w_decay.shape[1], w_aaa.shape[1], w_gate.shape[1]
    assert rank_d + rank_a == LANES and rank_g == LANES and n_shift == 3 * w + 2 * LANES
    groups, sgu_chunk, _ = w_s.shape[1:]
    row = lambda t: t[l].reshape(1, -1)
    lane_head = jnp.arange(LANES) // hd
    gd = w // groups
    return dict(
        width=w, head_dim=hd, sgu_chunk=sgu_chunk,
        g_pre_mix=g_pre_mix[l],
        w_in=w_in[l],
        mu_shift=row(mu_shift), w0=row(w0), a0=row(a0),
        wd_pad=jnp.pad(w_decay[l], ((0, rank_a), (0, 0))),
        wa_pad=jnp.pad(w_aaa[l], ((rank_d, 0), (0, 0))),
        w_gate=w_gate[l],
        k_k=row(k_k), k_a=row(k_a), r_k=row(r_k), lnx_w=row(lnx_w), lnx_b=row(lnx_b),
        ones2=(lane_head[:, None] == lane_head[None, :]).astype(BF16),
        sgu_g=row(sgu_g), sgu_b=row(sgu_b),
        w_s=w_s[l],
        sgu_bias=jnp.broadcast_to(b_s[l][:, :, None], (groups, sgu_chunk, gd)),
        w_branch=w_branch[l], w_out=w_out[l].astype(BF16),
        g_post_mix=g_post_mix[l], g_pre_ffn=g_pre_ffn[l],
        w_up=w_up[l], conv_w=conv_w[l], conv_b=conv_b[l], w_down=w_down[l].astype(BF16),
        g_post_ffn=g_post_ffn[l],
    )


def kernel(x_prompt, x_sample, mem_prompt, state_wkv, state_shift, cache_mem_k, cache_mem_v, state_ffn_conv, g_pre_mix, w_in, mu_shift, w0, w_decay, a0, w_aaa, w_gate, k_k, k_a, r_k, lnx_w, lnx_b, sgu_g, sgu_b, w_s, b_s, g_mem, w_mem_k, w_mem_v, w_branch, w_out, g_post_mix, g_pre_ffn, w_up, conv_w, conv_b, w_down, g_post_ffn):
    depth = w_in.shape[0]
    batch = x_prompt.shape[0]
    mem_len, d = mem_prompt.shape[1], mem_prompt.shape[2]
    xh, xhd = cache_mem_k.shape[3], cache_mem_k.shape[4]
    heads, hd = r_k.shape[1], r_k.shape[2]
    n_shift = mu_shift.shape[1]
    f2 = w_up.shape[2]
    taps = conv_w.shape[1]
    y_p, y_s = x_prompt, x_sample
    outs = [[] for _ in range(9)]
    for l in range(depth):
        prm = _prepare(l, g_pre_mix, w_in, mu_shift, w0, w_decay, a0, w_aaa, w_gate, k_k, k_a, r_k, lnx_w, lnx_b,
                       sgu_g, sgu_b, w_s, b_s, w_branch, w_out, g_post_mix, g_pre_ffn, w_up, conv_w, conv_b,
                       w_down, g_post_ffn)
        mn = _rmsnorm(mem_prompt.reshape(batch * mem_len, d), g_mem[l], BF16)
        mk_p = _matmul(mn, w_mem_k[l])[0].reshape(batch, mem_len, xh, xhd)
        mv_p = _matmul(mn, w_mem_v[l])[0].reshape(batch, mem_len, xh, xhd)
        y_p, sh_p, wkv_p, _, cv_p = _layer(
            y_p, jnp.zeros((batch, n_shift), F32), jnp.zeros((batch, heads, hd, hd), F32), mk_p, mv_p,
            jnp.zeros((batch, taps - 1, f2), F32), prm, False)
        y_s, sh_s, wkv_s, vr_s, cv_s = _layer(y_s, state_shift[l], state_wkv[l], cache_mem_k[l], cache_mem_v[l],
                                              state_ffn_conv[l], prm, True)
        for lst, val in zip(outs, (wkv_p, sh_p, mk_p, mv_p, cv_p, wkv_s, sh_s, vr_s, cv_s)):
            lst.append(val)
    return (y_p, y_s) + tuple(jnp.stack(lst) for lst in outs)
```

```python
import functools

import jax
import jax.numpy as jnp
from jax import lax
from jax.experimental import pallas as pl
from jax.experimental.pallas import tpu as pltpu

F32 = jnp.float32
BF16 = jnp.bfloat16
HIGHEST = lax.Precision.HIGHEST

LANES = 128
SUBLANES = 8
VMEM_LIMIT_BYTES = 56 * 1024 * 1024

RMS_EPS = 1e-6
LN_EPS = 1e-5
GN_EPS = 64e-5
KK_EPS = 1e-12
RWKV_CHUNK = 64


def _params(*semantics):
    return pltpu.CompilerParams(dimension_semantics=semantics, vmem_limit_bytes=VMEM_LIMIT_BYTES)


def _row_tile(rows, pref):
    t = min(rows, pref)
    while rows % t:
        t -= SUBLANES
    return t


def _col_tile(n, cap):
    best = n
    for t in range(LANES, min(n, cap) + 1, LANES):
        if n % t == 0:
            best = t
    return best


def _bdot(a, b):
    return jnp.dot(a.astype(BF16), b.astype(BF16), preferred_element_type=F32)


def _bdot_nt(a, b):
    return lax.dot_general(a.astype(BF16), b.astype(BF16), (((1,), (1,)), ((), ())),
                           preferred_element_type=F32)


def _fdot(a, b):
    return jnp.dot(a, b, precision=HIGHEST, preferred_element_type=F32)


def _dot3(a, b):
    a_hi = a.astype(BF16)
    b_hi = b.astype(BF16)
    a_lo = (a - a_hi.astype(F32)).astype(BF16)
    b_lo = (b - b_hi.astype(F32)).astype(BF16)
    dot = lambda x, y: jnp.dot(x, y, preferred_element_type=F32)
    return dot(a_hi, b_hi) + (dot(a_hi, b_lo) + dot(a_lo, b_hi))


def _rmsnorm_body(x_ref, g_ref, o_ref):
    x = x_ref[...]
    y = x * lax.rsqrt(jnp.mean(x * x, axis=-1, keepdims=True) + RMS_EPS)
    o_ref[...] = (y * g_ref[...]).astype(o_ref.dtype)


def _rmsnorm(x, g, out_dtype):
    rows, d = x.shape
    tm = _row_tile(rows, 512)
    return pl.pallas_call(
        _rmsnorm_body,
        grid=(rows // tm,),
        in_specs=[pl.BlockSpec((tm, d), lambda i: (i, 0)), pl.BlockSpec((1, d), lambda i: (0, 0))],
        out_specs=pl.BlockSpec((tm, d), lambda i: (i, 0)),
        out_shape=jax.ShapeDtypeStruct((rows, d), out_dtype),
        compiler_params=_params("parallel"),
        name="rmsnorm",
    )(x, g.reshape(1, d))


def _matmul_round_body(x_ref, w_ref, o_ref, wb_ref):
    @pl.when(pl.program_id(1) == 0)
    def _():
        wb_ref[...] = w_ref[...].astype(BF16)

    o_ref[...] = jnp.dot(x_ref[...], wb_ref[...], preferred_element_type=F32).astype(o_ref.dtype)


def _matmul_body(x_ref, w_ref, o_ref):
    o_ref[...] = jnp.dot(x_ref[...], w_ref[...], preferred_element_type=F32).astype(o_ref.dtype)


def _matmul(x, w, col_start=0, n_cols=None, out_dtype=F32):
    rows, k = x.shape
    n = w.shape[1] - col_start if n_cols is None else n_cols
    assert col_start % LANES == 0 and n % LANES == 0
    rounding = w.dtype != BF16
    tn = _col_tile(n, 1792)
    budget = VMEM_LIMIT_BYTES - (4 << 20)
    tm = _row_tile(rows, 1024)
    w_bytes = 2 * k * tn * (4 + 2) if rounding else 2 * k * tn * 2
    while w_bytes + 2 * tm * k * 2 + 2 * tm * tn * 4 > budget and tm % (2 * SUBLANES) == 0:
        tm //= 2
    w_spec = pl.BlockSpec((pl.Element(k), pl.Element(tn)),
                          lambda j, i: (0, pl.multiple_of(col_start + j * tn, LANES)))
    out_spec = pl.BlockSpec((tm, tn), lambda j, i: (i, j))
    out_shape = jax.ShapeDtypeStruct((rows, n), out_dtype)
    if rounding:
        out_spec = [out_spec, pl.BlockSpec((k, tn), lambda j, i: (0, j))]
        out_shape = [out_shape, jax.ShapeDtypeStruct((k, n), BF16)]
    return pl.pallas_call(
        _matmul_round_body if rounding else _matmul_body,
        grid=(n // tn, rows // tm),
        in_specs=[pl.BlockSpec((tm, k), lambda j, i: (i, 0)), w_spec],
        out_specs=out_spec,
        out_shape=out_shape,
        compiler_params=_params("parallel", "arbitrary"),
        name="matmul",
    )(x, w)


def _matmul_cached(x, prm, name, col_start=0, n_cols=None):
    key = (name, col_start)
    if key in prm:
        return _matmul(x, prm[key])
    out, prm[key] = _matmul(x, prm[name], col_start, n_cols)
    return out


def _shift_rows(x, halo, k):
    rolled = pltpu.roll(x, k, 0)
    row = lax.broadcasted_iota(jnp.int32, (SUBLANES, x.shape[1]), 0)
    top = rolled[0:SUBLANES]
    for j in range(k):
        top = jnp.where(row == j, halo[SUBLANES - k + j:SUBLANES - k + j + 1, :], top)
    return jnp.concatenate([top, rolled[SUBLANES:]], axis=0)


def _head_sum(x, ones_ref):
    ones = ones_ref[...]
    hi = x.astype(BF16)
    lo = (x - hi.astype(F32)).astype(BF16)
    outs = []
    for j in range(x.shape[1] // LANES):
        sl = slice(j * LANES, (j + 1) * LANES)
        outs.append(jnp.dot(hi[:, sl], ones, preferred_element_type=F32)
                    + jnp.dot(lo[:, sl], ones, preferred_element_type=F32))
    return jnp.concatenate(outs, axis=1)


def _pad_state_rows(state):
    b, n, f = state.shape
    return jnp.pad(state, ((0, 0), (0, SUBLANES - n), (0, 0))).reshape(b * SUBLANES, f)


def _rwkv_prep_body(rw_ref, halo_ref, sp_ref, mu_ref, w0_ref, wd_ref, a0_ref, wa_ref, wg_ref,
                    kk_ref, ka_ref, rk_ref, ones_ref,
                    r_o, lw_o, k_o, v_o, a_o, b_o, g_o, bonus_o, *, seq_rows, tm, width):
    x = rw_ref[...]
    if seq_rows >= tm:
        is_start = (pl.program_id(0) * tm) % seq_rows == 0
        halo = jnp.where(is_start, pltpu.roll(sp_ref[...], SUBLANES - 1, 0), halo_ref[...])
        prev = _shift_rows(x, halo, 1)
    else:
        t = lax.broadcasted_iota(jnp.int32, x.shape, 0) & (seq_rows - 1)
        prev = jnp.where(t == 0, sp_ref[...], pltpu.roll(x, 1, 0))
    xs = x + mu_ref[...] * (prev - x)

    w = width
    r = xs[:, 0:w]
    k = xs[:, w:2 * w]
    v = xs[:, 2 * w:3 * w]
    x_wa = xs[:, 3 * w:3 * w + LANES]
    x_g = xs[:, 3 * w + LANES:3 * w + 2 * LANES]

    z = -(w0_ref[...] + _dot3(jnp.tanh(x_wa), wd_ref[...]))
    softplus = jnp.maximum(z, 0.0) + jnp.log(1.0 + jnp.exp(-jnp.abs(z)))
    lw = -jnp.exp(-softplus - 0.5)
    a_lr = jax.nn.sigmoid(a0_ref[...] + _bdot(x_wa, wa_ref[...]))
    g = _bdot(jax.nn.sigmoid(x_g), wg_ref[...])

    kk = k * kk_ref[...]
    k_mod = k * (1.0 + (a_lr - 1.0) * ka_ref[...])
    kk = kk / jnp.maximum(jnp.sqrt(_head_sum(kk * kk, ones_ref)), KK_EPS)
    bonus = _head_sum(r * k_mod * rk_ref[...], ones_ref) * v

    r_o[...] = r
    lw_o[...] = lw
    k_o[...] = k_mod
    v_o[...] = v.astype(v_o.dtype)
    a_o[...] = -kk
    b_o[...] = kk * a_lr
    g_o[...] = g
    bonus_o[...] = bonus


def _rwkv_prep(rw, shift_rows, seq_rows, prm):
    rows, n_shift = rw.shape
    w = prm["width"]
    tm = _row_tile(rows, 256)
    if seq_rows >= tm:
        assert seq_rows % tm == 0
        sp_spec = pl.BlockSpec((SUBLANES, n_shift), lambda i: ((i * tm) // seq_rows, 0))
    else:
        assert seq_rows == SUBLANES and tm % seq_rows == 0
        sp_spec = pl.BlockSpec((tm, n_shift), lambda i: (i, 0))
    row_spec = lambda c: pl.BlockSpec((1, c), lambda i: (0, 0))
    full = lambda a: pl.BlockSpec(a.shape, lambda i: (0, 0))
    out_spec = pl.BlockSpec((tm, w), lambda i: (i, 0))
    return pl.pallas_call(
        functools.partial(_rwkv_prep_body, seq_rows=seq_rows, tm=tm, width=w),
        grid=(rows // tm,),
        in_specs=[pl.BlockSpec((tm, n_shift), lambda i: (i, 0)),
                  pl.BlockSpec((SUBLANES, n_shift), lambda i: (jnp.maximum(i * (tm // SUBLANES) - 1, 0), 0)),
                  sp_spec, row_spec(n_shift), row_spec(w), full(prm["wd_pad"]), row_spec(w), full(prm["wa_pad"]),
                  full(prm["w_gate"]), row_spec(w), row_spec(w), row_spec(w), full(prm["ones2"])],
        out_specs=[out_spec] * 8,
        out_shape=[jax.ShapeDtypeStruct((rows, w), BF16 if i == 3 else F32) for i in range(8)],
        compiler_params=_params("parallel"),
        name="rwkv_prep",
    )(rw, rw, shift_rows, prm["mu_shift"], prm["w0"], prm["wd_pad"], prm["a0"], prm["wa_pad"], prm["w_gate"],
      prm["k_k"], prm["k_a"], prm["r_k"], prm["ones2"])


def _wkv_body(r_ref, lw_ref, k_ref, v_ref, a_ref, b_ref, h0_ref, y_ref, hout_ref, h_scr,
              *, chunk, pairs, seqs, width, hd):
    c = pl.program_id(1)
    n_pairs = width // LANES

    @pl.when(c == 0)
    def _():
        zero = jnp.zeros((hd, hd), F32)
        for i in range(seqs):
            for p in range(n_pairs):
                h_scr[i, p] = jnp.concatenate(
                    [jnp.concatenate([h0_ref[i, 2 * p], zero], axis=1),
                     jnp.concatenate([zero, h0_ref[i, 2 * p + 1]], axis=1)], axis=0)

    cs = chunk
    lanes = LANES * pairs
    heads = 2 * pairs
    n = heads * cs
    log_c = cs.bit_length() - 1
    log_hd = hd.bit_length() - 1
    n_stacks = width // lanes

    tri = (lax.broadcasted_iota(jnp.int32, (cs, cs), 0) >= lax.broadcasted_iota(jnp.int32, (cs, cs), 1)).astype(F32)
    row = lax.broadcasted_iota(jnp.int32, (n, lanes), 0)
    lane = lax.broadcasted_iota(jnp.int32, (n, lanes), 1)
    head_mask = (row >> log_c) == (lane >> log_hd)
    ri = lax.broadcasted_iota(jnp.int32, (n, n), 0)
    ci = lax.broadcasted_iota(jnp.int32, (n, n), 1)
    same_head = (ri >> log_c) == (ci >> log_c)
    rt = ri & (cs - 1)
    ct = ci & (cs - 1)
    strict = same_head & (ct < rt)
    incl = same_head & (ct <= rt)
    eye = (ri == ci).astype(F32)
    diag = (lax.broadcasted_iota(jnp.int32, (LANES, LANES), 0)
            == lax.broadcasted_iota(jnp.int32, (LANES, LANES), 1))

    def stack(x):
        return jnp.where(head_mask, jnp.concatenate([x] * heads, axis=0), 0.0)

    def fold(x):
        out = x[0:cs]
        for hidx in range(1, heads):
            out = out + x[hidx * cs:(hidx + 1) * cs]
        return out

    scaled = []
    for i in range(seqs):
        rs = slice(i * cs, (i + 1) * cs)
        lw = lw_ref[rs, :]
        cum = _fdot(tri, lw)
        total = cum[cs - 1:cs]
        e_cum = jnp.exp(cum)
        e_neg = jnp.exp(-cum)
        e_rest = jnp.exp(total - cum)
        b = b_ref[rs, :]
        k = k_ref[rs, :]
        scaled.append(dict(a=a_ref[rs, :] * jnp.exp(cum - lw), r=r_ref[rs, :] * e_cum, b=b * e_neg, k=k * e_neg,
                           v=v_ref[rs, :].astype(F32), bh=b * e_rest, kh=k * e_rest, total=total))

    inst = [(i, s) for i in range(seqs) for s in range(n_stacks)]
    jj = range(len(inst))

    def stacked(name):
        return [stack(scaled[i][name][:, s * lanes:(s + 1) * lanes]) for i, s in inst]

    a_st, r_st, b_st, k_st, v_st, bh_st, kh_st = (stacked(nm) for nm in ("a", "r", "b", "k", "v", "bh", "kh"))

    gram = [_bdot_nt(jnp.concatenate([a_st[j], r_st[j]], axis=0), jnp.concatenate([b_st[j], k_st[j]], axis=0))
            for j in jj]
    l_ab = [jnp.where(strict, gram[j][:n, :n], 0.0) for j in jj]
    m_rb = [jnp.where(incl, gram[j][n:, :n], 0.0) for j in jj]
    l_ak_m_rk = [jnp.concatenate([jnp.where(strict, gram[j][:n, n:], 0.0), jnp.where(incl, gram[j][n:, n:], 0.0)],
                                 axis=0) for j in jj]

    t_inv = [eye + l_ab[j] for j in jj]
    x_pow = [_bdot(l_ab[j], l_ab[j]) for j in jj]
    for _ in range(log_c - 2):
        both = [_bdot(jnp.concatenate([x_pow[j], t_inv[j]], axis=0), x_pow[j]) for j in jj]
        t_inv = [t_inv[j] + both[j][n:] for j in jj]
        x_pow = [both[j][:n] for j in jj]
    t_inv = [t_inv[j] + _bdot(t_inv[j], x_pow[j]) for j in jj]

    lv = [_bdot(l_ak_m_rk[j], v_st[j]) for j in jj]
    aw = [_bdot(t_inv[j], jnp.concatenate([a_st[j], lv[j][:n]], axis=1)) for j in jj]
    rb = [_bdot(m_rb[j], aw[j]) for j in jj]
    r_bar = [fold(r_st[j] + rb[j][:, :lanes]) for j in jj]
    y0 = [fold(rb[j][:, lanes:] + lv[j][n:]) for j in jj]

    jp = [(j, p) for j in jj for p in range(pairs)]

    def pair_lanes(x, p, off=0):
        return x[:, off + p * LANES:off + (p + 1) * LANES]

    h = [h_scr[inst[j][0], inst[j][1] * pairs + p] for j, p in jp]
    yv = [_bdot(pair_lanes(r_bar[j], p), h[q]) + pair_lanes(y0[j], p) for q, (j, p) in enumerate(jp)]
    for q, (j, p) in enumerate(jp):
        i, s = inst[j]
        y_ref[i * cs:(i + 1) * cs, s * lanes + p * LANES:s * lanes + (p + 1) * LANES] = yv[q]
    bh_t = [pair_lanes(bh_st[j], p).T for j, p in jp]
    kh_t = [pair_lanes(kh_st[j], p).T for j, p in jp]
    pp = [_bdot(bh_t[q], jnp.concatenate([pair_lanes(aw[j], p), pair_lanes(aw[j], p, lanes)], axis=1))
          for q, (j, p) in enumerate(jp)]
    kv = [_bdot(kh_t[q], pair_lanes(v_st[j], p)) for q, (j, p) in enumerate(jp)]
    for q, (j, p) in enumerate(jp):
        i, s = inst[j]
        decay = pair_lanes(scaled[i]["total"], p, s * lanes)
        phi = jnp.where(diag, jnp.exp(decay), 0.0) + pp[q][:, :LANES]
        h_scr[i, s * pairs + p] = _bdot(phi, h[q]) + (pp[q][:, LANES:] + kv[q])

    @pl.when(c == pl.num_programs(1) - 1)
    def _():
        for i in range(seqs):
            for p in range(n_pairs):
                both = h_scr[i, p]
                hout_ref[i, 2 * p] = both[:hd, :hd]
                hout_ref[i, 2 * p + 1] = both[hd:, hd:]


def _wkv(r, lw, k, v, a, b, h0, batch, seq_len, hd):
    rows, w = r.shape
    chunk = min(seq_len, RWKV_CHUNK)
    assert seq_len % chunk == 0 and chunk % SUBLANES == 0 and LANES % chunk == 0
    pairs = LANES // (2 * chunk)
    n_chunks = seq_len // chunk
    n_pairs = w // LANES
    seqs = 2 if (n_chunks == 1 and batch % 2 == 0) else 1
    vec = pl.BlockSpec((seqs * chunk, w), lambda bi, ci: (bi * n_chunks + ci, 0))
    st = pl.BlockSpec((seqs, w // hd, hd, hd), lambda bi, ci: (bi, 0, 0, 0))
    return pl.pallas_call(
        functools.partial(_wkv_body, chunk=chunk, pairs=pairs, seqs=seqs, width=w, hd=hd),
        grid=(batch // seqs, n_chunks),
        in_specs=[vec] * 6 + [st],
        out_specs=[vec, st],
        out_shape=[jax.ShapeDtypeStruct((rows, w), F32),
                   jax.ShapeDtypeStruct((batch, w // hd, hd, hd), F32)],
        scratch_shapes=[pltpu.VMEM((seqs, n_pairs, LANES, LANES), F32)],
        compiler_params=_params("parallel", "arbitrary"),
        name="rwkv_chunks",
    )(r, lw, k, v, a, b, h0)


def _rwkv_post_body(y_ref, bonus_ref, g_ref, lw_ref, lb_ref, ones_ref, o_ref, *, hd):
    y = y_ref[...]
    mu = _head_sum(y, ones_ref) * (1.0 / hd)
    d = y - mu
    var = _head_sum(d * d, ones_ref) * (1.0 / hd)
    yn = d * lax.rsqrt(var + GN_EPS) * lw_ref[...] + lb_ref[...]
    o_ref[...] = ((yn + bonus_ref[...]) * g_ref[...]).astype(o_ref.dtype)


def _rwkv_post(y, bonus, g, prm, hd):
    rows, w = y.shape
    tm = _row_tile(rows, 512)
    blk = pl.BlockSpec((tm, w), lambda i: (i, 0))
    row_spec = pl.BlockSpec((1, w), lambda i: (0, 0))
    return pl.pallas_call(
        functools.partial(_rwkv_post_body, hd=hd),
        grid=(rows // tm,),
        in_specs=[blk, blk, blk, row_spec, row_spec, pl.BlockSpec((LANES, LANES), lambda i: (0, 0))],
        out_specs=blk,
        out_shape=jax.ShapeDtypeStruct((rows, w), BF16),
        compiler_params=_params("parallel"),
        name="rwkv_post",
    )(y, bonus, g, prm["lnx_w"], prm["lnx_b"], prm["ones2"])


def _sgu_body(gu_ref, gv_ref, sg_ref, sb_ref, wm_ref, bias_ref, o_ref, *maybe_v, groups, seq_rows):
    u = jax.nn.gelu(gu_ref[...])
    vf = jax.nn.gelu(gv_ref[...])
    mu = jnp.mean(vf, axis=-1, keepdims=True)
    d = vf - mu
    var = jnp.mean(d * d, axis=-1, keepdims=True)
    v = (d * lax.rsqrt(var + LN_EPS)) * sg_ref[...] + sb_ref[...]
    if maybe_v:
        maybe_v[0][...] = v
    rows = v.shape[0]
    gd = v.shape[1] // groups
    ri = lax.broadcasted_iota(jnp.int32, (rows, rows), 0)
    ci = lax.broadcasted_iota(jnp.int32, (rows, rows), 1)
    causal = ri >= ci
    if seq_rows is not None:
        shift = seq_rows.bit_length() - 1
        causal = causal & ((ri >> shift) == (ci >> shift))
        pos = ((lax.broadcasted_iota(jnp.int32, (rows, seq_rows), 0) & (seq_rows - 1))
               == lax.broadcasted_iota(jnp.int32, (rows, seq_rows), 1)).astype(F32)
    for g in range(groups):
        sl = slice(g * gd, (g + 1) * gd)
        if seq_rows is None:
            w_full, bias = wm_ref[g], bias_ref[g]
        else:
            w_full = lax.dot_general(_fdot(pos, wm_ref[g][:seq_rows, :seq_rows]), pos, (((1,), (1,)), ((), ())),
                                     precision=HIGHEST, preferred_element_type=F32)
            bias = _fdot(pos, bias_ref[g])
        s = _bdot(jnp.where(causal, w_full, 0.0), v[:, sl]) + bias
        o_ref[:, sl] = (u[:, sl] * s).astype(o_ref.dtype)


def _sgu(guvq, w_mix, bias, prm, seq_rows, want_v_rows):
    rows = guvq.shape[0]
    w = prm["width"]
    groups, chunk, _ = w_mix.shape
    cs = chunk if seq_rows is None else _row_tile(rows, 2 * LANES)
    row_spec = pl.BlockSpec((1, w), lambda i: (0, 0))
    out = pl.BlockSpec((cs, w), lambda i: (i, 0))
    outs = pl.pallas_call(
        functools.partial(_sgu_body, groups=groups, seq_rows=seq_rows),
        grid=(rows // cs,),
        in_specs=[pl.BlockSpec((cs, w), lambda i: (i, 0)), pl.BlockSpec((cs, w), lambda i: (i, 1)),
                  row_spec, row_spec,
                  pl.BlockSpec(w_mix.shape, lambda i: (0, 0, 0)), pl.BlockSpec(bias.shape, lambda i: (0, 0, 0))],
        out_specs=[out, out] if want_v_rows else [out],
        out_shape=[jax.ShapeDtypeStruct((rows, w), BF16)] + ([jax.ShapeDtypeStruct((rows, w), F32)]
                                                           if want_v_rows else []),
        compiler_params=_params("parallel"),
        name="sgu",
    )(guvq, guvq, prm["sgu_g"], prm["sgu_b"], w_mix, bias)
    return (outs[0], outs[1]) if want_v_rows else (outs[0], None)


def _xattn_body(q_ref, k_ref, v_ref, o_ref, *, heads, seqs, tq, m):
    hd = q_ref.shape[1] // heads
    scale = hd ** -0.5
    cases = [(h, i) for h in range(heads) for i in range(seqs)]
    cols = lambda h: slice(h * hd, (h + 1) * hd)
    mem = lambda i: slice(i * m, (i + 1) * m)
    s = [_bdot_nt(q_ref[i * tq:(i + 1) * tq, cols(h)], k_ref[mem(i), cols(h)]) * scale for h, i in cases]
    e = [jnp.exp(x - jnp.max(x, axis=-1, keepdims=True)) for x in s]
    p = [x / jnp.sum(x, axis=-1, keepdims=True) for x in e]
    o = [_bdot(p[c], v_ref[mem(i), cols(h)]) for c, (h, i) in enumerate(cases)]
    for h in range(heads):
        o_ref[:, cols(h)] = jnp.concatenate(o[h * seqs:(h + 1) * seqs], axis=0).astype(o_ref.dtype)


def _xattn(guvq, mem_k, mem_v, seq_len, w):
    batch, m, heads, hd = mem_k.shape
    rows = guvq.shape[0]
    tq = _row_tile(seq_len, 512)
    nq = seq_len // tq
    seqs = max(1, 32 // tq)
    assert batch % seqs == 0 and (seqs == 1 or nq == 1)
    kv = pl.BlockSpec((seqs * m, w), lambda bi, qi: (bi, 0))
    qo = lambda col: pl.BlockSpec((seqs * tq, w), lambda bi, qi: (bi * nq + qi, col))
    return pl.pallas_call(
        functools.partial(_xattn_body, heads=heads, seqs=seqs, tq=tq, m=m),
        grid=(batch // seqs, nq),
        in_specs=[qo(2), kv, kv],
        out_specs=qo(0),
        out_shape=jax.ShapeDtypeStruct((rows, w), BF16),
        compiler_params=_params("parallel", "parallel"),
        name="mem_xattn",
    )(guvq, mem_k.reshape(batch * m, w), mem_v.reshape(batch * m, w))


def _gate_branch_body(*refs, rounding):
    if rounding:
        x_ref, a_ref, b_ref, c_ref, wg0_ref, wg1_ref, wg2_ref, wb_ref, o_ref, wg_bf, wb_bf = refs

        @pl.when(pl.program_id(1) == 0)
        def _():
            for n, src in enumerate((wg0_ref, wg1_ref, wg2_ref)):
                wg_bf[n] = src[...].astype(BF16)
            wb_bf[...] = wb_ref[...].astype(BF16)
    else:
        x_ref, a_ref, b_ref, c_ref, wg_bf, wb_bf, o_ref = refs

    x = x_ref[...]
    acc = None
    for n, br_ref in enumerate((a_ref, b_ref, c_ref)):
        gate = jax.nn.sigmoid(jnp.dot(x, wg_bf[n], preferred_element_type=F32))
        term = gate * jnp.dot(br_ref[...], wb_bf[n], preferred_element_type=F32)
        acc = term if acc is None else acc + term
    o_ref[...] = acc.astype(o_ref.dtype)


def _gate_branch_mix(xn, a_out, b_out, c_out, w_gates, gate_col0, w_branch):
    rows, k = xn.shape
    w = a_out.shape[1]
    nb, _, d = w_branch.shape
    assert nb == 3
    rounding = w_branch.dtype != BF16
    tm = _row_tile(rows, 1024)
    tn = _col_tile(d, 256)
    nj = d // tn
    act = lambda width: pl.BlockSpec((tm, width), lambda j, i: (i, 0))
    wb_spec = pl.BlockSpec((nb, w, tn), lambda j, i: (0, 0, j))
    wg_spec = pl.BlockSpec((nb, k, tn), lambda j, i: (0, 0, j))
    out_specs = [pl.BlockSpec((tm, tn), lambda j, i: (i, j))]
    out_shape = [jax.ShapeDtypeStruct((rows, d), BF16)]
    if rounding:
        gate_cols = lambda n: pl.BlockSpec(
            (pl.Element(k), pl.Element(tn)),
            lambda j, i: (0, pl.multiple_of(gate_col0 + n * d + j * tn, LANES)))
        w_specs = [gate_cols(0), gate_cols(1), gate_cols(2), wb_spec]
        w_args = (w_gates, w_gates, w_gates, w_branch)
        out_specs += [wg_spec, wb_spec]
        out_shape += [jax.ShapeDtypeStruct((nb, k, d), BF16), jax.ShapeDtypeStruct(w_branch.shape, BF16)]
    else:
        w_specs = [wg_spec, wb_spec]
        w_args = (w_gates, w_branch)
    outs = pl.pallas_call(
        functools.partial(_gate_branch_body, rounding=rounding),
        grid=(nj, rows // tm),
        in_specs=[act(k), act(w), act(w), act(w)] + w_specs,
        out_specs=out_specs,
        out_shape=out_shape,
        compiler_params=_params("parallel", "arbitrary"),
        name="gate_branch_mix",
    )(xn, a_out, b_out, c_out, *w_args)
    return outs[0], ((outs[1], outs[2]) if rounding else (w_gates, w_branch))


def _mm_res_norm_body(x_ref, w_ref, res_ref, g_ref, g2_ref, y_ref, *maybe_next, with_next):
    f = jnp.dot(x_ref[...], w_ref[...], preferred_element_type=F32)
    y = res_ref[...] + (f * lax.rsqrt(jnp.mean(f * f, axis=-1, keepdims=True) + RMS_EPS)) * g_ref[...]
    y_ref[...] = y
    if with_next:
        yn = y * lax.rsqrt(jnp.mean(y * y, axis=-1, keepdims=True) + RMS_EPS)
        maybe_next[0][...] = (yn * g2_ref[...]).astype(maybe_next[0].dtype)


def _mm_res_norm(x, w, res, g, g_next=None):
    rows, k = x.shape
    d = w.shape[1]
    with_next = g_next is not None
    tm = _row_tile(rows, 512 if k * d * 2 <= (12 << 20) else 256)
    blk = pl.BlockSpec((tm, d), lambda i: (i, 0))
    row_spec = pl.BlockSpec((1, d), lambda i: (0, 0))
    out_specs = [blk, blk] if with_next else [blk]
    out_shape = [jax.ShapeDtypeStruct((rows, d), F32)]
    if with_next:
        out_shape.append(jax.ShapeDtypeStruct((rows, d), BF16))
    g2 = g_next if with_next else g
    outs = pl.pallas_call(
        functools.partial(_mm_res_norm_body, with_next=with_next),
        grid=(rows // tm,),
        in_specs=[pl.BlockSpec((tm, k), lambda i: (i, 0)),
                  pl.BlockSpec((k, d), lambda i: (0, 0), pipeline_mode=pl.Buffered(1)),
                  blk, row_spec, row_spec],
        out_specs=out_specs,
        out_shape=out_shape,
        compiler_params=_params("arbitrary"),
        name="matmul_res_norm",
    )(x, w, res, g.reshape(1, d), g2.reshape(1, d))
    return outs if with_next else outs[0]


def _up_conv_gate_body(x_ref, wg_ref, wv_ref, eg_ref, ev_ref, cwg_ref, cwv_ref, cbg_ref, cbv_ref,
                       o_ref, tg_ref, tv_ref, *rest, seq_rows, tm, taps, rounding):
    i = pl.program_id(1)
    if rounding:
        wg_bf, wv_bf, keep_g, keep_v = rest
    else:
        keep_g, keep_v = rest
        wg_bf, wv_bf = wg_ref, wv_ref

    @pl.when(i == 0)
    def _():
        if rounding:
            wg_bf[...] = wg_ref[...].astype(BF16)
            wv_bf[...] = wv_ref[...].astype(BF16)
        keep_g[...] = jnp.zeros_like(keep_g)
        keep_v[...] = jnp.zeros_like(keep_v)

    x = x_ref[...]

    def conv(w_bf, e_ref, cw_ref, cb_ref, keep, tail_ref):
        u = jnp.dot(x, w_bf[...], preferred_element_type=F32)
        cw = cw_ref[...]
        acc = cb_ref[...] + cw[taps - 1:taps] * u
        if seq_rows >= tm:
            is_start = (i * tm) % seq_rows == 0
            halo = jnp.where(is_start, pltpu.roll(e_ref[...], SUBLANES - (taps - 1), 0), keep[...])
            for back in range(1, taps):
                acc = acc + cw[taps - 1 - back:taps - back] * _shift_rows(u, halo, back)
            keep[...] = u[tm - SUBLANES:tm]
            tail_ref[...] = u[tm - SUBLANES:tm]
        else:
            t = lax.broadcasted_iota(jnp.int32, u.shape, 0) & (seq_rows - 1)
            e = e_ref[...]
            for back in range(1, taps):
                up_by = taps - 1 - back
                state = pltpu.roll(e, tm - up_by, 0) if up_by else e
                prev = jnp.where(t < back, state, pltpu.roll(u, back, 0))
                acc = acc + cw[taps - 1 - back:taps - back] * prev
            n_lane_tiles = u.shape[1] // LANES
            for c in range(n_lane_tiles):
                keep[c] = u[:, c * LANES:(c + 1) * LANES]
            for idx in range(taps - 1):
                rows_t = pl.ds(seq_rows - (taps - 1) + idx, tm // seq_rows, stride=seq_rows)
                tail_ref[idx] = jnp.concatenate([keep[c, rows_t, :] for c in range(n_lane_tiles)], axis=1)
        return acc

    gate = conv(wg_bf, eg_ref, cwg_ref, cbg_ref, keep_g, tg_ref)
    val = conv(wv_bf, ev_ref, cwv_ref, cbv_ref, keep_v, tv_ref)
    o_ref[...] = (jax.nn.gelu(gate) * val).astype(o_ref.dtype)


def _up_conv_gate(x, w_up, state_rows, conv_w, conv_b, seq_rows):
    rows, d = x.shape
    rounding = not isinstance(w_up, tuple)
    f2 = conv_w.shape[1]
    dff = f2 // 2
    taps = conv_w.shape[0]
    tm = _row_tile(rows if seq_rows == SUBLANES else seq_rows, 1024)
    tn = _col_tile(dff, 512)
    nj = dff // tn
    if seq_rows >= tm:
        assert seq_rows % tm == 0
        per_seq = lambda off: pl.BlockSpec((SUBLANES, tn), lambda j, i: ((i * tm) // seq_rows, j + off))
        st, tail = per_seq, per_seq(0)
        tail_shape = ((rows // seq_rows) * SUBLANES, dff)
        keep_shape = (SUBLANES, tn)
    else:
        assert seq_rows == SUBLANES and tm % seq_rows == 0
        st = lambda off: pl.BlockSpec((tm, tn), lambda j, i: (i, j + off))
        tail = pl.BlockSpec((taps - 1, tm // seq_rows, tn), lambda j, i: (0, i, j))
        tail_shape = (taps - 1, rows // seq_rows, dff)
        keep_shape = (tn // LANES, tm, LANES)
    wt = lambda off: pl.BlockSpec((d, tn), lambda j, i: (0, j + off))
    cw = lambda off: pl.BlockSpec((taps, tn), lambda j, i: (0, j + off))
    cb = lambda off: pl.BlockSpec((1, tn), lambda j, i: (0, j + off))
    out_specs = [pl.BlockSpec((tm, tn), lambda j, i: (i, j)), tail, tail]
    out_shape = [jax.ShapeDtypeStruct((rows, dff), BF16), jax.ShapeDtypeStruct(tail_shape, F32),
                 jax.ShapeDtypeStruct(tail_shape, F32)]
    if rounding:
        w_gate, w_val, w_specs = w_up, w_up, [wt(0), wt(nj)]
        out_specs += [wt(0), wt(0)]
        out_shape += [jax.ShapeDtypeStruct((d, dff), BF16)] * 2
    else:
        (w_gate, w_val), w_specs = w_up, [wt(0), wt(0)]
    outs = pl.pallas_call(
        functools.partial(_up_conv_gate_body, seq_rows=seq_rows, tm=tm, taps=taps, rounding=rounding),
        grid=(nj, rows // tm),
        in_specs=[pl.BlockSpec((tm, d), lambda j, i: (i, 0))] + w_specs + [st(0), st(nj),
                                                                             cw(0), cw(nj), cb(0), cb(nj)],
        out_specs=out_specs,
        out_shape=out_shape,
        scratch_shapes=[pltpu.VMEM(keep_shape, F32), pltpu.VMEM(keep_shape, F32)],
        compiler_params=_params("parallel", "arbitrary"),
        name="up_conv_gate",
    )(x, w_gate, w_val, state_rows, state_rows, conv_w, conv_w, conv_b.reshape(1, f2), conv_b.reshape(1, f2))
    act, tail_g, tail_v = outs[:3]
    w_bf = tuple(outs[3:]) if rounding else w_up
    tail = jnp.concatenate([tail_g, tail_v], axis=-1)
    if seq_rows >= tm:
        return act, tail.reshape(rows // seq_rows, SUBLANES, f2)[:, SUBLANES - (taps - 1):], w_bf
    return act, jnp.swapaxes(tail, 0, 1), w_bf


def _layer(x, shift_prev, wkv0, mem_k, mem_v, conv_prev, prm, want_v_rows):
    batch, seq_len, d = x.shape
    rows = batch * seq_len
    w = prm["width"]
    hd = prm["head_dim"]
    x2 = x.reshape(rows, d)

    xn = _rmsnorm(x2, prm["g_pre_mix"], BF16)
    n_shift = prm["mu_shift"].shape[1]
    rw = _matmul_cached(xn, prm, "w_in", 0, n_shift)
    guvq = _matmul_cached(xn, prm, "w_in", n_shift, 3 * w)
    new_shift = rw.reshape(batch, seq_len, -1)[:, -1]

    r, lw, k, v, a, b, g, bonus = _rwkv_prep(rw, _pad_state_rows(shift_prev[:, None, :]), seq_len, prm)
    y, h_new = _wkv(r, lw, k, v, a, b, jnp.swapaxes(wkv0, -1, -2), batch, seq_len, hd)
    new_wkv = jnp.swapaxes(h_new, -1, -2)
    a_out = _rwkv_post(y, bonus, g, prm, hd)

    if seq_len % prm["sgu_chunk"] == 0:
        b_out, v_rows = _sgu(guvq, prm["w_s"], prm["sgu_bias"], prm, None, want_v_rows)
    else:
        assert seq_len == SUBLANES
        b_out, v_rows = _sgu(guvq, prm["w_s"], prm["sgu_bias"][:, :SUBLANES], prm, SUBLANES, want_v_rows)
    c_out = _xattn(guvq, mem_k, mem_v, seq_len, w)

    mix, (prm["w_gates"], prm["w_branch"]) = _gate_branch_mix(
        xn, a_out, b_out, c_out, prm["w_gates"], n_shift + 3 * w, prm["w_branch"])
    h, hn = _mm_res_norm(mix, prm["w_out"], x2, prm["g_post_mix"], prm["g_pre_ffn"])

    act, conv_new, prm["w_up"] = _up_conv_gate(hn, prm["w_up"], _pad_state_rows(conv_prev), prm["conv_w"],
                                               prm["conv_b"], seq_len)
    y_out = _mm_res_norm(act, prm["w_down"], h, prm["g_post_ffn"])
    if want_v_rows:
        v_rows = v_rows.reshape(batch, seq_len, w)
    return y_out.reshape(batch, seq_len, d), new_shift, new_wkv, v_rows, conv_new


def _prepare(l, g_pre_mix, w_in, mu_shift, w0, w_decay, a0, w_aaa, w_gate, k_k, k_a, r_k, lnx_w, lnx_b,
             sgu_g, sgu_b, w_s, b_s, w_branch, w_out, g_post_mix, g_pre_ffn, w_up, conv_w, conv_b, w_down,
             g_post_ffn):
    heads, hd = r_k.shape[1], r_k.shape[2]
    w = heads * hd
    d = w_in.shape[1]
    n_shift = mu_shift.shape[1]
    rank_d, rank_a, rank_g = w_decay.shape[1], w_aaa.shape[1], w_gate.shape[1]
    assert rank_d + rank_a == LANES and rank_g == LANES and n_shift == 3 * w + 2 * LANES
    groups, sgu_chunk, _ = w_s.shape[1:]
    row = lambda t: t[l].reshape(1, -1)
    lane_head = jnp.arange(LANES) // hd
    gd = w // groups
    return dict(
        width=w, head_dim=hd, sgu_chunk=sgu_chunk,
        g_pre_mix=g_pre_mix[l],
        w_in=w_in[l], w_gates=w_in[l],
        mu_shift=row(mu_shift), w0=row(w0), a0=row(a0),
        wd_pad=jnp.pad(w_decay[l], ((0, rank_a), (0, 0))),
        wa_pad=jnp.pad(w_aaa[l], ((rank_d, 0), (0, 0))),
        w_gate=w_gate[l],
        k_k=row(k_k), k_a=row(k_a), r_k=row(r_k), lnx_w=row(lnx_w), lnx_b=row(lnx_b),
        ones2=(lane_head[:, None] == lane_head[None, :]).astype(BF16),
        sgu_g=row(sgu_g), sgu_b=row(sgu_b),
        w_s=w_s[l],
        sgu_bias=jnp.broadcast_to(b_s[l][:, :, None], (groups, sgu_chunk, gd)),
        w_branch=w_branch[l], w_out=w_out[l].astype(BF16),
        g_post_mix=g_post_mix[l], g_pre_ffn=g_pre_ffn[l],
        w_up=w_up[l], conv_w=conv_w[l], conv_b=conv_b[l], w_down=w_down[l].astype(BF16),
        g_post_ffn=g_post_ffn[l],
    )


def kernel(x_prompt, x_sample, mem_prompt, state_wkv, state_shift, cache_mem_k, cache_mem_v, state_ffn_conv, g_pre_mix, w_in, mu_shift, w0, w_decay, a0, w_aaa, w_gate, k_k, k_a, r_k, lnx_w, lnx_b, sgu_g, sgu_b, w_s, b_s, g_mem, w_mem_k, w_mem_v, w_branch, w_out, g_post_mix, g_pre_ffn, w_up, conv_w, conv_b, w_down, g_post_ffn):
    depth = w_in.shape[0]
    batch = x_prompt.shape[0]
    mem_len, d = mem_prompt.shape[1], mem_prompt.shape[2]
    xh, xhd = cache_mem_k.shape[3], cache_mem_k.shape[4]
    heads, hd = r_k.shape[1], r_k.shape[2]
    n_shift = mu_shift.shape[1]
    f2 = w_up.shape[2]
    taps = conv_w.shape[1]
    y_p, y_s = x_prompt, x_sample
    outs = [[] for _ in range(9)]
    for l in range(depth):
        prm = _prepare(l, g_pre_mix, w_in, mu_shift, w0, w_decay, a0, w_aaa, w_gate, k_k, k_a, r_k, lnx_w, lnx_b,
                       sgu_g, sgu_b, w_s, b_s, w_branch, w_out, g_post_mix, g_pre_ffn, w_up, conv_w, conv_b,
                       w_down, g_post_ffn)
        mn = _rmsnorm(mem_prompt.reshape(batch * mem_len, d), g_mem[l], BF16)
        mk_p = _matmul(mn, w_mem_k[l])[0].reshape(batch, mem_len, xh, xhd)
        mv_p = _matmul(mn, w_mem_v[l])[0].reshape(batch, mem_len, xh, xhd)
        y_p, sh_p, wkv_p, _, cv_p = _layer(
            y_p, jnp.zeros((batch, n_shift), F32), jnp.zeros((batch, heads, hd, hd), F32), mk_p, mv_p,
            jnp.zeros((batch, taps - 1, f2), F32), prm, False)
        y_s, sh_s, wkv_s, vr_s, cv_s = _layer(y_s, state_shift[l], state_wkv[l], cache_mem_k[l], cache_mem_v[l],
                                              state_ffn_conv[l], prm, True)
        for lst, val in zip(outs, (wkv_p, sh_p, mk_p, mv_p, cv_p, wkv_s, sh_s, vr_s, cv_s)):
            lst.append(val)
    return (y_p, y_s) + tuple(jnp.stack(lst) for lst in outs)
```

```python
import functools

import jax
import jax.numpy as jnp
from jax import lax
from jax.experimental import pallas as pl
from jax.experimental.pallas import tpu as pltpu

F32 = jnp.float32
BF16 = jnp.bfloat16
HIGHEST = lax.Precision.HIGHEST

LANES = 128
SUBLANES = 8
VMEM_LIMIT_BYTES = 56 * 1024 * 1024

RMS_EPS = 1e-6
LN_EPS = 1e-5
GN_EPS = 64e-5
KK_EPS = 1e-12
RWKV_CHUNK = 64


def _params(*semantics):
    return pltpu.CompilerParams(dimension_semantics=semantics, vmem_limit_bytes=VMEM_LIMIT_BYTES)


def _row_tile(rows, pref):
    t = min(rows, pref)
    while rows % t:
        t -= SUBLANES
    return t


def _col_tile(n, cap):
    best = n
    for t in range(LANES, min(n, cap) + 1, LANES):
        if n % t == 0:
            best = t
    return best


def _bdot(a, b):
    return jnp.dot(a.astype(BF16), b.astype(BF16), preferred_element_type=F32)


def _bdot_nt(a, b):
    return lax.dot_general(a.astype(BF16), b.astype(BF16), (((1,), (1,)), ((), ())),
                           preferred_element_type=F32)


def _fdot(a, b):
    return jnp.dot(a, b, precision=HIGHEST, preferred_element_type=F32)


def _dot3(a, b):
    a_hi = a.astype(BF16)
    b_hi = b.astype(BF16)
    a_lo = (a - a_hi.astype(F32)).astype(BF16)
    b_lo = (b - b_hi.astype(F32)).astype(BF16)
    dot = lambda x, y: jnp.dot(x, y, preferred_element_type=F32)
    return dot(a_hi, b_hi) + (dot(a_hi, b_lo) + dot(a_lo, b_hi))


def _rmsnorm_body(x_ref, g_ref, o_ref):
    x = x_ref[...]
    y = x * lax.rsqrt(jnp.mean(x * x, axis=-1, keepdims=True) + RMS_EPS)
    o_ref[...] = (y * g_ref[...]).astype(o_ref.dtype)


def _rmsnorm(x, g, out_dtype):
    rows, d = x.shape
    tm = _row_tile(rows, 512)
    return pl.pallas_call(
        _rmsnorm_body,
        grid=(rows // tm,),
        in_specs=[pl.BlockSpec((tm, d), lambda i: (i, 0)), pl.BlockSpec((1, d), lambda i: (0, 0))],
        out_specs=pl.BlockSpec((tm, d), lambda i: (i, 0)),
        out_shape=jax.ShapeDtypeStruct((rows, d), out_dtype),
        compiler_params=_params("parallel"),
        name="rmsnorm",
    )(x, g.reshape(1, d))


def _matmul_round_body(x_ref, w_ref, o_ref, wb_ref):
    @pl.when(pl.program_id(1) == 0)
    def _():
        wb_ref[...] = w_ref[...].astype(BF16)

    o_ref[...] = jnp.dot(x_ref[...], wb_ref[...], preferred_element_type=F32).astype(o_ref.dtype)


def _matmul_body(x_ref, w_ref, o_ref):
    o_ref[...] = jnp.dot(x_ref[...], w_ref[...], preferred_element_type=F32).astype(o_ref.dtype)


def _matmul(x, w, col_start=0, n_cols=None, out_dtype=F32):
    rows, k = x.shape
    n = w.shape[1] - col_start if n_cols is None else n_cols
    assert col_start % LANES == 0 and n % LANES == 0
    rounding = w.dtype != BF16
    tn = _col_tile(n, 1792)
    budget = VMEM_LIMIT_BYTES - (4 << 20)
    tm = _row_tile(rows, 1024)
    w_bytes = 2 * k * tn * (4 + 2) if rounding else 2 * k * tn * 2
    while w_bytes + 2 * tm * k * 2 + 2 * tm * tn * 4 > budget and tm % (2 * SUBLANES) == 0:
        tm //= 2
    w_spec = pl.BlockSpec((pl.Element(k), pl.Element(tn)),
                          lambda j, i: (0, pl.multiple_of(col_start + j * tn, LANES)))
    out_spec = pl.BlockSpec((tm, tn), lambda j, i: (i, j))
    out_shape = jax.ShapeDtypeStruct((rows, n), out_dtype)
    if rounding:
        out_spec = [out_spec, pl.BlockSpec((k, tn), lambda j, i: (0, j))]
        out_shape = [out_shape, jax.ShapeDtypeStruct((k, n), BF16)]
    return pl.pallas_call(
        _matmul_round_body if rounding else _matmul_body,
        grid=(n // tn, rows // tm),
        in_specs=[pl.BlockSpec((tm, k), lambda j, i: (i, 0)), w_spec],
        out_specs=out_spec,
        out_shape=out_shape,
        compiler_params=_params("parallel", "arbitrary"),
        name="matmul",
    )(x, w)


def _matmul_cached(x, prm, name, col_start=0, n_cols=None):
    key = (name, col_start)
    if key in prm:
        return _matmul(x, prm[key])
    out, prm[key] = _matmul(x, prm[name], col_start, n_cols)
    return out


def _shift_rows(x, halo, k):
    rolled = pltpu.roll(x, k, 0)
    row = lax.broadcasted_iota(jnp.int32, (SUBLANES, x.shape[1]), 0)
    top = rolled[0:SUBLANES]
    for j in range(k):
        top = jnp.where(row == j, halo[SUBLANES - k + j:SUBLANES - k + j + 1, :], top)
    return jnp.concatenate([top, rolled[SUBLANES:]], axis=0)


def _head_sum(x, ones_ref):
    ones = ones_ref[...]
    hi = x.astype(BF16)
    lo = (x - hi.astype(F32)).astype(BF16)
    outs = []
    for j in range(x.shape[1] // LANES):
        sl = slice(j * LANES, (j + 1) * LANES)
        outs.append(jnp.dot(hi[:, sl], ones, preferred_element_type=F32)
                    + jnp.dot(lo[:, sl], ones, preferred_element_type=F32))
    return jnp.concatenate(outs, axis=1)


def _pad_state_rows(state):
    b, n, f = state.shape
    return jnp.pad(state, ((0, 0), (0, SUBLANES - n), (0, 0))).reshape(b * SUBLANES, f)


def _rwkv_prep_body(rw_ref, halo_ref, sp_ref, mu_ref, w0_ref, wd_ref, a0_ref, wa_ref, wg_ref,
                    kk_ref, ka_ref, rk_ref, ones_ref,
                    r_o, lw_o, k_o, v_o, a_o, b_o, g_o, bonus_o, *, seq_rows, tm, width):
    x = rw_ref[...]
    if seq_rows >= tm:
        is_start = (pl.program_id(0) * tm) % seq_rows == 0
        halo = jnp.where(is_start, pltpu.roll(sp_ref[...], SUBLANES - 1, 0), halo_ref[...])
        prev = _shift_rows(x, halo, 1)
    else:
        t = lax.broadcasted_iota(jnp.int32, x.shape, 0) & (seq_rows - 1)
        prev = jnp.where(t == 0, sp_ref[...], pltpu.roll(x, 1, 0))
    xs = x + mu_ref[...] * (prev - x)

    w = width
    r = xs[:, 0:w]
    k = xs[:, w:2 * w]
    v = xs[:, 2 * w:3 * w]
    x_wa = xs[:, 3 * w:3 * w + LANES]
    x_g = xs[:, 3 * w + LANES:3 * w + 2 * LANES]

    z = -(w0_ref[...] + _dot3(jnp.tanh(x_wa), wd_ref[...]))
    softplus = jnp.maximum(z, 0.0) + jnp.log(1.0 + jnp.exp(-jnp.abs(z)))
    lw = -jnp.exp(-softplus - 0.5)
    a_lr = jax.nn.sigmoid(a0_ref[...] + _bdot(x_wa, wa_ref[...]))
    g = _bdot(jax.nn.sigmoid(x_g), wg_ref[...])

    kk = k * kk_ref[...]
    k_mod = k * (1.0 + (a_lr - 1.0) * ka_ref[...])
    kk = kk / jnp.maximum(jnp.sqrt(_head_sum(kk * kk, ones_ref)), KK_EPS)
    bonus = _head_sum(r * k_mod * rk_ref[...], ones_ref) * v

    r_o[...] = r
    lw_o[...] = lw
    k_o[...] = k_mod
    v_o[...] = v.astype(v_o.dtype)
    a_o[...] = -kk
    b_o[...] = kk * a_lr
    g_o[...] = g
    bonus_o[...] = bonus


def _rwkv_prep(rw, shift_rows, seq_rows, prm):
    rows, n_shift = rw.shape
    w = prm["width"]
    tm = _row_tile(rows, 256)
    if seq_rows >= tm:
        assert seq_rows % tm == 0
        sp_spec = pl.BlockSpec((SUBLANES, n_shift), lambda i: ((i * tm) // seq_rows, 0))
    else:
        assert seq_rows == SUBLANES and tm % seq_rows == 0
        sp_spec = pl.BlockSpec((tm, n_shift), lambda i: (i, 0))
    row_spec = lambda c: pl.BlockSpec((1, c), lambda i: (0, 0))
    full = lambda a: pl.BlockSpec(a.shape, lambda i: (0, 0))
    out_spec = pl.BlockSpec((tm, w), lambda i: (i, 0))
    return pl.pallas_call(
        functools.partial(_rwkv_prep_body, seq_rows=seq_rows, tm=tm, width=w),
        grid=(rows // tm,),
        in_specs=[pl.BlockSpec((tm, n_shift), lambda i: (i, 0)),
                  pl.BlockSpec((SUBLANES, n_shift), lambda i: (jnp.maximum(i * (tm // SUBLANES) - 1, 0), 0)),
                  sp_spec, row_spec(n_shift), row_spec(w), full(prm["wd_pad"]), row_spec(w), full(prm["wa_pad"]),
                  full(prm["w_gate"]), row_spec(w), row_spec(w), row_spec(w), full(prm["ones2"])],
        out_specs=[out_spec] * 8,
        out_shape=[jax.ShapeDtypeStruct((rows, w), BF16 if i == 3 else F32) for i in range(8)],
        compiler_params=_params("parallel"),
        name="rwkv_prep",
    )(rw, rw, shift_rows, prm["mu_shift"], prm["w0"], prm["wd_pad"], prm["a0"], prm["wa_pad"], prm["w_gate"],
      prm["k_k"], prm["k_a"], prm["r_k"], prm["ones2"])


def _wkv_body(r_ref, lw_ref, k_ref, v_ref, a_ref, b_ref, h0_ref, y_ref, hout_ref, h_scr,
              *, chunk, pairs, seqs, width, hd):
    c = pl.program_id(1)
    n_pairs = width // LANES

    @pl.when(c == 0)
    def _():
        zero = jnp.zeros((hd, hd), F32)
        for i in range(seqs):
            for p in range(n_pairs):
                h_scr[i, p] = jnp.concatenate(
                    [jnp.concatenate([h0_ref[i, 2 * p], zero], axis=1),
                     jnp.concatenate([zero, h0_ref[i, 2 * p + 1]], axis=1)], axis=0)

    cs = chunk
    lanes = LANES * pairs
    heads = 2 * pairs
    n = heads * cs
    log_c = cs.bit_length() - 1
    log_hd = hd.bit_length() - 1
    n_stacks = width // lanes

    tri = (lax.broadcasted_iota(jnp.int32, (cs, cs), 0) >= lax.broadcasted_iota(jnp.int32, (cs, cs), 1)).astype(F32)
    row = lax.broadcasted_iota(jnp.int32, (n, lanes), 0)
    lane = lax.broadcasted_iota(jnp.int32, (n, lanes), 1)
    head_mask = (row >> log_c) == (lane >> log_hd)
    ri = lax.broadcasted_iota(jnp.int32, (n, n), 0)
    ci = lax.broadcasted_iota(jnp.int32, (n, n), 1)
    same_head = (ri >> log_c) == (ci >> log_c)
    rt = ri & (cs - 1)
    ct = ci & (cs - 1)
    strict = same_head & (ct < rt)
    incl = same_head & (ct <= rt)
    eye = (ri == ci).astype(F32)
    diag = (lax.broadcasted_iota(jnp.int32, (LANES, LANES), 0)
            == lax.broadcasted_iota(jnp.int32, (LANES, LANES), 1))

    def stack(x):
        return jnp.where(head_mask, jnp.concatenate([x] * heads, axis=0), 0.0)

    def fold(x):
        out = x[0:cs]
        for hidx in range(1, heads):
            out = out + x[hidx * cs:(hidx + 1) * cs]
        return out

    scaled = []
    for i in range(seqs):
        rs = slice(i * cs, (i + 1) * cs)
        lw = lw_ref[rs, :]
        cum = _fdot(tri, lw)
        total = cum[cs - 1:cs]
        e_cum = jnp.exp(cum)
        e_neg = jnp.exp(-cum)
        e_rest = jnp.exp(total - cum)
        b = b_ref[rs, :]
        k = k_ref[rs, :]
        scaled.append(dict(a=a_ref[rs, :] * jnp.exp(cum - lw), r=r_ref[rs, :] * e_cum, b=b * e_neg, k=k * e_neg,
                           v=v_ref[rs, :].astype(F32), bh=b * e_rest, kh=k * e_rest, total=total))

    inst = [(i, s) for i in range(seqs) for s in range(n_stacks)]
    jj = range(len(inst))

    def stacked(name):
        return [stack(scaled[i][name][:, s * lanes:(s + 1) * lanes]) for i, s in inst]

    a_st, r_st, b_st, k_st, v_st, bh_st, kh_st = (stacked(nm) for nm in ("a", "r", "b", "k", "v", "bh", "kh"))

    gram = [_bdot_nt(jnp.concatenate([a_st[j], r_st[j]], axis=0), jnp.concatenate([b_st[j], k_st[j]], axis=0))
            for j in jj]
    l_ab = [jnp.where(strict, gram[j][:n, :n], 0.0) for j in jj]
    m_rb = [jnp.where(incl, gram[j][n:, :n], 0.0) for j in jj]
    l_ak_m_rk = [jnp.concatenate([jnp.where(strict, gram[j][:n, n:], 0.0), jnp.where(incl, gram[j][n:, n:], 0.0)],
                                 axis=0) for j in jj]

    t_inv = [eye + l_ab[j] for j in jj]
    x_pow = [_bdot(l_ab[j], l_ab[j]) for j in jj]
    for _ in range(log_c - 2):
        both = [_bdot(jnp.concatenate([x_pow[j], t_inv[j]], axis=0), x_pow[j]) for j in jj]
        t_inv = [t_inv[j] + both[j][n:] for j in jj]
        x_pow = [both[j][:n] for j in jj]
    t_inv = [t_inv[j] + _bdot(t_inv[j], x_pow[j]) for j in jj]

    lv = [_bdot(l_ak_m_rk[j], v_st[j]) for j in jj]
    aw = [_bdot(t_inv[j], jnp.concatenate([a_st[j], lv[j][:n]], axis=1)) for j in jj]
    rb = [_bdot(m_rb[j], aw[j]) for j in jj]
    r_bar = [fold(r_st[j] + rb[j][:, :lanes]) for j in jj]
    y0 = [fold(rb[j][:, lanes:] + lv[j][n:]) for j in jj]

    jp = [(j, p) for j in jj for p in range(pairs)]

    def pair_lanes(x, p, off=0):
        return x[:, off + p * LANES:off + (p + 1) * LANES]

    h = [h_scr[inst[j][0], inst[j][1] * pairs + p] for j, p in jp]
    yv = [_bdot(pair_lanes(r_bar[j], p), h[q]) + pair_lanes(y0[j], p) for q, (j, p) in enumerate(jp)]
    for q, (j, p) in enumerate(jp):
        i, s = inst[j]
        y_ref[i * cs:(i + 1) * cs, s * lanes + p * LANES:s * lanes + (p + 1) * LANES] = yv[q]
    bh_t = [pair_lanes(bh_st[j], p).T for j, p in jp]
    kh_t = [pair_lanes(kh_st[j], p).T for j, p in jp]
    pp = [_bdot(bh_t[q], jnp.concatenate([pair_lanes(aw[j], p), pair_lanes(aw[j], p, lanes)], axis=1))
          for q, (j, p) in enumerate(jp)]
    kv = [_bdot(kh_t[q], pair_lanes(v_st[j], p)) for q, (j, p) in enumerate(jp)]
    for q, (j, p) in enumerate(jp):
        i, s = inst[j]
        decay = pair_lanes(scaled[i]["total"], p, s * lanes)
        phi = jnp.where(diag, jnp.exp(decay), 0.0) + pp[q][:, :LANES]
        h_scr[i, s * pairs + p] = _bdot(phi, h[q]) + (pp[q][:, LANES:] + kv[q])

    @pl.when(c == pl.num_programs(1) - 1)
    def _():
        for i in range(seqs):
            for p in range(n_pairs):
                both = h_scr[i, p]
                hout_ref[i, 2 * p] = both[:hd, :hd]
                hout_ref[i, 2 * p + 1] = both[hd:, hd:]


def _wkv(r, lw, k, v, a, b, h0, batch, seq_len, hd):
    rows, w = r.shape
    chunk = min(seq_len, RWKV_CHUNK)
    assert seq_len % chunk == 0 and chunk % SUBLANES == 0 and LANES % chunk == 0
    pairs = LANES // (2 * chunk)
    n_chunks = seq_len // chunk
    n_pairs = w // LANES
    seqs = 2 if (n_chunks == 1 and batch % 2 == 0) else 1
    vec = pl.BlockSpec((seqs * chunk, w), lambda bi, ci: (bi * n_chunks + ci, 0))
    st = pl.BlockSpec((seqs, w // hd, hd, hd), lambda bi, ci: (bi, 0, 0, 0))
    return pl.pallas_call(
        functools.partial(_wkv_body, chunk=chunk, pairs=pairs, seqs=seqs, width=w, hd=hd),
        grid=(batch // seqs, n_chunks),
        in_specs=[vec] * 6 + [st],
        out_specs=[vec, st],
        out_shape=[jax.ShapeDtypeStruct((rows, w), F32),
                   jax.ShapeDtypeStruct((batch, w // hd, hd, hd), F32)],
        scratch_shapes=[pltpu.VMEM((seqs, n_pairs, LANES, LANES), F32)],
        compiler_params=_params("parallel", "arbitrary"),
        name="rwkv_chunks",
    )(r, lw, k, v, a, b, h0)


def _rwkv_post_body(y_ref, bonus_ref, g_ref, lw_ref, lb_ref, ones_ref, o_ref, *, hd):
    y = y_ref[...]
    mu = _head_sum(y, ones_ref) * (1.0 / hd)
    d = y - mu
    var = _head_sum(d * d, ones_ref) * (1.0 / hd)
    yn = d * lax.rsqrt(var + GN_EPS) * lw_ref[...] + lb_ref[...]
    o_ref[...] = ((yn + bonus_ref[...]) * g_ref[...]).astype(o_ref.dtype)


def _rwkv_post(y, bonus, g, prm, hd):
    rows, w = y.shape
    tm = _row_tile(rows, 512)
    blk = pl.BlockSpec((tm, w), lambda i: (i, 0))
    row_spec = pl.BlockSpec((1, w), lambda i: (0, 0))
    return pl.pallas_call(
        functools.partial(_rwkv_post_body, hd=hd),
        grid=(rows // tm,),
        in_specs=[blk, blk, blk, row_spec, row_spec, pl.BlockSpec((LANES, LANES), lambda i: (0, 0))],
        out_specs=blk,
        out_shape=jax.ShapeDtypeStruct((rows, w), BF16),
        compiler_params=_params("parallel"),
        name="rwkv_post",
    )(y, bonus, g, prm["lnx_w"], prm["lnx_b"], prm["ones2"])


def _sgu_body(gu_ref, gv_ref, sg_ref, sb_ref, wm_ref, bias_ref, o_ref, *maybe_v, groups, seq_rows):
    u = jax.nn.gelu(gu_ref[...])
    vf = jax.nn.gelu(gv_ref[...])
    mu = jnp.mean(vf, axis=-1, keepdims=True)
    d = vf - mu
    var = jnp.mean(d * d, axis=-1, keepdims=True)
    v = (d * lax.rsqrt(var + LN_EPS)) * sg_ref[...] + sb_ref[...]
    if maybe_v:
        maybe_v[0][...] = v
    rows = v.shape[0]
    gd = v.shape[1] // groups
    ri = lax.broadcasted_iota(jnp.int32, (rows, rows), 0)
    ci = lax.broadcasted_iota(jnp.int32, (rows, rows), 1)
    causal = ri >= ci
    if seq_rows is not None:
        shift = seq_rows.bit_length() - 1
        causal = causal & ((ri >> shift) == (ci >> shift))
        pos = ((lax.broadcasted_iota(jnp.int32, (rows, seq_rows), 0) & (seq_rows - 1))
               == lax.broadcasted_iota(jnp.int32, (rows, seq_rows), 1)).astype(F32)
    for g in range(groups):
        sl = slice(g * gd, (g + 1) * gd)
        if seq_rows is None:
            w_full, bias = wm_ref[g], bias_ref[g]
        else:
            w_full = lax.dot_general(_fdot(pos, wm_ref[g][:seq_rows, :seq_rows]), pos, (((1,), (1,)), ((), ())),
                                     precision=HIGHEST, preferred_element_type=F32)
            bias = _fdot(pos, bias_ref[g])
        s = _bdot(jnp.where(causal, w_full, 0.0), v[:, sl]) + bias
        o_ref[:, sl] = (u[:, sl] * s).astype(o_ref.dtype)


def _sgu(guvq, w_mix, bias, prm, seq_rows, want_v_rows):
    rows = guvq.shape[0]
    w = prm["width"]
    groups, chunk, _ = w_mix.shape
    cs = chunk if seq_rows is None else _row_tile(rows, 2 * LANES)
    row_spec = pl.BlockSpec((1, w), lambda i: (0, 0))
    out = pl.BlockSpec((cs, w), lambda i: (i, 0))
    outs = pl.pallas_call(
        functools.partial(_sgu_body, groups=groups, seq_rows=seq_rows),
        grid=(rows // cs,),
        in_specs=[pl.BlockSpec((cs, w), lambda i: (i, 0)), pl.BlockSpec((cs, w), lambda i: (i, 1)),
                  row_spec, row_spec,
                  pl.BlockSpec(w_mix.shape, lambda i: (0, 0, 0)), pl.BlockSpec(bias.shape, lambda i: (0, 0, 0))],
        out_specs=[out, out] if want_v_rows else [out],
        out_shape=[jax.ShapeDtypeStruct((rows, w), BF16)] + ([jax.ShapeDtypeStruct((rows, w), F32)]
                                                           if want_v_rows else []),
        compiler_params=_params("parallel"),
        name="sgu",
    )(guvq, guvq, prm["sgu_g"], prm["sgu_b"], w_mix, bias)
    return (outs[0], outs[1]) if want_v_rows else (outs[0], None)


def _xattn_body(q_ref, k_ref, v_ref, o_ref, *, heads, seqs, tq, m):
    hd = q_ref.shape[1] // heads
    scale = hd ** -0.5
    cases = [(h, i) for h in range(heads) for i in range(seqs)]
    cols = lambda h: slice(h * hd, (h + 1) * hd)
    mem = lambda i: slice(i * m, (i + 1) * m)
    s = [_bdot_nt(q_ref[i * tq:(i + 1) * tq, cols(h)], k_ref[mem(i), cols(h)]) * scale for h, i in cases]
    e = [jnp.exp(x - jnp.max(x, axis=-1, keepdims=True)) for x in s]
    p = [x / jnp.sum(x, axis=-1, keepdims=True) for x in e]
    o = [_bdot(p[c], v_ref[mem(i), cols(h)]) for c, (h, i) in enumerate(cases)]
    for h in range(heads):
        o_ref[:, cols(h)] = jnp.concatenate(o[h * seqs:(h + 1) * seqs], axis=0).astype(o_ref.dtype)


def _xattn_cache_body(q_ref, k_ref, v_ref, o_ref, *, heads, seqs, tq):
    hd = q_ref.shape[1] // heads
    m = k_ref.shape[1]
    n = m * heads
    scale = hd ** -0.5
    row_head = lax.broadcasted_iota(jnp.int32, (heads * tq, n), 0) >> (tq.bit_length() - 1)
    lane_head = lax.broadcasted_iota(jnp.int32, (heads * tq, n), 1) & (heads - 1)
    own = row_head == lane_head
    q = [jnp.concatenate([q_ref[i * tq:(i + 1) * tq, h * hd:(h + 1) * hd] for h in range(heads)], axis=0)
         for i in range(seqs)]
    s = [jnp.where(own, _bdot_nt(q[i], k_ref[i].reshape(n, hd)) * scale, -1e30) for i in range(seqs)]
    e = [jnp.exp(x - jnp.max(x, axis=-1, keepdims=True)) for x in s]
    p = [x / jnp.sum(x, axis=-1, keepdims=True) for x in e]
    o = [_bdot(p[i], v_ref[i].reshape(n, hd)) for i in range(seqs)]
    for h in range(heads):
        o_ref[:, h * hd:(h + 1) * hd] = jnp.concatenate(
            [o[i][h * tq:(h + 1) * tq] for i in range(seqs)], axis=0).astype(o_ref.dtype)


def _xattn(guvq, mem_k, mem_v, batch, seq_len, heads):
    rows = guvq.shape[0]
    w = guvq.shape[1] // 3
    tq = _row_tile(seq_len, 512)
    nq = seq_len // tq
    seqs = max(1, 32 // tq)
    assert batch % seqs == 0 and (seqs == 1 or nq == 1)
    if mem_k.ndim == 2:
        m = mem_k.shape[0] // batch
        kv = pl.BlockSpec((seqs * m, w), lambda bi, qi: (bi, 0))
        body = functools.partial(_xattn_body, heads=heads, seqs=seqs, tq=tq, m=m)
    else:
        assert tq & (tq - 1) == 0 and heads & (heads - 1) == 0
        kv = pl.BlockSpec((seqs,) + mem_k.shape[1:], lambda bi, qi: (bi, 0, 0, 0))
        body = functools.partial(_xattn_cache_body, heads=heads, seqs=seqs, tq=tq)
    qo = lambda col: pl.BlockSpec((seqs * tq, w), lambda bi, qi: (bi * nq + qi, col))
    return pl.pallas_call(
        body,
        grid=(batch // seqs, nq),
        in_specs=[qo(2), kv, kv],
        out_specs=qo(0),
        out_shape=jax.ShapeDtypeStruct((rows, w), BF16),
        compiler_params=_params("parallel", "parallel"),
        name="mem_xattn",
    )(guvq, mem_k, mem_v)


def _gate_branch_body(*refs, rounding):
    if rounding:
        x_ref, a_ref, b_ref, c_ref, wg0_ref, wg1_ref, wg2_ref, wb_ref, o_ref, wg_bf, wb_bf = refs

        @pl.when(pl.program_id(1) == 0)
        def _():
            for n, src in enumerate((wg0_ref, wg1_ref, wg2_ref)):
                wg_bf[n] = src[...].astype(BF16)
            wb_bf[...] = wb_ref[...].astype(BF16)
    else:
        x_ref, a_ref, b_ref, c_ref, wg_bf, wb_bf, o_ref = refs

    x = x_ref[...]
    acc = None
    for n, br_ref in enumerate((a_ref, b_ref, c_ref)):
        gate = jax.nn.sigmoid(jnp.dot(x, wg_bf[n], preferred_element_type=F32))
        term = gate * jnp.dot(br_ref[...], wb_bf[n], preferred_element_type=F32)
        acc = term if acc is None else acc + term
    o_ref[...] = acc.astype(o_ref.dtype)


def _gate_branch_mix(xn, a_out, b_out, c_out, w_gates, gate_col0, w_branch):
    rows, k = xn.shape
    w = a_out.shape[1]
    nb, _, d = w_branch.shape
    assert nb == 3
    rounding = w_branch.dtype != BF16
    tm = _row_tile(rows, 1024)
    tn = _col_tile(d, 256)
    nj = d // tn
    act = lambda width: pl.BlockSpec((tm, width), lambda j, i: (i, 0))
    wb_spec = pl.BlockSpec((nb, w, tn), lambda j, i: (0, 0, j))
    wg_spec = pl.BlockSpec((nb, k, tn), lambda j, i: (0, 0, j))
    out_specs = [pl.BlockSpec((tm, tn), lambda j, i: (i, j))]
    out_shape = [jax.ShapeDtypeStruct((rows, d), BF16)]
    if rounding:
        gate_cols = lambda n: pl.BlockSpec(
            (pl.Element(k), pl.Element(tn)),
            lambda j, i: (0, pl.multiple_of(gate_col0 + n * d + j * tn, LANES)))
        w_specs = [gate_cols(0), gate_cols(1), gate_cols(2), wb_spec]
        w_args = (w_gates, w_gates, w_gates, w_branch)
        out_specs += [wg_spec, wb_spec]
        out_shape += [jax.ShapeDtypeStruct((nb, k, d), BF16), jax.ShapeDtypeStruct(w_branch.shape, BF16)]
    else:
        w_specs = [wg_spec, wb_spec]
        w_args = (w_gates, w_branch)
    outs = pl.pallas_call(
        functools.partial(_gate_branch_body, rounding=rounding),
        grid=(nj, rows // tm),
        in_specs=[act(k), act(w), act(w), act(w)] + w_specs,
        out_specs=out_specs,
        out_shape=out_shape,
        compiler_params=_params("parallel", "arbitrary"),
        name="gate_branch_mix",
    )(xn, a_out, b_out, c_out, *w_args)
    return outs[0], ((outs[1], outs[2]) if rounding else (w_gates, w_branch))


def _mm_res_norm_body(x_ref, w_ref, res_ref, g_ref, g2_ref, y_ref, *maybe_next, with_next):
    f = jnp.dot(x_ref[...], w_ref[...], preferred_element_type=F32)
    y = res_ref[...] + (f * lax.rsqrt(jnp.mean(f * f, axis=-1, keepdims=True) + RMS_EPS)) * g_ref[...]
    y_ref[...] = y
    if with_next:
        yn = y * lax.rsqrt(jnp.mean(y * y, axis=-1, keepdims=True) + RMS_EPS)
        maybe_next[0][...] = (yn * g2_ref[...]).astype(maybe_next[0].dtype)


def _mm_res_norm(x, w, res, g, g_next=None):
    rows, k = x.shape
    d = w.shape[1]
    with_next = g_next is not None
    tm = _row_tile(rows, 512 if k * d * 2 <= (12 << 20) else 256)
    blk = pl.BlockSpec((tm, d), lambda i: (i, 0))
    row_spec = pl.BlockSpec((1, d), lambda i: (0, 0))
    out_specs = [blk, blk] if with_next else [blk]
    out_shape = [jax.ShapeDtypeStruct((rows, d), F32)]
    if with_next:
        out_shape.append(jax.ShapeDtypeStruct((rows, d), BF16))
    g2 = g_next if with_next else g
    outs = pl.pallas_call(
        functools.partial(_mm_res_norm_body, with_next=with_next),
        grid=(rows // tm,),
        in_specs=[pl.BlockSpec((tm, k), lambda i: (i, 0)),
                  pl.BlockSpec((k, d), lambda i: (0, 0), pipeline_mode=pl.Buffered(1)),
                  blk, row_spec, row_spec],
        out_specs=out_specs,
        out_shape=out_shape,
        compiler_params=_params("arbitrary"),
        name="matmul_res_norm",
    )(x, w, res, g.reshape(1, d), g2.reshape(1, d))
    return outs if with_next else outs[0]


def _up_conv_gate_body(x_ref, wg_ref, wv_ref, eg_ref, ev_ref, cwg_ref, cwv_ref, cbg_ref, cbv_ref,
                       o_ref, tg_ref, tv_ref, *rest, seq_rows, tm, taps, rounding):
    i = pl.program_id(1)
    if rounding:
        wg_bf, wv_bf, keep_g, keep_v = rest
    else:
        keep_g, keep_v = rest
        wg_bf, wv_bf = wg_ref, wv_ref

    @pl.when(i == 0)
    def _():
        if rounding:
            wg_bf[...] = wg_ref[...].astype(BF16)
            wv_bf[...] = wv_ref[...].astype(BF16)
        keep_g[...] = jnp.zeros_like(keep_g)
        keep_v[...] = jnp.zeros_like(keep_v)

    x = x_ref[...]

    def conv(w_bf, e_ref, cw_ref, cb_ref, keep, tail_ref):
        u = jnp.dot(x, w_bf[...], preferred_element_type=F32)
        cw = cw_ref[...]
        acc = cb_ref[...] + cw[taps - 1:taps] * u
        n_lane_tiles = u.shape[1] // LANES
        if seq_rows >= tm:
            is_start = (i * tm) % seq_rows == 0
            row8 = lax.broadcasted_iota(jnp.int32, (SUBLANES, u.shape[1]), 0)
            start = jnp.zeros((SUBLANES, u.shape[1]), F32)
            for idx in range(taps - 1):
                start = jnp.where(row8 == SUBLANES - (taps - 1) + idx, e_ref[0, idx:idx + 1, :], start)
            halo = jnp.where(is_start, start, keep[...])
            for back in range(1, taps):
                acc = acc + cw[taps - 1 - back:taps - back] * _shift_rows(u, halo, back)
            keep[...] = u[tm - SUBLANES:tm]
            tail_ref[...] = u[tm - SUBLANES:tm]
        else:
            t = lax.broadcasted_iota(jnp.int32, u.shape, 0) & (seq_rows - 1)
            for idx in range(taps - 1):
                for c in range(n_lane_tiles):
                    keep[c, pl.ds(idx, tm // seq_rows, stride=seq_rows), :] = e_ref[:, idx, c * LANES:(c + 1) * LANES]
            e = jnp.concatenate([keep[c] for c in range(n_lane_tiles)], axis=1)
            for back in range(1, taps):
                up_by = taps - 1 - back
                state = pltpu.roll(e, tm - up_by, 0) if up_by else e
                prev = jnp.where(t < back, state, pltpu.roll(u, back, 0))
                acc = acc + cw[taps - 1 - back:taps - back] * prev
            for c in range(n_lane_tiles):
                keep[c] = u[:, c * LANES:(c + 1) * LANES]
            for idx in range(taps - 1):
                rows_t = pl.ds(seq_rows - (taps - 1) + idx, tm // seq_rows, stride=seq_rows)
                tail_ref[idx] = jnp.concatenate([keep[c, rows_t, :] for c in range(n_lane_tiles)], axis=1)
        return acc

    gate = conv(wg_bf, eg_ref, cwg_ref, cbg_ref, keep_g, tg_ref)
    val = conv(wv_bf, ev_ref, cwv_ref, cbv_ref, keep_v, tv_ref)
    o_ref[...] = (jax.nn.gelu(gate) * val).astype(o_ref.dtype)


def _up_conv_gate(x, w_up, state_rows, conv_w, conv_b, seq_rows):
    rows, d = x.shape
    rounding = not isinstance(w_up, tuple)
    f2 = conv_w.shape[1]
    dff = f2 // 2
    taps = conv_w.shape[0]
    tm = _row_tile(rows if seq_rows == SUBLANES else seq_rows, 1024)
    tn = _col_tile(dff, 512)
    nj = dff // tn
    if seq_rows >= tm:
        assert seq_rows % tm == 0
        st = lambda off: pl.BlockSpec((1, taps - 1, tn), lambda j, i: ((i * tm) // seq_rows, 0, j + off))
        tail = pl.BlockSpec((SUBLANES, tn), lambda j, i: ((i * tm) // seq_rows, j))
        tail_shape = ((rows // seq_rows) * SUBLANES, dff)
        keep_shape = (SUBLANES, tn)
    else:
        assert seq_rows == SUBLANES and tm % seq_rows == 0
        st = lambda off: pl.BlockSpec((tm // seq_rows, taps - 1, tn), lambda j, i: (i, 0, j + off))
        tail = pl.BlockSpec((taps - 1, tm // seq_rows, tn), lambda j, i: (0, i, j))
        tail_shape = (taps - 1, rows // seq_rows, dff)
        keep_shape = (tn // LANES, tm, LANES)
    wt = lambda off: pl.BlockSpec((d, tn), lambda j, i: (0, j + off))
    cw = lambda off: pl.BlockSpec((taps, tn), lambda j, i: (0, j + off))
    cb = lambda off: pl.BlockSpec((1, tn), lambda j, i: (0, j + off))
    out_specs = [pl.BlockSpec((tm, tn), lambda j, i: (i, j)), tail, tail]
    out_shape = [jax.ShapeDtypeStruct((rows, dff), BF16), jax.ShapeDtypeStruct(tail_shape, F32),
                 jax.ShapeDtypeStruct(tail_shape, F32)]
    if rounding:
        w_gate, w_val, w_specs = w_up, w_up, [wt(0), wt(nj)]
        out_specs += [wt(0), wt(0)]
        out_shape += [jax.ShapeDtypeStruct((d, dff), BF16)] * 2
    else:
        (w_gate, w_val), w_specs = w_up, [wt(0), wt(0)]
    outs = pl.pallas_call(
        functools.partial(_up_conv_gate_body, seq_rows=seq_rows, tm=tm, taps=taps, rounding=rounding),
        grid=(nj, rows // tm),
        in_specs=[pl.BlockSpec((tm, d), lambda j, i: (i, 0))] + w_specs + [st(0), st(nj),
                                                                             cw(0), cw(nj), cb(0), cb(nj)],
        out_specs=out_specs,
        out_shape=out_shape,
        scratch_shapes=[pltpu.VMEM(keep_shape, F32), pltpu.VMEM(keep_shape, F32)],
        compiler_params=_params("parallel", "arbitrary"),
        name="up_conv_gate",
    )(x, w_gate, w_val, state_rows, state_rows, conv_w, conv_w, conv_b.reshape(1, f2), conv_b.reshape(1, f2))
    act, tail_g, tail_v = outs[:3]
    w_bf = tuple(outs[3:]) if rounding else w_up
    tail = jnp.concatenate([tail_g, tail_v], axis=-1)
    if seq_rows >= tm:
        return act, tail.reshape(rows // seq_rows, SUBLANES, f2)[:, SUBLANES - (taps - 1):], w_bf
    return act, jnp.swapaxes(tail, 0, 1), w_bf


def _layer(x, shift_prev, wkv0, mem_k, mem_v, conv_prev, prm, want_v_rows):
    batch, seq_len, d = x.shape
    rows = batch * seq_len
    w = prm["width"]
    hd = prm["head_dim"]
    x2 = x.reshape(rows, d)

    xn = _rmsnorm(x2, prm["g_pre_mix"], BF16)
    n_shift = prm["mu_shift"].shape[1]
    rw = _matmul_cached(xn, prm, "w_in", 0, n_shift)
    guvq = _matmul_cached(xn, prm, "w_in", n_shift, 3 * w)
    new_shift = rw.reshape(batch, seq_len, -1)[:, -1]

    r, lw, k, v, a, b, g, bonus = _rwkv_prep(rw, _pad_state_rows(shift_prev[:, None, :]), seq_len, prm)
    y, h_new = _wkv(r, lw, k, v, a, b, jnp.swapaxes(wkv0, -1, -2), batch, seq_len, hd)
    new_wkv = jnp.swapaxes(h_new, -1, -2)
    a_out = _rwkv_post(y, bonus, g, prm, hd)

    if seq_len % prm["sgu_chunk"] == 0:
        b_out, v_rows = _sgu(guvq, prm["w_s"], prm["sgu_bias"], prm, None, want_v_rows)
    else:
        assert seq_len == SUBLANES
        b_out, v_rows = _sgu(guvq, prm["w_s"], prm["sgu_bias"][:, :SUBLANES], prm, SUBLANES, want_v_rows)
    c_out = _xattn(guvq, mem_k, mem_v, batch, seq_len, prm["xattn_heads"])

    mix, (prm["w_gates"], prm["w_branch"]) = _gate_branch_mix(
        xn, a_out, b_out, c_out, prm["w_gates"], n_shift + 3 * w, prm["w_branch"])
    h, hn = _mm_res_norm(mix, prm["w_out"], x2, prm["g_post_mix"], prm["g_pre_ffn"])

    act, conv_new, prm["w_up"] = _up_conv_gate(hn, prm["w_up"], conv_prev, prm["conv_w"],
                                               prm["conv_b"], seq_len)
    y_out = _mm_res_norm(act, prm["w_down"], h, prm["g_post_ffn"])
    if want_v_rows:
        v_rows = v_rows.reshape(batch, seq_len, w)
    return y_out.reshape(batch, seq_len, d), new_shift, new_wkv, v_rows, conv_new


def _prepare(l, g_pre_mix, w_in, mu_shift, w0, w_decay, a0, w_aaa, w_gate, k_k, k_a, r_k, lnx_w, lnx_b,
             sgu_g, sgu_b, w_s, b_s, w_branch, w_out, g_post_mix, g_pre_ffn, w_up, conv_w, conv_b, w_down,
             g_post_ffn):
    heads, hd = r_k.shape[1], r_k.shape[2]
    w = heads * hd
    d = w_in.shape[1]
    n_shift = mu_shift.shape[1]
    rank_d, rank_a, rank_g = w_decay.shape[1], w_aaa.shape[1], w_gate.shape[1]
    assert rank_d + rank_a == LANES and rank_g == LANES and n_shift == 3 * w + 2 * LANES
    groups, sgu_chunk, _ = w_s.shape[1:]
    row = lambda t: t[l].reshape(1, -1)
    lane_head = jnp.arange(LANES) // hd
    gd = w // groups
    return dict(
        width=w, head_dim=hd, sgu_chunk=sgu_chunk,
        g_pre_mix=g_pre_mix[l],
        w_in=w_in[l], w_gates=w_in[l],
        mu_shift=row(mu_shift), w0=row(w0), a0=row(a0),
        wd_pad=jnp.pad(w_decay[l], ((0, rank_a), (0, 0))),
        wa_pad=jnp.pad(w_aaa[l], ((rank_d, 0), (0, 0))),
        w_gate=w_gate[l],
        k_k=row(k_k), k_a=row(k_a), r_k=row(r_k), lnx_w=row(lnx_w), lnx_b=row(lnx_b),
        ones2=(lane_head[:, None] == lane_head[None, :]).astype(BF16),
        sgu_g=row(sgu_g), sgu_b=row(sgu_b),
        w_s=w_s[l],
        sgu_bias=jnp.broadcast_to(b_s[l][:, :, None], (groups, sgu_chunk, gd)),
        w_branch=w_branch[l], w_out=w_out[l].astype(BF16),
        g_post_mix=g_post_mix[l], g_pre_ffn=g_pre_ffn[l],
        w_up=w_up[l], conv_w=conv_w[l], conv_b=conv_b[l], w_down=w_down[l].astype(BF16),
        g_post_ffn=g_post_ffn[l],
    )


def kernel(x_prompt, x_sample, mem_prompt, state_wkv, state_shift, cache_mem_k, cache_mem_v, state_ffn_conv, g_pre_mix, w_in, mu_shift, w0, w_decay, a0, w_aaa, w_gate, k_k, k_a, r_k, lnx_w, lnx_b, sgu_g, sgu_b, w_s, b_s, g_mem, w_mem_k, w_mem_v, w_branch, w_out, g_post_mix, g_pre_ffn, w_up, conv_w, conv_b, w_down, g_post_ffn):
    depth = w_in.shape[0]
    batch = x_prompt.shape[0]
    mem_len, d = mem_prompt.shape[1], mem_prompt.shape[2]
    xh, xhd = cache_mem_k.shape[3], cache_mem_k.shape[4]
    heads, hd = r_k.shape[1], r_k.shape[2]
    n_shift = mu_shift.shape[1]
    f2 = w_up.shape[2]
    taps = conv_w.shape[1]
    y_p, y_s = x_prompt, x_sample
    outs = [[] for _ in range(9)]
    for l in range(depth):
        prm = _prepare(l, g_pre_mix, w_in, mu_shift, w0, w_decay, a0, w_aaa, w_gate, k_k, k_a, r_k, lnx_w, lnx_b,
                       sgu_g, sgu_b, w_s, b_s, w_branch, w_out, g_post_mix, g_pre_ffn, w_up, conv_w, conv_b,
                       w_down, g_post_ffn)
        mn = _rmsnorm(mem_prompt.reshape(batch * mem_len, d), g_mem[l], BF16)
        prm["xattn_heads"] = xh
        mk_rows = _matmul(mn, w_mem_k[l])[0]
        mv_rows = _matmul(mn, w_mem_v[l])[0]
        mk_p = mk_rows.reshape(batch, mem_len, xh, xhd)
        mv_p = mv_rows.reshape(batch, mem_len, xh, xhd)
        y_p, sh_p, wkv_p, _, cv_p = _layer(
            y_p, jnp.zeros((batch, n_shift), F32), jnp.zeros((batch, heads, hd, hd), F32), mk_rows, mv_rows,
            jnp.zeros((batch, taps - 1, f2), F32), prm, False)
        y_s, sh_s, wkv_s, vr_s, cv_s = _layer(y_s, state_shift[l], state_wkv[l], cache_mem_k[l], cache_mem_v[l],
                                              state_ffn_conv[l], prm, True)
        for lst, val in zip(outs, (wkv_p, sh_p, mk_p, mv_p, cv_p, wkv_s, sh_s, vr_s, cv_s)):
            lst.append(val)
    return (y_p, y_s) + tuple(jnp.stack(lst) for lst in outs)
```

```python
import functools

import jax
import jax.numpy as jnp
from jax import lax
from jax.experimental import pallas as pl
from jax.experimental.pallas import tpu as pltpu

F32 = jnp.float32
BF16 = jnp.bfloat16
HIGHEST = lax.Precision.HIGHEST

LANES = 128
SUBLANES = 8
VMEM_LIMIT_BYTES = 56 * 1024 * 1024

RMS_EPS = 1e-6
LN_EPS = 1e-5
GN_EPS = 64e-5
KK_EPS = 1e-12
RWKV_CHUNK = 64


def _params(*semantics):
    return pltpu.CompilerParams(dimension_semantics=semantics, vmem_limit_bytes=VMEM_LIMIT_BYTES)


def _row_tile(rows, pref):
    t = min(rows, pref)
    while rows % t:
        t -= SUBLANES
    return t


def _col_tile(n, cap):
    best = n
    for t in range(LANES, min(n, cap) + 1, LANES):
        if n % t == 0:
            best = t
    return best


def _bdot(a, b):
    return jnp.dot(a.astype(BF16), b.astype(BF16), preferred_element_type=F32)


def _bdot_nt(a, b):
    return lax.dot_general(a.astype(BF16), b.astype(BF16), (((1,), (1,)), ((), ())),
                           preferred_element_type=F32)


def _fdot(a, b):
    return jnp.dot(a, b, precision=HIGHEST, preferred_element_type=F32)


def _dot3(a, b):
    a_hi = a.astype(BF16)
    b_hi = b.astype(BF16)
    a_lo = (a - a_hi.astype(F32)).astype(BF16)
    b_lo = (b - b_hi.astype(F32)).astype(BF16)
    dot = lambda x, y: jnp.dot(x, y, preferred_element_type=F32)
    return dot(a_hi, b_hi) + (dot(a_hi, b_lo) + dot(a_lo, b_hi))


def _rmsnorm_body(x_ref, g_ref, o_ref):
    x = x_ref[...]
    y = x * lax.rsqrt(jnp.mean(x * x, axis=-1, keepdims=True) + RMS_EPS)
    o_ref[...] = (y * g_ref[...]).astype(o_ref.dtype)


def _rmsnorm(x, g, out_dtype):
    rows, d = x.shape
    tm = _row_tile(rows, 512)
    return pl.pallas_call(
        _rmsnorm_body,
        grid=(rows // tm,),
        in_specs=[pl.BlockSpec((tm, d), lambda i: (i, 0)), pl.BlockSpec((1, d), lambda i: (0, 0))],
        out_specs=pl.BlockSpec((tm, d), lambda i: (i, 0)),
        out_shape=jax.ShapeDtypeStruct((rows, d), out_dtype),
        compiler_params=_params("parallel"),
        name="rmsnorm",
    )(x, g.reshape(1, d))


def _matmul_round_body(x_ref, w_ref, o_ref, wb_ref):
    @pl.when(pl.program_id(1) == 0)
    def _():
        wb_ref[...] = w_ref[...].astype(BF16)

    o_ref[...] = jnp.dot(x_ref[...], wb_ref[...], preferred_element_type=F32).astype(o_ref.dtype)


def _matmul_body(x_ref, w_ref, o_ref):
    o_ref[...] = jnp.dot(x_ref[...], w_ref[...], preferred_element_type=F32).astype(o_ref.dtype)


def _matmul(x, w, col_start=0, n_cols=None, out_dtype=F32):
    rows, k = x.shape
    n = w.shape[1] - col_start if n_cols is None else n_cols
    assert col_start % LANES == 0 and n % LANES == 0
    rounding = w.dtype != BF16
    tn = _col_tile(n, 1792)
    budget = VMEM_LIMIT_BYTES - (4 << 20)
    tm = _row_tile(rows, 1024)
    w_bytes = 2 * k * tn * (4 + 2) if rounding else 2 * k * tn * 2
    while w_bytes + 2 * tm * k * 2 + 2 * tm * tn * 4 > budget and tm % (2 * SUBLANES) == 0:
        tm //= 2
    w_spec = pl.BlockSpec((pl.Element(k), pl.Element(tn)),
                          lambda j, i: (0, pl.multiple_of(col_start + j * tn, LANES)))
    out_spec = pl.BlockSpec((tm, tn), lambda j, i: (i, j))
    out_shape = jax.ShapeDtypeStruct((rows, n), out_dtype)
    if rounding:
        out_spec = [out_spec, pl.BlockSpec((k, tn), lambda j, i: (0, j))]
        out_shape = [out_shape, jax.ShapeDtypeStruct((k, n), BF16)]
    return pl.pallas_call(
        _matmul_round_body if rounding else _matmul_body,
        grid=(n // tn, rows // tm),
        in_specs=[pl.BlockSpec((tm, k), lambda j, i: (i, 0)), w_spec],
        out_specs=out_spec,
        out_shape=out_shape,
        compiler_params=_params("parallel", "arbitrary"),
        name="matmul",
    )(x, w)


def _matmul_cached(x, prm, name, col_start=0, n_cols=None):
    key = (name, col_start)
    if key in prm:
        return _matmul(x, prm[key])
    out, prm[key] = _matmul(x, prm[name], col_start, n_cols)
    return out


def _shift_rows(x, halo, k):
    rolled = pltpu.roll(x, k, 0)
    row = lax.broadcasted_iota(jnp.int32, (SUBLANES, x.shape[1]), 0)
    top = rolled[0:SUBLANES]
    for j in range(k):
        top = jnp.where(row == j, halo[SUBLANES - k + j:SUBLANES - k + j + 1, :], top)
    return jnp.concatenate([top, rolled[SUBLANES:]], axis=0)


def _head_sum(x, ones_ref):
    ones = ones_ref[...]
    hi = x.astype(BF16)
    lo = (x - hi.astype(F32)).astype(BF16)
    outs = []
    for j in range(x.shape[1] // LANES):
        sl = slice(j * LANES, (j + 1) * LANES)
        outs.append(jnp.dot(hi[:, sl], ones, preferred_element_type=F32)
                    + jnp.dot(lo[:, sl], ones, preferred_element_type=F32))
    return jnp.concatenate(outs, axis=1)


def _pad_state_rows(state):
    b, n, f = state.shape
    return jnp.pad(state, ((0, 0), (0, SUBLANES - n), (0, 0))).reshape(b * SUBLANES, f)


def _rwkv_prep_body(rw_ref, halo_ref, sp_ref, mu_ref, w0_ref, wd_ref, a0_ref, wa_ref, wg_ref,
                    kk_ref, ka_ref, rk_ref, ones_ref,
                    r_o, lw_o, k_o, v_o, a_o, b_o, g_o, bonus_o, *, seq_rows, tm, width):
    x = rw_ref[...]
    if seq_rows >= tm:
        is_start = (pl.program_id(0) * tm) % seq_rows == 0
        halo = jnp.where(is_start, pltpu.roll(sp_ref[...], SUBLANES - 1, 0), halo_ref[...])
        prev = _shift_rows(x, halo, 1)
    else:
        t = lax.broadcasted_iota(jnp.int32, x.shape, 0) & (seq_rows - 1)
        prev = jnp.where(t == 0, sp_ref[...], pltpu.roll(x, 1, 0))
    xs = x + mu_ref[...] * (prev - x)

    w = width
    r = xs[:, 0:w]
    k = xs[:, w:2 * w]
    v = xs[:, 2 * w:3 * w]
    x_wa = xs[:, 3 * w:3 * w + LANES]
    x_g = xs[:, 3 * w + LANES:3 * w + 2 * LANES]

    z = -(w0_ref[...] + _dot3(jnp.tanh(x_wa), wd_ref[...]))
    softplus = jnp.maximum(z, 0.0) + jnp.log(1.0 + jnp.exp(-jnp.abs(z)))
    lw = -jnp.exp(-softplus - 0.5)
    a_lr = jax.nn.sigmoid(a0_ref[...] + _bdot(x_wa, wa_ref[...]))
    g = _bdot(jax.nn.sigmoid(x_g), wg_ref[...])

    kk = k * kk_ref[...]
    k_mod = k * (1.0 + (a_lr - 1.0) * ka_ref[...])
    kk = kk / jnp.maximum(jnp.sqrt(_head_sum(kk * kk, ones_ref)), KK_EPS)
    bonus = _head_sum(r * k_mod * rk_ref[...], ones_ref) * v

    r_o[...] = r
    lw_o[...] = lw
    k_o[...] = k_mod
    v_o[...] = v.astype(v_o.dtype)
    a_o[...] = -kk
    b_o[...] = kk * a_lr
    g_o[...] = g
    bonus_o[...] = bonus


def _rwkv_prep(rw, shift_rows, seq_rows, prm):
    rows = rw.shape[0]
    n_shift = prm["mu_shift"].shape[1]
    w = prm["width"]
    tm = _row_tile(rows, 256)
    if seq_rows >= tm:
        assert seq_rows % tm == 0
        sp_spec = pl.BlockSpec((SUBLANES, n_shift), lambda i: ((i * tm) // seq_rows, 0))
    else:
        assert seq_rows == SUBLANES and tm % seq_rows == 0
        sp_spec = pl.BlockSpec((tm, n_shift), lambda i: (i, 0))
    row_spec = lambda c: pl.BlockSpec((1, c), lambda i: (0, 0))
    full = lambda a: pl.BlockSpec(a.shape, lambda i: (0, 0))
    out_spec = pl.BlockSpec((tm, w), lambda i: (i, 0))
    return pl.pallas_call(
        functools.partial(_rwkv_prep_body, seq_rows=seq_rows, tm=tm, width=w),
        grid=(rows // tm,),
        in_specs=[pl.BlockSpec((tm, n_shift), lambda i: (i, 0)),
                  pl.BlockSpec((SUBLANES, n_shift), lambda i: (jnp.maximum(i * (tm // SUBLANES) - 1, 0), 0)),
                  sp_spec, row_spec(n_shift), row_spec(w), full(prm["wd_pad"]), row_spec(w), full(prm["wa_pad"]),
                  full(prm["w_gate"]), row_spec(w), row_spec(w), row_spec(w), full(prm["ones2"])],
        out_specs=[out_spec] * 8,
        out_shape=[jax.ShapeDtypeStruct((rows, w), BF16 if i == 3 else F32) for i in range(8)],
        compiler_params=_params("parallel"),
        name="rwkv_prep",
    )(rw, rw, shift_rows, prm["mu_shift"], prm["w0"], prm["wd_pad"], prm["a0"], prm["wa_pad"], prm["w_gate"],
      prm["k_k"], prm["k_a"], prm["r_k"], prm["ones2"])


def _wkv_body(r_ref, lw_ref, k_ref, v_ref, a_ref, b_ref, h0_ref, y_ref, hout_ref, h_scr,
              *, chunk, pairs, seqs, width, hd):
    c = pl.program_id(1)
    n_pairs = width // LANES

    @pl.when(c == 0)
    def _():
        zero = jnp.zeros((hd, hd), F32)
        for i in range(seqs):
            for p in range(n_pairs):
                h_scr[i, p] = jnp.concatenate(
                    [jnp.concatenate([h0_ref[i, 2 * p], zero], axis=1),
                     jnp.concatenate([zero, h0_ref[i, 2 * p + 1]], axis=1)], axis=0)

    cs = chunk
    lanes = LANES * pairs
    heads = 2 * pairs
    n = heads * cs
    log_c = cs.bit_length() - 1
    log_hd = hd.bit_length() - 1
    n_stacks = width // lanes

    tri = (lax.broadcasted_iota(jnp.int32, (cs, cs), 0) >= lax.broadcasted_iota(jnp.int32, (cs, cs), 1)).astype(F32)
    row = lax.broadcasted_iota(jnp.int32, (n, lanes), 0)
    lane = lax.broadcasted_iota(jnp.int32, (n, lanes), 1)
    head_mask = (row >> log_c) == (lane >> log_hd)
    ri = lax.broadcasted_iota(jnp.int32, (n, n), 0)
    ci = lax.broadcasted_iota(jnp.int32, (n, n), 1)
    same_head = (ri >> log_c) == (ci >> log_c)
    rt = ri & (cs - 1)
    ct = ci & (cs - 1)
    strict = same_head & (ct < rt)
    incl = same_head & (ct <= rt)
    eye = (ri == ci).astype(F32)
    diag = (lax.broadcasted_iota(jnp.int32, (LANES, LANES), 0)
            == lax.broadcasted_iota(jnp.int32, (LANES, LANES), 1))

    def stack(x):
        return jnp.where(head_mask, jnp.concatenate([x] * heads, axis=0), 0.0)

    def fold(x):
        out = x[0:cs]
        for hidx in range(1, heads):
            out = out + x[hidx * cs:(hidx + 1) * cs]
        return out

    scaled = []
    for i in range(seqs):
        rs = slice(i * cs, (i + 1) * cs)
        lw = lw_ref[rs, :]
        cum = _fdot(tri, lw)
        total = cum[cs - 1:cs]
        e_cum = jnp.exp(cum)
        e_neg = jnp.exp(-cum)
        e_rest = jnp.exp(total - cum)
        b = b_ref[rs, :]
        k = k_ref[rs, :]
        scaled.append(dict(a=a_ref[rs, :] * jnp.exp(cum - lw), r=r_ref[rs, :] * e_cum, b=b * e_neg, k=k * e_neg,
                           v=v_ref[rs, :].astype(F32), bh=b * e_rest, kh=k * e_rest, total=total))

    inst = [(i, s) for i in range(seqs) for s in range(n_stacks)]
    jj = range(len(inst))

    def stacked(name):
        return [stack(scaled[i][name][:, s * lanes:(s + 1) * lanes]) for i, s in inst]

    a_st, r_st, b_st, k_st, v_st, bh_st, kh_st = (stacked(nm) for nm in ("a", "r", "b", "k", "v", "bh", "kh"))

    gram = [_bdot_nt(jnp.concatenate([a_st[j], r_st[j]], axis=0), jnp.concatenate([b_st[j], k_st[j]], axis=0))
            for j in jj]
    l_ab = [jnp.where(strict, gram[j][:n, :n], 0.0) for j in jj]
    m_rb = [jnp.where(incl, gram[j][n:, :n], 0.0) for j in jj]
    l_ak_m_rk = [jnp.concatenate([jnp.where(strict, gram[j][:n, n:], 0.0), jnp.where(incl, gram[j][n:, n:], 0.0)],
                                 axis=0) for j in jj]

    t_inv = [eye + l_ab[j] for j in jj]
    x_pow = [_bdot(l_ab[j], l_ab[j]) for j in jj]
    for _ in range(log_c - 2):
        both = [_bdot(jnp.concatenate([x_pow[j], t_inv[j]], axis=0), x_pow[j]) for j in jj]
        t_inv = [t_inv[j] + both[j][n:] for j in jj]
        x_pow = [both[j][:n] for j in jj]
    t_inv = [t_inv[j] + _bdot(t_inv[j], x_pow[j]) for j in jj]

    lv = [_bdot(l_ak_m_rk[j], v_st[j]) for j in jj]
    aw = [_bdot(t_inv[j], jnp.concatenate([a_st[j], lv[j][:n]], axis=1)) for j in jj]
    rb = [_bdot(m_rb[j], aw[j]) for j in jj]
    r_bar = [fold(r_st[j] + rb[j][:, :lanes]) for j in jj]
    y0 = [fold(rb[j][:, lanes:] + lv[j][n:]) for j in jj]

    jp = [(j, p) for j in jj for p in range(pairs)]

    def pair_lanes(x, p, off=0):
        return x[:, off + p * LANES:off + (p + 1) * LANES]

    h = [h_scr[inst[j][0], inst[j][1] * pairs + p] for j, p in jp]
    yv = [_bdot(pair_lanes(r_bar[j], p), h[q]) + pair_lanes(y0[j], p) for q, (j, p) in enumerate(jp)]
    for q, (j, p) in enumerate(jp):
        i, s = inst[j]
        y_ref[i * cs:(i + 1) * cs, s * lanes + p * LANES:s * lanes + (p + 1) * LANES] = yv[q]
    bh_t = [pair_lanes(bh_st[j], p).T for j, p in jp]
    kh_t = [pair_lanes(kh_st[j], p).T for j, p in jp]
    pp = [_bdot(bh_t[q], jnp.concatenate([pair_lanes(aw[j], p), pair_lanes(aw[j], p, lanes)], axis=1))
          for q, (j, p) in enumerate(jp)]
    kv = [_bdot(kh_t[q], pair_lanes(v_st[j], p)) for q, (j, p) in enumerate(jp)]
    for q, (j, p) in enumerate(jp):
        i, s = inst[j]
        decay = pair_lanes(scaled[i]["total"], p, s * lanes)
        phi = jnp.where(diag, jnp.exp(decay), 0.0) + pp[q][:, :LANES]
        h_scr[i, s * pairs + p] = _bdot(phi, h[q]) + (pp[q][:, LANES:] + kv[q])

    @pl.when(c == pl.num_programs(1) - 1)
    def _():
        for i in range(seqs):
            for p in range(n_pairs):
                both = h_scr[i, p]
                hout_ref[i, 2 * p] = both[:hd, :hd]
                hout_ref[i, 2 * p + 1] = both[hd:, hd:]


def _wkv(r, lw, k, v, a, b, h0, batch, seq_len, hd):
    rows, w = r.shape
    chunk = min(seq_len, RWKV_CHUNK)
    assert seq_len % chunk == 0 and chunk % SUBLANES == 0 and LANES % chunk == 0
    pairs = LANES // (2 * chunk)
    n_chunks = seq_len // chunk
    n_pairs = w // LANES
    seqs = 1
    while n_chunks == 1 and seqs < 4 and batch % (2 * seqs) == 0:
        seqs *= 2
    vec = pl.BlockSpec((seqs * chunk, w), lambda bi, ci: (bi * n_chunks + ci, 0))
    st = pl.BlockSpec((seqs, w // hd, hd, hd), lambda bi, ci: (bi, 0, 0, 0))
    return pl.pallas_call(
        functools.partial(_wkv_body, chunk=chunk, pairs=pairs, seqs=seqs, width=w, hd=hd),
        grid=(batch // seqs, n_chunks),
        in_specs=[vec] * 6 + [st],
        out_specs=[vec, st],
        out_shape=[jax.ShapeDtypeStruct((rows, w), F32),
                   jax.ShapeDtypeStruct((batch, w // hd, hd, hd), F32)],
        scratch_shapes=[pltpu.VMEM((seqs, n_pairs, LANES, LANES), F32)],
        compiler_params=_params("parallel", "arbitrary"),
        name="rwkv_chunks",
    )(r, lw, k, v, a, b, h0)


def _rwkv_post_body(y_ref, bonus_ref, g_ref, lw_ref, lb_ref, ones_ref, o_ref, *, hd):
    y = y_ref[...]
    mu = _head_sum(y, ones_ref) * (1.0 / hd)
    d = y - mu
    var = _head_sum(d * d, ones_ref) * (1.0 / hd)
    yn = d * lax.rsqrt(var + GN_EPS) * lw_ref[...] + lb_ref[...]
    o_ref[...] = ((yn + bonus_ref[...]) * g_ref[...]).astype(o_ref.dtype)


def _rwkv_post(y, bonus, g, prm, hd):
    rows, w = y.shape
    tm = _row_tile(rows, 512)
    blk = pl.BlockSpec((tm, w), lambda i: (i, 0))
    row_spec = pl.BlockSpec((1, w), lambda i: (0, 0))
    return pl.pallas_call(
        functools.partial(_rwkv_post_body, hd=hd),
        grid=(rows // tm,),
        in_specs=[blk, blk, blk, row_spec, row_spec, pl.BlockSpec((LANES, LANES), lambda i: (0, 0))],
        out_specs=blk,
        out_shape=jax.ShapeDtypeStruct((rows, w), BF16),
        compiler_params=_params("parallel"),
        name="rwkv_post",
    )(y, bonus, g, prm["lnx_w"], prm["lnx_b"], prm["ones2"])


def _sgu_body(gu_ref, gv_ref, sg_ref, sb_ref, wm_ref, bias_ref, o_ref, *maybe_v, groups, rows, seq_rows):
    u = jax.nn.gelu(gu_ref[...])
    vf = jax.nn.gelu(gv_ref[...])
    mu = jnp.mean(vf, axis=-1, keepdims=True)
    d = vf - mu
    var = jnp.mean(d * d, axis=-1, keepdims=True)
    v = (d * lax.rsqrt(var + LN_EPS)) * sg_ref[...] + sb_ref[...]
    if maybe_v:
        maybe_v[0][...] = v
    gd = v.shape[1] // groups
    ri = lax.broadcasted_iota(jnp.int32, (rows, rows), 0)
    ci = lax.broadcasted_iota(jnp.int32, (rows, rows), 1)
    causal = ri >= ci
    if seq_rows is not None:
        shift = seq_rows.bit_length() - 1
        causal = causal & ((ri >> shift) == (ci >> shift))
        pos = ((lax.broadcasted_iota(jnp.int32, (rows, seq_rows), 0) & (seq_rows - 1))
               == lax.broadcasted_iota(jnp.int32, (rows, seq_rows), 1)).astype(F32)
    for g in range(groups):
        sl = slice(g * gd, (g + 1) * gd)
        if seq_rows is None:
            w_full, bias = wm_ref[g], bias_ref[g]
        else:
            w_full = lax.dot_general(_fdot(pos, wm_ref[g][:seq_rows, :seq_rows]), pos, (((1,), (1,)), ((), ())),
                                     precision=HIGHEST, preferred_element_type=F32)
            bias = _fdot(pos, bias_ref[g])
        w_causal = jnp.where(causal, w_full, 0.0).astype(BF16)
        for blk in range(v.shape[0] // rows):
            rs = slice(blk * rows, (blk + 1) * rows)
            s = _bdot(w_causal, v[rs, sl]) + bias
            o_ref[rs, sl] = (u[rs, sl] * s).astype(o_ref.dtype)


def _col_window(rows_per_step, width, col0, row_of):
    return pl.BlockSpec((pl.Element(rows_per_step), pl.Element(width)),
                        lambda *idx: (pl.multiple_of(row_of(*idx) * rows_per_step, SUBLANES), col0))


def _sgu(proj, gu_col, w_mix, bias, prm, seq_rows, want_v_rows):
    rows = proj.shape[0]
    w = prm["width"]
    groups, chunk, _ = w_mix.shape
    mix_rows = chunk if seq_rows is None else _row_tile(rows, 2 * LANES)
    cs = mix_rows
    while cs < 512 and rows % (2 * cs) == 0:
        cs *= 2
    row_spec = pl.BlockSpec((1, w), lambda i: (0, 0))
    out = pl.BlockSpec((cs, w), lambda i: (i, 0))
    outs = pl.pallas_call(
        functools.partial(_sgu_body, groups=groups, rows=mix_rows, seq_rows=seq_rows),
        grid=(rows // cs,),
        in_specs=[_col_window(cs, w, gu_col, lambda i: i), _col_window(cs, w, gu_col + w, lambda i: i),
                  row_spec, row_spec,
                  pl.BlockSpec(w_mix.shape, lambda i: (0, 0, 0)), pl.BlockSpec(bias.shape, lambda i: (0, 0, 0))],
        out_specs=[out, out] if want_v_rows else [out],
        out_shape=[jax.ShapeDtypeStruct((rows, w), BF16)] + ([jax.ShapeDtypeStruct((rows, w), F32)]
                                                           if want_v_rows else []),
        compiler_params=_params("parallel"),
        name="sgu",
    )(proj, proj, prm["sgu_g"], prm["sgu_b"], w_mix, bias)
    return (outs[0], outs[1]) if want_v_rows else (outs[0], None)


def _xattn_body(q_ref, k_ref, v_ref, o_ref, *, heads, seqs, tq, m):
    hd = q_ref.shape[1] // heads
    scale = hd ** -0.5
    cases = [(h, i) for h in range(heads) for i in range(seqs)]
    cols = lambda h: slice(h * hd, (h + 1) * hd)
    mem = lambda i: slice(i * m, (i + 1) * m)
    s = [_bdot_nt(q_ref[i * tq:(i + 1) * tq, cols(h)], k_ref[mem(i), cols(h)]) * scale for h, i in cases]
    e = [jnp.exp(x - jnp.max(x, axis=-1, keepdims=True)) for x in s]
    p = [x / jnp.sum(x, axis=-1, keepdims=True) for x in e]
    o = [_bdot(p[c], v_ref[mem(i), cols(h)]) for c, (h, i) in enumerate(cases)]
    for h in range(heads):
        o_ref[:, cols(h)] = jnp.concatenate(o[h * seqs:(h + 1) * seqs], axis=0).astype(o_ref.dtype)


def _xattn_cache_body(q_ref, k_ref, v_ref, o_ref, *, heads, seqs, tq):
    hd = q_ref.shape[1] // heads
    m = k_ref.shape[1]
    n = m * heads
    scale = hd ** -0.5
    row_head = lax.broadcasted_iota(jnp.int32, (heads * tq, n), 0) >> (tq.bit_length() - 1)
    lane_head = lax.broadcasted_iota(jnp.int32, (heads * tq, n), 1) & (heads - 1)
    own = row_head == lane_head
    q = [jnp.concatenate([q_ref[i * tq:(i + 1) * tq, h * hd:(h + 1) * hd] for h in range(heads)], axis=0)
         for i in range(seqs)]
    s = [jnp.where(own, _bdot_nt(q[i], k_ref[i].reshape(n, hd)) * scale, -1e30) for i in range(seqs)]
    e = [jnp.exp(x - jnp.max(x, axis=-1, keepdims=True)) for x in s]
    p = [x / jnp.sum(x, axis=-1, keepdims=True) for x in e]
    o = [_bdot(p[i], v_ref[i].reshape(n, hd)) for i in range(seqs)]
    for h in range(heads):
        o_ref[:, h * hd:(h + 1) * hd] = jnp.concatenate(
            [o[i][h * tq:(h + 1) * tq] for i in range(seqs)], axis=0).astype(o_ref.dtype)


def _xattn(proj, q_col, w, mem_k, mem_v, batch, seq_len, heads):
    rows = proj.shape[0]
    tq = _row_tile(seq_len, 512)
    nq = seq_len // tq
    seqs = max(1, 32 // tq)
    assert batch % seqs == 0 and (seqs == 1 or nq == 1)
    if mem_k.ndim == 2:
        m = mem_k.shape[0] // batch
        kv = pl.BlockSpec((seqs * m, w), lambda bi, qi: (bi, 0))
        body = functools.partial(_xattn_body, heads=heads, seqs=seqs, tq=tq, m=m)
    else:
        assert tq & (tq - 1) == 0 and heads & (heads - 1) == 0
        kv = pl.BlockSpec((seqs,) + mem_k.shape[1:], lambda bi, qi: (bi, 0, 0, 0))
        body = functools.partial(_xattn_cache_body, heads=heads, seqs=seqs, tq=tq)
    return pl.pallas_call(
        body,
        grid=(batch // seqs, nq),
        in_specs=[_col_window(seqs * tq, w, q_col, lambda bi, qi: bi * nq + qi), kv, kv],
        out_specs=pl.BlockSpec((seqs * tq, w), lambda bi, qi: (bi * nq + qi, 0)),
        out_shape=jax.ShapeDtypeStruct((rows, w), BF16),
        compiler_params=_params("parallel", "parallel"),
        name="mem_xattn",
    )(proj, mem_k, mem_v)


def _gate_branch_body(*refs, rounding):
    if rounding:
        x_ref, a_ref, b_ref, c_ref, wg0_ref, wg1_ref, wg2_ref, wb_ref, o_ref, wg_bf, wb_bf = refs

        @pl.when(pl.program_id(1) == 0)
        def _():
            for n, src in enumerate((wg0_ref, wg1_ref, wg2_ref)):
                wg_bf[n] = src[...].astype(BF16)
            wb_bf[...] = wb_ref[...].astype(BF16)
    else:
        x_ref, a_ref, b_ref, c_ref, wg_bf, wb_bf, o_ref = refs

    x = x_ref[...]
    acc = None
    for n, br_ref in enumerate((a_ref, b_ref, c_ref)):
        gate = jax.nn.sigmoid(jnp.dot(x, wg_bf[n], preferred_element_type=F32))
        term = gate * jnp.dot(br_ref[...], wb_bf[n], preferred_element_type=F32)
        acc = term if acc is None else acc + term
    o_ref[...] = acc.astype(o_ref.dtype)


def _gate_branch_mix(xn, a_out, b_out, c_out, w_gates, gate_col0, w_branch):
    rows, k = xn.shape
    w = a_out.shape[1]
    nb, _, d = w_branch.shape
    assert nb == 3
    rounding = w_branch.dtype != BF16
    tm = _row_tile(rows, 1024)
    tn = _col_tile(d, 256)
    nj = d // tn
    act = lambda width: pl.BlockSpec((tm, width), lambda j, i: (i, 0))
    wb_spec = pl.BlockSpec((nb, w, tn), lambda j, i: (0, 0, j))
    wg_spec = pl.BlockSpec((nb, k, tn), lambda j, i: (0, 0, j))
    out_specs = [pl.BlockSpec((tm, tn), lambda j, i: (i, j))]
    out_shape = [jax.ShapeDtypeStruct((rows, d), BF16)]
    if rounding:
        gate_cols = lambda n: pl.BlockSpec(
            (pl.Element(k), pl.Element(tn)),
            lambda j, i: (0, pl.multiple_of(gate_col0 + n * d + j * tn, LANES)))
        w_specs = [gate_cols(0), gate_cols(1), gate_cols(2), wb_spec]
        w_args = (w_gates, w_gates, w_gates, w_branch)
        out_specs += [wg_spec, wb_spec]
        out_shape += [jax.ShapeDtypeStruct((nb, k, d), BF16), jax.ShapeDtypeStruct(w_branch.shape, BF16)]
    else:
        w_specs = [wg_spec, wb_spec]
        w_args = (w_gates, w_branch)
    outs = pl.pallas_call(
        functools.partial(_gate_branch_body, rounding=rounding),
        grid=(nj, rows // tm),
        in_specs=[act(k), act(w), act(w), act(w)] + w_specs,
        out_specs=out_specs,
        out_shape=out_shape,
        compiler_params=_params("parallel", "arbitrary"),
        name="gate_branch_mix",
    )(xn, a_out, b_out, c_out, *w_args)
    return outs[0], ((outs[1], outs[2]) if rounding else (w_gates, w_branch))


def _mm_res_norm_body(x_ref, w_ref, res_ref, g_ref, g2_ref, y_ref, *maybe_next, with_next):
    f = jnp.dot(x_ref[...], w_ref[...], preferred_element_type=F32)
    y = res_ref[...] + (f * lax.rsqrt(jnp.mean(f * f, axis=-1, keepdims=True) + RMS_EPS)) * g_ref[...]
    y_ref[...] = y
    if with_next:
        yn = y * lax.rsqrt(jnp.mean(y * y, axis=-1, keepdims=True) + RMS_EPS)
        maybe_next[0][...] = (yn * g2_ref[...]).astype(maybe_next[0].dtype)


def _mm_res_norm(x, w, res, g, g_next=None):
    rows, k = x.shape
    d = w.shape[1]
    with_next = g_next is not None
    tm = _row_tile(rows, 512 if k * d * 2 <= (12 << 20) else 256)
    blk = pl.BlockSpec((tm, d), lambda i: (i, 0))
    row_spec = pl.BlockSpec((1, d), lambda i: (0, 0))
    out_specs = [blk, blk] if with_next else [blk]
    out_shape = [jax.ShapeDtypeStruct((rows, d), F32)]
    if with_next:
        out_shape.append(jax.ShapeDtypeStruct((rows, d), BF16))
    g2 = g_next if with_next else g
    outs = pl.pallas_call(
        functools.partial(_mm_res_norm_body, with_next=with_next),
        grid=(rows // tm,),
        in_specs=[pl.BlockSpec((tm, k), lambda i: (i, 0)),
                  pl.BlockSpec((k, d), lambda i: (0, 0), pipeline_mode=pl.Buffered(1)),
                  blk, row_spec, row_spec],
        out_specs=out_specs,
        out_shape=out_shape,
        compiler_params=_params("arbitrary"),
        name="matmul_res_norm",
    )(x, w, res, g.reshape(1, d), g2.reshape(1, d))
    return outs if with_next else outs[0]


def _up_conv_gate_body(x_ref, wg_ref, wv_ref, eg_ref, ev_ref, cwg_ref, cwv_ref, cbg_ref, cbv_ref,
                       o_ref, tg_ref, tv_ref, *rest, seq_rows, tm, taps, rounding):
    i = pl.program_id(1)
    if rounding:
        wg_bf, wv_bf, keep_g, keep_v = rest
    else:
        keep_g, keep_v = rest
        wg_bf, wv_bf = wg_ref, wv_ref

    @pl.when(i == 0)
    def _():
        if rounding:
            wg_bf[...] = wg_ref[...].astype(BF16)
            wv_bf[...] = wv_ref[...].astype(BF16)
        keep_g[...] = jnp.zeros_like(keep_g)
        keep_v[...] = jnp.zeros_like(keep_v)

    if seq_rows >= tm:
        is_start = (i * tm) % seq_rows == 0
        tn = o_ref.shape[1]
        row8 = lax.broadcasted_iota(jnp.int32, (SUBLANES, tn), 0)

        def first_halo(e_ref, keep):
            start = jnp.zeros((SUBLANES, tn), F32)
            for idx in range(taps - 1):
                start = jnp.where(row8 == SUBLANES - (taps - 1) + idx, e_ref[0, idx:idx + 1, :], start)
            return jnp.where(is_start, start, keep[...])

        def conv_rows(u, halo, cw, cb):
            acc = cb + cw[taps - 1:taps] * u
            for back in range(1, taps):
                acc = acc + cw[taps - 1 - back:taps - back] * _shift_rows(u, halo, back)
            return acc

        halo_g, halo_v = first_halo(eg_ref, keep_g), first_halo(ev_ref, keep_v)
        cwg, cwv, cbg, cbv = cwg_ref[...], cwv_ref[...], cbg_ref[...], cbv_ref[...]
        xt = x_ref[...]
        ug = jnp.dot(xt, wg_bf[...], preferred_element_type=F32)
        uv = jnp.dot(xt, wv_bf[...], preferred_element_type=F32)
        o_ref[...] = (jax.nn.gelu(conv_rows(ug, halo_g, cwg, cbg)) * conv_rows(uv, halo_v, cwv, cbv)).astype(o_ref.dtype)
        for keep, tail_ref, u in ((keep_g, tg_ref, ug), (keep_v, tv_ref, uv)):
            keep[...] = u[tm - SUBLANES:tm]
            tail_ref[...] = u[tm - SUBLANES:tm]
        return

    x = x_ref[...]

    def conv(w_bf, e_ref, cw_ref, cb_ref, keep, tail_ref):
        u = jnp.dot(x, w_bf[...], preferred_element_type=F32)
        cw = cw_ref[...]
        acc = cb_ref[...] + cw[taps - 1:taps] * u
        n_lane_tiles = u.shape[1] // LANES
        t = lax.broadcasted_iota(jnp.int32, u.shape, 0) & (seq_rows - 1)
        for idx in range(taps - 1):
            for c in range(n_lane_tiles):
                keep[c, pl.ds(idx, tm // seq_rows, stride=seq_rows), :] = e_ref[:, idx, c * LANES:(c + 1) * LANES]
        e = jnp.concatenate([keep[c] for c in range(n_lane_tiles)], axis=1)
        for back in range(1, taps):
            up_by = taps - 1 - back
            state = pltpu.roll(e, tm - up_by, 0) if up_by else e
            prev = jnp.where(t < back, state, pltpu.roll(u, back, 0))
            acc = acc + cw[taps - 1 - back:taps - back] * prev
        for c in range(n_lane_tiles):
            keep[c] = u[:, c * LANES:(c + 1) * LANES]
        for idx in range(taps - 1):
            rows_t = pl.ds(seq_rows - (taps - 1) + idx, tm // seq_rows, stride=seq_rows)
            tail_ref[idx] = jnp.concatenate([keep[c, rows_t, :] for c in range(n_lane_tiles)], axis=1)
        return acc

    gate = conv(wg_bf, eg_ref, cwg_ref, cbg_ref, keep_g, tg_ref)
    val = conv(wv_bf, ev_ref, cwv_ref, cbv_ref, keep_v, tv_ref)
    o_ref[...] = (jax.nn.gelu(gate) * val).astype(o_ref.dtype)


def _up_conv_gate(x, w_up, state_rows, conv_w, conv_b, seq_rows):
    rows, d = x.shape
    rounding = not isinstance(w_up, tuple)
    f2 = conv_w.shape[1]
    dff = f2 // 2
    taps = conv_w.shape[0]
    tm = _row_tile(rows if seq_rows == SUBLANES else seq_rows, 1024)
    tn = _col_tile(dff, 512)
    nj = dff // tn
    if seq_rows >= tm:
        assert seq_rows % tm == 0
        st = lambda off: pl.BlockSpec((1, taps - 1, tn), lambda j, i: ((i * tm) // seq_rows, 0, j + off))
        tail = pl.BlockSpec((SUBLANES, tn), lambda j, i: ((i * tm) // seq_rows, j))
        tail_shape = ((rows // seq_rows) * SUBLANES, dff)
        keep_shape = (SUBLANES, tn)
    else:
        assert seq_rows == SUBLANES and tm % seq_rows == 0
        st = lambda off: pl.BlockSpec((tm // seq_rows, taps - 1, tn), lambda j, i: (i, 0, j + off))
        tail = pl.BlockSpec((taps - 1, tm // seq_rows, tn), lambda j, i: (0, i, j))
        tail_shape = (taps - 1, rows // seq_rows, dff)
        keep_shape = (tn // LANES, tm, LANES)
    wt = lambda off: pl.BlockSpec((d, tn), lambda j, i: (0, j + off))
    cw = lambda off: pl.BlockSpec((taps, tn), lambda j, i: (0, j + off))
    cb = lambda off: pl.BlockSpec((1, tn), lambda j, i: (0, j + off))
    out_specs = [pl.BlockSpec((tm, tn), lambda j, i: (i, j)), tail, tail]
    out_shape = [jax.ShapeDtypeStruct((rows, dff), BF16), jax.ShapeDtypeStruct(tail_shape, F32),
                 jax.ShapeDtypeStruct(tail_shape, F32)]
    if rounding:
        w_gate, w_val, w_specs = w_up, w_up, [wt(0), wt(nj)]
        out_specs += [wt(0), wt(0)]
        out_shape += [jax.ShapeDtypeStruct((d, dff), BF16)] * 2
    else:
        (w_gate, w_val), w_specs = w_up, [wt(0), wt(0)]
    outs = pl.pallas_call(
        functools.partial(_up_conv_gate_body, seq_rows=seq_rows, tm=tm, taps=taps, rounding=rounding),
        grid=(nj, rows // tm),
        in_specs=[pl.BlockSpec((tm, d), lambda j, i: (i, 0))] + w_specs + [st(0), st(nj),
                                                                             cw(0), cw(nj), cb(0), cb(nj)],
        out_specs=out_specs,
        out_shape=out_shape,
        scratch_shapes=[pltpu.VMEM(keep_shape, F32), pltpu.VMEM(keep_shape, F32)],
        compiler_params=_params("parallel", "arbitrary"),
        name="up_conv_gate",
    )(x, w_gate, w_val, state_rows, state_rows, conv_w, conv_w, conv_b.reshape(1, f2), conv_b.reshape(1, f2))
    act, tail_g, tail_v = outs[:3]
    w_bf = tuple(outs[3:]) if rounding else w_up
    tail = jnp.concatenate([tail_g, tail_v], axis=-1)
    if seq_rows >= tm:
        return act, tail.reshape(rows // seq_rows, SUBLANES, f2)[:, SUBLANES - (taps - 1):], w_bf
    return act, jnp.swapaxes(tail, 0, 1), w_bf


def _layer(x, shift_prev, wkv0, mem_k, mem_v, conv_prev, prm, want_v_rows):
    batch, seq_len, d = x.shape
    rows = batch * seq_len
    w = prm["width"]
    hd = prm["head_dim"]
    x2 = x.reshape(rows, d)

    xn = _rmsnorm(x2, prm["g_pre_mix"], BF16)
    n_shift = prm["mu_shift"].shape[1]
    proj = _matmul_cached(xn, prm, "w_in", 0, n_shift + 3 * w)
    new_shift = proj.reshape(batch, seq_len, -1)[:, -1, :n_shift]

    r, lw, k, v, a, b, g, bonus = _rwkv_prep(proj, _pad_state_rows(shift_prev[:, None, :]), seq_len, prm)
    y, h_new = _wkv(r, lw, k, v, a, b, jnp.swapaxes(wkv0, -1, -2), batch, seq_len, hd)
    new_wkv = jnp.swapaxes(h_new, -1, -2)
    a_out = _rwkv_post(y, bonus, g, prm, hd)

    if seq_len % prm["sgu_chunk"] == 0:
        b_out, v_rows = _sgu(proj, n_shift, prm["w_s"], prm["sgu_bias"], prm, None, want_v_rows)
    else:
        assert seq_len == SUBLANES
        b_out, v_rows = _sgu(proj, n_shift, prm["w_s"], prm["sgu_bias"][:, :SUBLANES], prm, SUBLANES,
                             want_v_rows)
    c_out = _xattn(proj, n_shift + 2 * w, w, mem_k, mem_v, batch, seq_len, prm["xattn_heads"])

    mix, (prm["w_gates"], prm["w_branch"]) = _gate_branch_mix(
        xn, a_out, b_out, c_out, prm["w_gates"], n_shift + 3 * w, prm["w_branch"])
    h, hn = _mm_res_norm(mix, prm["w_out"], x2, prm["g_post_mix"], prm["g_pre_ffn"])

    act, conv_new, prm["w_up"] = _up_conv_gate(hn, prm["w_up"], conv_prev, prm["conv_w"],
                                               prm["conv_b"], seq_len)
    y_out = _mm_res_norm(act, prm["w_down"], h, prm["g_post_ffn"])
    if want_v_rows:
        v_rows = v_rows.reshape(batch, seq_len, w)
    return y_out.reshape(batch, seq_len, d), new_shift, new_wkv, v_rows, conv_new


def _prepare(l, g_pre_mix, w_in, mu_shift, w0, w_decay, a0, w_aaa, w_gate, k_k, k_a, r_k, lnx_w, lnx_b,
             sgu_g, sgu_b, w_s, b_s, w_branch, w_out, g_post_mix, g_pre_ffn, w_up, conv_w, conv_b, w_down,
             g_post_ffn):
    heads, hd = r_k.shape[1], r_k.shape[2]
    w = heads * hd
    d = w_in.shape[1]
    n_shift = mu_shift.shape[1]
    rank_d, rank_a, rank_g = w_decay.shape[1], w_aaa.shape[1], w_gate.shape[1]
    assert rank_d + rank_a == LANES and rank_g == LANES and n_shift == 3 * w + 2 * LANES
    groups, sgu_chunk, _ = w_s.shape[1:]
    row = lambda t: t[l].reshape(1, -1)
    lane_head = jnp.arange(LANES) // hd
    gd = w // groups
    return dict(
        width=w, head_dim=hd, sgu_chunk=sgu_chunk,
        g_pre_mix=g_pre_mix[l],
        w_in=w_in[l], w_gates=w_in[l],
        mu_shift=row(mu_shift), w0=row(w0), a0=row(a0),
        wd_pad=jnp.pad(w_decay[l], ((0, rank_a), (0, 0))),
        wa_pad=jnp.pad(w_aaa[l], ((rank_d, 0), (0, 0))),
        w_gate=w_gate[l],
        k_k=row(k_k), k_a=row(k_a), r_k=row(r_k), lnx_w=row(lnx_w), lnx_b=row(lnx_b),
        ones2=(lane_head[:, None] == lane_head[None, :]).astype(BF16),
        sgu_g=row(sgu_g), sgu_b=row(sgu_b),
        w_s=w_s[l],
        sgu_bias=jnp.broadcast_to(b_s[l][:, :, None], (groups, sgu_chunk, gd)),
        w_branch=w_branch[l], w_out=w_out[l].astype(BF16),
        g_post_mix=g_post_mix[l], g_pre_ffn=g_pre_ffn[l],
        w_up=w_up[l], conv_w=conv_w[l], conv_b=conv_b[l], w_down=w_down[l].astype(BF16),
        g_post_ffn=g_post_ffn[l],
    )


def kernel(x_prompt, x_sample, mem_prompt, state_wkv, state_shift, cache_mem_k, cache_mem_v, state_ffn_conv, g_pre_mix, w_in, mu_shift, w0, w_decay, a0, w_aaa, w_gate, k_k, k_a, r_k, lnx_w, lnx_b, sgu_g, sgu_b, w_s, b_s, g_mem, w_mem_k, w_mem_v, w_branch, w_out, g_post_mix, g_pre_ffn, w_up, conv_w, conv_b, w_down, g_post_ffn):
    depth = w_in.shape[0]
    batch = x_prompt.shape[0]
    mem_len, d = mem_prompt.shape[1], mem_prompt.shape[2]
    xh, xhd = cache_mem_k.shape[3], cache_mem_k.shape[4]
    heads, hd = r_k.shape[1], r_k.shape[2]
    n_shift = mu_shift.shape[1]
    f2 = w_up.shape[2]
    taps = conv_w.shape[1]
    y_p, y_s = x_prompt, x_sample
    outs = [[] for _ in range(9)]
    for l in range(depth):
        prm = _prepare(l, g_pre_mix, w_in, mu_shift, w0, w_decay, a0, w_aaa, w_gate, k_k, k_a, r_k, lnx_w, lnx_b,
                       sgu_g, sgu_b, w_s, b_s, w_branch, w_out, g_post_mix, g_pre_ffn, w_up, conv_w, conv_b,
                       w_down, g_post_ffn)
        mn = _rmsnorm(mem_prompt.reshape(batch * mem_len, d), g_mem[l], BF16)
        prm["xattn_heads"] = xh
        mk_rows = _matmul(mn, w_mem_k[l])[0]
        mv_rows = _matmul(mn, w_mem_v[l])[0]
        mk_p = mk_rows.reshape(batch, mem_len, xh, xhd)
        mv_p = mv_rows.reshape(batch, mem_len, xh, xhd)
        y_p, sh_p, wkv_p, _, cv_p = _layer(
            y_p, jnp.zeros((batch, n_shift), F32), jnp.zeros((batch, heads, hd, hd), F32), mk_rows, mv_rows,
            jnp.zeros((batch, taps - 1, f2), F32), prm, False)
        y_s, sh_s, wkv_s, vr_s, cv_s = _layer(y_s, state_shift[l], state_wkv[l], cache_mem_k[l], cache_mem_v[l],
                                              state_ffn_conv[l], prm, True)
        for lst, val in zip(outs, (wkv_p, sh_p, mk_p, mv_p, cv_p, wkv_s, sh_s, vr_s, cv_s)):
            lst.append(val)
    return (y_p, y_s) + tuple(jnp.stack(lst) for lst in outs)
```

```python
import functools

import jax
import jax.numpy as jnp
from jax import lax
from jax.experimental import pallas as pl
from jax.experimental.pallas import tpu as pltpu

F32 = jnp.float32
BF16 = jnp.bfloat16
HIGHEST = lax.Precision.HIGHEST

LANES = 128
SUBLANES = 8
VMEM_LIMIT_BYTES = 56 * 1024 * 1024

RMS_EPS = 1e-6
LN_EPS = 1e-5
GN_EPS = 64e-5
KK_EPS = 1e-12
RWKV_CHUNK = 64


def _params(*semantics):
    return pltpu.CompilerParams(dimension_semantics=semantics, vmem_limit_bytes=VMEM_LIMIT_BYTES)


def _row_tile(rows, pref):
    t = min(rows, pref)
    while rows % t:
        t -= SUBLANES
    return t


def _col_tile(n, cap):
    best = n
    for t in range(LANES, min(n, cap) + 1, LANES):
        if n % t == 0:
            best = t
    return best


def _bdot(a, b):
    return jnp.dot(a.astype(BF16), b.astype(BF16), preferred_element_type=F32)


def _bdot_nt(a, b):
    return lax.dot_general(a.astype(BF16), b.astype(BF16), (((1,), (1,)), ((), ())),
                           preferred_element_type=F32)


def _fdot(a, b):
    return jnp.dot(a, b, precision=HIGHEST, preferred_element_type=F32)


def _dot3(a, b):
    a_hi = a.astype(BF16)
    b_hi = b.astype(BF16)
    a_lo = (a - a_hi.astype(F32)).astype(BF16)
    b_lo = (b - b_hi.astype(F32)).astype(BF16)
    dot = lambda x, y: jnp.dot(x, y, preferred_element_type=F32)
    return dot(a_hi, b_hi) + (dot(a_hi, b_lo) + dot(a_lo, b_hi))


def _rmsnorm_body(x_ref, g_ref, o_ref):
    x = x_ref[...]
    y = x * lax.rsqrt(jnp.mean(x * x, axis=-1, keepdims=True) + RMS_EPS)
    o_ref[...] = (y * g_ref[...]).astype(o_ref.dtype)


def _rmsnorm(x, g, out_dtype):
    rows, d = x.shape
    tm = _row_tile(rows, 512)
    return pl.pallas_call(
        _rmsnorm_body,
        grid=(rows // tm,),
        in_specs=[pl.BlockSpec((tm, d), lambda i: (i, 0)), pl.BlockSpec((1, d), lambda i: (0, 0))],
        out_specs=pl.BlockSpec((tm, d), lambda i: (i, 0)),
        out_shape=jax.ShapeDtypeStruct((rows, d), out_dtype),
        compiler_params=_params("parallel"),
        name="rmsnorm",
    )(x, g.reshape(1, d))


def _matmul_round_body(x_ref, w_ref, o_ref, wb_ref):
    @pl.when(pl.program_id(1) == 0)
    def _():
        wb_ref[...] = w_ref[...].astype(BF16)

    o_ref[...] = jnp.dot(x_ref[...], wb_ref[...], preferred_element_type=F32).astype(o_ref.dtype)


def _matmul_body(x_ref, w_ref, o_ref):
    o_ref[...] = jnp.dot(x_ref[...], w_ref[...], preferred_element_type=F32).astype(o_ref.dtype)


def _matmul(x, w, col_start=0, n_cols=None, out_dtype=F32):
    rows, k = x.shape
    n = w.shape[1] - col_start if n_cols is None else n_cols
    assert col_start % LANES == 0 and n % LANES == 0
    rounding = w.dtype != BF16
    tn = _col_tile(n, 1792)
    budget = VMEM_LIMIT_BYTES - (4 << 20)
    tm = _row_tile(rows, 1024)
    w_bytes = 2 * k * tn * (4 + 2) if rounding else 2 * k * tn * 2
    while w_bytes + 2 * tm * k * 2 + 2 * tm * tn * 4 > budget and tm % (2 * SUBLANES) == 0:
        tm //= 2
    w_spec = pl.BlockSpec((pl.Element(k), pl.Element(tn)),
                          lambda j, i: (0, pl.multiple_of(col_start + j * tn, LANES)))
    out_spec = pl.BlockSpec((tm, tn), lambda j, i: (i, j))
    out_shape = jax.ShapeDtypeStruct((rows, n), out_dtype)
    if rounding:
        out_spec = [out_spec, pl.BlockSpec((k, tn), lambda j, i: (0, j))]
        out_shape = [out_shape, jax.ShapeDtypeStruct((k, n), BF16)]
    return pl.pallas_call(
        _matmul_round_body if rounding else _matmul_body,
        grid=(n // tn, rows // tm),
        in_specs=[pl.BlockSpec((tm, k), lambda j, i: (i, 0)), w_spec],
        out_specs=out_spec,
        out_shape=out_shape,
        compiler_params=_params("parallel", "arbitrary"),
        name="matmul",
    )(x, w)


def _matmul_cached(x, prm, name, col_start=0, n_cols=None):
    key = (name, col_start)
    if key in prm:
        return _matmul(x, prm[key])
    out, prm[key] = _matmul(x, prm[name], col_start, n_cols)
    return out


def _shift_rows(x, halo, k):
    rolled = pltpu.roll(x, k, 0)
    row = lax.broadcasted_iota(jnp.int32, (SUBLANES, x.shape[1]), 0)
    top = rolled[0:SUBLANES]
    for j in range(k):
        top = jnp.where(row == j, halo[SUBLANES - k + j:SUBLANES - k + j + 1, :], top)
    return jnp.concatenate([top, rolled[SUBLANES:]], axis=0)


def _head_sum(x, ones_ref):
    hi = x.astype(BF16)
    lo = (x - hi.astype(F32)).astype(BF16)
    n, tiles = x.shape[0], x.shape[1] // LANES
    stacked = jnp.concatenate([part[:, j * LANES:(j + 1) * LANES] for part in (hi, lo) for j in range(tiles)], axis=0)
    s = jnp.dot(stacked, ones_ref[...], preferred_element_type=F32)
    return jnp.concatenate([s[j * n:(j + 1) * n] + s[(tiles + j) * n:(tiles + j + 1) * n] for j in range(tiles)],
                           axis=1)


def _pad_state_rows(state):
    b, n, f = state.shape
    return jnp.pad(state, ((0, 0), (0, SUBLANES - n), (0, 0))).reshape(b * SUBLANES, f)


def _rwkv_inputs(x, prev, mu_ref, w0_ref, wd_ref, a0_ref, wa_ref, wg_ref, kk_ref, ka_ref, rk_ref, ones_ref, width):
    xs = x + mu_ref[...] * (prev - x)

    w = width
    r = xs[:, 0:w]
    k = xs[:, w:2 * w]
    v = xs[:, 2 * w:3 * w]
    x_wa = xs[:, 3 * w:3 * w + LANES]
    x_g = xs[:, 3 * w + LANES:3 * w + 2 * LANES]

    z = -(w0_ref[...] + _dot3(jnp.tanh(x_wa), wd_ref[...]))
    softplus = jnp.maximum(z, 0.0) + jnp.log(1.0 + jnp.exp(-jnp.abs(z)))
    lw = -jnp.exp(-softplus - 0.5)
    a_lr = jax.nn.sigmoid(a0_ref[...] + _bdot(x_wa, wa_ref[...]))
    g = _bdot(jax.nn.sigmoid(x_g), wg_ref[...])

    kk = k * kk_ref[...]
    k_mod = k * (1.0 + (a_lr - 1.0) * ka_ref[...])
    kk = kk / jnp.maximum(jnp.sqrt(_head_sum(kk * kk, ones_ref)), KK_EPS)
    bonus = _head_sum(r * k_mod * rk_ref[...], ones_ref) * v
    return r, lw, k_mod, v, -kk, kk * a_lr, g, bonus


def _rwkv_body(rw_ref, sp_ref, mu_ref, w0_ref, wd_ref, a0_ref, wa_ref, wg_ref, kk_ref, ka_ref, rk_ref, ones_ref,
               lnw_ref, lnb_ref, h0_ref, o_ref, hout_ref, h_scr, carry, *, chunk, pairs, seqs, width, hd, n_chunks):
    c = pl.program_id(1)
    n_pairs = width // LANES

    @pl.when(c == 0)
    def _():
        zero = jnp.zeros((hd, hd), F32)
        for i in range(seqs):
            for p in range(n_pairs):
                h_scr[i, p] = jnp.concatenate(
                    [jnp.concatenate([h0_ref[i, 2 * p], zero], axis=1),
                     jnp.concatenate([zero, h0_ref[i, 2 * p + 1]], axis=1)], axis=0)
        carry[...] = jnp.zeros_like(carry)

    x = rw_ref[...]
    if n_chunks > 1:
        halo = jnp.where(c == 0, pltpu.roll(sp_ref[...], SUBLANES - 1, 0), carry[...])
        prev = _shift_rows(x, halo, 1)
        carry[...] = x[x.shape[0] - SUBLANES:]
    else:
        t = lax.broadcasted_iota(jnp.int32, x.shape, 0) & (chunk - 1)
        prev = jnp.where(t == 0, sp_ref[...], pltpu.roll(x, 1, 0))
    r_all, lw_all, k_all, v_all, a_all, b_all, g_all, bonus_all = _rwkv_inputs(
        x, prev, mu_ref, w0_ref, wd_ref, a0_ref, wa_ref, wg_ref, kk_ref, ka_ref, rk_ref, ones_ref, width)

    cs = chunk
    lanes = LANES * pairs
    heads = 2 * pairs
    n = heads * cs
    log_c = cs.bit_length() - 1
    log_hd = hd.bit_length() - 1
    n_stacks = width // lanes

    tri = (lax.broadcasted_iota(jnp.int32, (cs, cs), 0) >= lax.broadcasted_iota(jnp.int32, (cs, cs), 1)).astype(F32)
    row = lax.broadcasted_iota(jnp.int32, (n, lanes), 0)
    lane = lax.broadcasted_iota(jnp.int32, (n, lanes), 1)
    head_mask = (row >> log_c) == (lane >> log_hd)
    ri = lax.broadcasted_iota(jnp.int32, (n, n), 0)
    ci = lax.broadcasted_iota(jnp.int32, (n, n), 1)
    same_head = (ri >> log_c) == (ci >> log_c)
    rt = ri & (cs - 1)
    ct = ci & (cs - 1)
    strict = same_head & (ct < rt)
    incl = same_head & (ct <= rt)
    eye = (ri == ci).astype(F32)
    diag = (lax.broadcasted_iota(jnp.int32, (LANES, LANES), 0)
            == lax.broadcasted_iota(jnp.int32, (LANES, LANES), 1))

    def stack(x):
        return jnp.where(head_mask, jnp.concatenate([x] * heads, axis=0), 0.0)

    def fold(x):
        out = x[0:cs]
        for hidx in range(1, heads):
            out = out + x[hidx * cs:(hidx + 1) * cs]
        return out

    scaled = []
    for i in range(seqs):
        rs = slice(i * cs, (i + 1) * cs)
        lw = lw_all[rs]
        cum = _fdot(tri, lw)
        total = cum[cs - 1:cs]
        e_cum = jnp.exp(cum)
        e_neg = jnp.exp(-cum)
        e_rest = jnp.exp(total - cum)
        b = b_all[rs]
        k = k_all[rs]
        scaled.append(dict(a=a_all[rs] * jnp.exp(cum - lw), r=r_all[rs] * e_cum, b=b * e_neg, k=k * e_neg,
                           v=v_all[rs], bh=b * e_rest, kh=k * e_rest, total=total))

    inst = [(i, s) for i in range(seqs) for s in range(n_stacks)]
    jj = range(len(inst))

    def stacked(name):
        return [stack(scaled[i][name][:, s * lanes:(s + 1) * lanes]) for i, s in inst]

    a_st, r_st, b_st, k_st, v_st, bh_st, kh_st = (stacked(nm) for nm in ("a", "r", "b", "k", "v", "bh", "kh"))

    gram = [_bdot_nt(jnp.concatenate([a_st[j], r_st[j]], axis=0), jnp.concatenate([b_st[j], k_st[j]], axis=0))
            for j in jj]
    l_ab = [jnp.where(strict, gram[j][:n, :n], 0.0) for j in jj]
    m_rb = [jnp.where(incl, gram[j][n:, :n], 0.0) for j in jj]
    l_ak_m_rk = [jnp.concatenate([jnp.where(strict, gram[j][:n, n:], 0.0), jnp.where(incl, gram[j][n:, n:], 0.0)],
                                 axis=0) for j in jj]

    t_inv = [eye + l_ab[j] for j in jj]
    x_pow = [_bdot(l_ab[j], l_ab[j]) for j in jj]
    for _ in range(log_c - 2):
        both = [_bdot(jnp.concatenate([x_pow[j], t_inv[j]], axis=0), x_pow[j]) for j in jj]
        t_inv = [t_inv[j] + both[j][n:] for j in jj]
        x_pow = [both[j][:n] for j in jj]
    t_inv = [t_inv[j] + _bdot(t_inv[j], x_pow[j]) for j in jj]

    lv = [_bdot(l_ak_m_rk[j], v_st[j]) for j in jj]
    aw = [_bdot(t_inv[j], jnp.concatenate([a_st[j], lv[j][:n]], axis=1)) for j in jj]
    rb = [_bdot(m_rb[j], aw[j]) for j in jj]
    r_bar = [fold(r_st[j] + rb[j][:, :lanes]) for j in jj]
    y0 = [fold(rb[j][:, lanes:] + lv[j][n:]) for j in jj]

    jp = [(j, p) for j in jj for p in range(pairs)]

    def pair_lanes(x, p, off=0):
        return x[:, off + p * LANES:off + (p + 1) * LANES]

    h = [h_scr[inst[j][0], inst[j][1] * pairs + p] for j, p in jp]
    yv = [_bdot(pair_lanes(r_bar[j], p), h[q]) + pair_lanes(y0[j], p) for q, (j, p) in enumerate(jp)]
    per_seq = n_stacks * pairs
    y = jnp.concatenate([jnp.concatenate(yv[i * per_seq:(i + 1) * per_seq], axis=1) for i in range(seqs)], axis=0)
    mean = _head_sum(y, ones_ref) * (1.0 / hd)
    dev = y - mean
    var = _head_sum(dev * dev, ones_ref) * (1.0 / hd)
    yn = dev * lax.rsqrt(var + GN_EPS) * lnw_ref[...] + lnb_ref[...]
    o_ref[...] = ((yn + bonus_all) * g_all).astype(o_ref.dtype)
    bh_t = [pair_lanes(bh_st[j], p).T for j, p in jp]
    kh_t = [pair_lanes(kh_st[j], p).T for j, p in jp]
    pp = [_bdot(bh_t[q], jnp.concatenate([pair_lanes(aw[j], p), pair_lanes(aw[j], p, lanes)], axis=1))
          for q, (j, p) in enumerate(jp)]
    kv = [_bdot(kh_t[q], pair_lanes(v_st[j], p)) for q, (j, p) in enumerate(jp)]
    for q, (j, p) in enumerate(jp):
        i, s = inst[j]
        decay = pair_lanes(scaled[i]["total"], p, s * lanes)
        phi = jnp.where(diag, jnp.exp(decay), 0.0) + pp[q][:, :LANES]
        h_scr[i, s * pairs + p] = _bdot(phi, h[q]) + (pp[q][:, LANES:] + kv[q])

    @pl.when(c == pl.num_programs(1) - 1)
    def _():
        for i in range(seqs):
            for p in range(n_pairs):
                both = h_scr[i, p]
                hout_ref[i, 2 * p] = both[:hd, :hd]
                hout_ref[i, 2 * p + 1] = both[hd:, hd:]


def _rwkv(proj, shift_rows, h0, batch, seq_len, prm):
    rows = proj.shape[0]
    w, hd = prm["width"], prm["head_dim"]
    n_shift = prm["mu_shift"].shape[1]
    chunk = min(seq_len, RWKV_CHUNK)
    assert seq_len % chunk == 0 and chunk % SUBLANES == 0 and LANES % chunk == 0
    pairs = LANES // (2 * chunk)
    n_chunks = seq_len // chunk
    n_pairs = w // LANES
    seqs = 1
    while n_chunks == 1 and seqs < 4 and batch % (2 * seqs) == 0:
        seqs *= 2
    step_rows = seqs * chunk
    assert n_chunks == 1 or seqs == 1
    vec = lambda cols: pl.BlockSpec((step_rows, cols), lambda bi, ci: (bi * n_chunks + ci, 0))
    st = pl.BlockSpec((seqs, w // hd, hd, hd), lambda bi, ci: (bi, 0, 0, 0))
    row_spec = lambda c: pl.BlockSpec((1, c), lambda bi, ci: (0, 0))
    full = lambda a: pl.BlockSpec(a.shape, lambda bi, ci: (0, 0))
    return pl.pallas_call(
        functools.partial(_rwkv_body, chunk=chunk, pairs=pairs, seqs=seqs, width=w, hd=hd, n_chunks=n_chunks),
        grid=(batch // seqs, n_chunks),
        in_specs=[vec(n_shift), pl.BlockSpec((seqs * SUBLANES, n_shift), lambda bi, ci: (bi, 0)),
                  row_spec(n_shift), row_spec(w), full(prm["wd_pad"]), row_spec(w), full(prm["wa_pad"]),
                  full(prm["w_gate"]), row_spec(w), row_spec(w), row_spec(w), full(prm["ones2"]),
                  row_spec(w), row_spec(w), st],
        out_specs=[vec(w), st],
        out_shape=[jax.ShapeDtypeStruct((rows, w), BF16),
                   jax.ShapeDtypeStruct((batch, w // hd, hd, hd), F32)],
        scratch_shapes=[pltpu.VMEM((seqs, n_pairs, LANES, LANES), F32), pltpu.VMEM((SUBLANES, n_shift), F32)],
        compiler_params=_params("parallel", "arbitrary"),
        name="rwkv",
    )(proj, shift_rows, prm["mu_shift"], prm["w0"], prm["wd_pad"], prm["a0"], prm["wa_pad"], prm["w_gate"],
      prm["k_k"], prm["k_a"], prm["r_k"], prm["ones2"], prm["lnx_w"], prm["lnx_b"], h0)


def _sgu_body(gu_ref, gv_ref, sg_ref, sb_ref, wm_ref, bias_ref, o_ref, *maybe_v, groups, rows, seq_rows):
    u = jax.nn.gelu(gu_ref[...])
    vf = jax.nn.gelu(gv_ref[...])
    mu = jnp.mean(vf, axis=-1, keepdims=True)
    d = vf - mu
    var = jnp.mean(d * d, axis=-1, keepdims=True)
    v = (d * lax.rsqrt(var + LN_EPS)) * sg_ref[...] + sb_ref[...]
    if maybe_v:
        maybe_v[0][...] = v
    gd = v.shape[1] // groups
    ri = lax.broadcasted_iota(jnp.int32, (rows, rows), 0)
    ci = lax.broadcasted_iota(jnp.int32, (rows, rows), 1)
    causal = ri >= ci
    if seq_rows is not None:
        shift = seq_rows.bit_length() - 1
        causal = causal & ((ri >> shift) == (ci >> shift))
        pos = ((lax.broadcasted_iota(jnp.int32, (rows, seq_rows), 0) & (seq_rows - 1))
               == lax.broadcasted_iota(jnp.int32, (rows, seq_rows), 1)).astype(F32)
    for g in range(groups):
        sl = slice(g * gd, (g + 1) * gd)
        if seq_rows is None:
            w_full, bias = wm_ref[g], bias_ref[g]
        else:
            w_full = lax.dot_general(_fdot(pos, wm_ref[g][:seq_rows, :seq_rows]), pos, (((1,), (1,)), ((), ())),
                                     precision=HIGHEST, preferred_element_type=F32)
            bias = _fdot(pos, bias_ref[g])
        w_causal = jnp.where(causal, w_full, 0.0).astype(BF16)
        for blk in range(v.shape[0] // rows):
            rs = slice(blk * rows, (blk + 1) * rows)
            s = _bdot(w_causal, v[rs, sl]) + bias
            o_ref[rs, sl] = (u[rs, sl] * s).astype(o_ref.dtype)


def _col_window(rows_per_step, width, col0, row_of):
    return pl.BlockSpec((pl.Element(rows_per_step), pl.Element(width)),
                        lambda *idx: (pl.multiple_of(row_of(*idx) * rows_per_step, SUBLANES), col0))


def _sgu(proj, gu_col, w_mix, bias, prm, seq_rows, want_v_rows):
    rows = proj.shape[0]
    w = prm["width"]
    groups, chunk, _ = w_mix.shape
    mix_rows = chunk if seq_rows is None else _row_tile(rows, 2 * LANES)
    cs = mix_rows
    while cs < 512 and rows % (2 * cs) == 0:
        cs *= 2
    row_spec = pl.BlockSpec((1, w), lambda i: (0, 0))
    out = pl.BlockSpec((cs, w), lambda i: (i, 0))
    outs = pl.pallas_call(
        functools.partial(_sgu_body, groups=groups, rows=mix_rows, seq_rows=seq_rows),
        grid=(rows // cs,),
        in_specs=[_col_window(cs, w, gu_col, lambda i: i), _col_window(cs, w, gu_col + w, lambda i: i),
                  row_spec, row_spec,
                  pl.BlockSpec(w_mix.shape, lambda i: (0, 0, 0)), pl.BlockSpec(bias.shape, lambda i: (0, 0, 0))],
        out_specs=[out, out] if want_v_rows else [out],
        out_shape=[jax.ShapeDtypeStruct((rows, w), BF16)] + ([jax.ShapeDtypeStruct((rows, w), F32)]
                                                           if want_v_rows else []),
        compiler_params=_params("parallel"),
        name="sgu",
    )(proj, proj, prm["sgu_g"], prm["sgu_b"], w_mix, bias)
    return (outs[0], outs[1]) if want_v_rows else (outs[0], None)


def _xattn_body(q_ref, k_ref, v_ref, o_ref, *, heads, seqs, tq, m):
    hd = q_ref.shape[1] // heads
    scale = hd ** -0.5
    cases = [(h, i) for h in range(heads) for i in range(seqs)]
    cols = lambda h: slice(h * hd, (h + 1) * hd)
    mem = lambda i: slice(i * m, (i + 1) * m)
    s = [_bdot_nt(q_ref[i * tq:(i + 1) * tq, cols(h)], k_ref[mem(i), cols(h)]) * scale for h, i in cases]
    e = [jnp.exp(x - jnp.max(x, axis=-1, keepdims=True)) for x in s]
    p = [x / jnp.sum(x, axis=-1, keepdims=True) for x in e]
    o = [_bdot(p[c], v_ref[mem(i), cols(h)]) for c, (h, i) in enumerate(cases)]
    for h in range(heads):
        o_ref[:, cols(h)] = jnp.concatenate(o[h * seqs:(h + 1) * seqs], axis=0).astype(o_ref.dtype)


def _xattn_cache_body(q_ref, k_ref, v_ref, o_ref, *, heads, seqs, tq):
    hd = q_ref.shape[1] // heads
    m = k_ref.shape[1]
    n = m * heads
    scale = hd ** -0.5
    row_head = lax.broadcasted_iota(jnp.int32, (heads * tq, n), 0) >> (tq.bit_length() - 1)
    lane_head = lax.broadcasted_iota(jnp.int32, (heads * tq, n), 1) & (heads - 1)
    own = row_head == lane_head
    q = [jnp.concatenate([q_ref[i * tq:(i + 1) * tq, h * hd:(h + 1) * hd] for h in range(heads)], axis=0)
         for i in range(seqs)]
    s = [jnp.where(own, _bdot_nt(q[i], k_ref[i].reshape(n, hd)) * scale, -1e30) for i in range(seqs)]
    e = [jnp.exp(x - jnp.max(x, axis=-1, keepdims=True)) for x in s]
    p = [x / jnp.sum(x, axis=-1, keepdims=True) for x in e]
    o = [_bdot(p[i], v_ref[i].reshape(n, hd)) for i in range(seqs)]
    for h in range(heads):
        o_ref[:, h * hd:(h + 1) * hd] = jnp.concatenate(
            [o[i][h * tq:(h + 1) * tq] for i in range(seqs)], axis=0).astype(o_ref.dtype)


def _xattn(proj, q_col, w, mem_k, mem_v, batch, seq_len, heads):
    rows = proj.shape[0]
    tq = _row_tile(seq_len, 512)
    nq = seq_len // tq
    seqs = max(1, 32 // tq)
    assert batch % seqs == 0 and (seqs == 1 or nq == 1)
    if mem_k.ndim == 2:
        m = mem_k.shape[0] // batch
        kv = pl.BlockSpec((seqs * m, w), lambda bi, qi: (bi, 0))
        body = functools.partial(_xattn_body, heads=heads, seqs=seqs, tq=tq, m=m)
    else:
        assert tq & (tq - 1) == 0 and heads & (heads - 1) == 0
        kv = pl.BlockSpec((seqs,) + mem_k.shape[1:], lambda bi, qi: (bi, 0, 0, 0))
        body = functools.partial(_xattn_cache_body, heads=heads, seqs=seqs, tq=tq)
    return pl.pallas_call(
        body,
        grid=(batch // seqs, nq),
        in_specs=[_col_window(seqs * tq, w, q_col, lambda bi, qi: bi * nq + qi), kv, kv],
        out_specs=pl.BlockSpec((seqs * tq, w), lambda bi, qi: (bi * nq + qi, 0)),
        out_shape=jax.ShapeDtypeStruct((rows, w), BF16),
        compiler_params=_params("parallel", "parallel"),
        name="mem_xattn",
    )(proj, mem_k, mem_v)


def _gate_branch_body(*refs, rounding):
    if rounding:
        x_ref, a_ref, b_ref, c_ref, wg0_ref, wg1_ref, wg2_ref, wb_ref, o_ref, wg_bf, wb_bf = refs

        @pl.when(pl.program_id(1) == 0)
        def _():
            for n, src in enumerate((wg0_ref, wg1_ref, wg2_ref)):
                wg_bf[n] = src[...].astype(BF16)
            wb_bf[...] = wb_ref[...].astype(BF16)
    else:
        x_ref, a_ref, b_ref, c_ref, wg_bf, wb_bf, o_ref = refs

    x = x_ref[...]
    acc = None
    for n, br_ref in enumerate((a_ref, b_ref, c_ref)):
        gate = jax.nn.sigmoid(jnp.dot(x, wg_bf[n], preferred_element_type=F32))
        term = gate * jnp.dot(br_ref[...], wb_bf[n], preferred_element_type=F32)
        acc = term if acc is None else acc + term
    o_ref[...] = acc.astype(o_ref.dtype)


def _gate_branch_mix(xn, a_out, b_out, c_out, w_gates, gate_col0, w_branch):
    rows, k = xn.shape
    w = a_out.shape[1]
    nb, _, d = w_branch.shape
    assert nb == 3
    rounding = w_branch.dtype != BF16
    tm = _row_tile(rows, 1024)
    tn = _col_tile(d, 256)
    nj = d // tn
    act = lambda width: pl.BlockSpec((tm, width), lambda j, i: (i, 0))
    wb_spec = pl.BlockSpec((nb, w, tn), lambda j, i: (0, 0, j))
    wg_spec = pl.BlockSpec((nb, k, tn), lambda j, i: (0, 0, j))
    out_specs = [pl.BlockSpec((tm, tn), lambda j, i: (i, j))]
    out_shape = [jax.ShapeDtypeStruct((rows, d), BF16)]
    if rounding:
        gate_cols = lambda n: pl.BlockSpec(
            (pl.Element(k), pl.Element(tn)),
            lambda j, i: (0, pl.multiple_of(gate_col0 + n * d + j * tn, LANES)))
        w_specs = [gate_cols(0), gate_cols(1), gate_cols(2), wb_spec]
        w_args = (w_gates, w_gates, w_gates, w_branch)
        out_specs += [wg_spec, wb_spec]
        out_shape += [jax.ShapeDtypeStruct((nb, k, d), BF16), jax.ShapeDtypeStruct(w_branch.shape, BF16)]
    else:
        w_specs = [wg_spec, wb_spec]
        w_args = (w_gates, w_branch)
    outs = pl.pallas_call(
        functools.partial(_gate_branch_body, rounding=rounding),
        grid=(nj, rows // tm),
        in_specs=[act(k), act(w), act(w), act(w)] + w_specs,
        out_specs=out_specs,
        out_shape=out_shape,
        compiler_params=_params("parallel", "arbitrary"),
        name="gate_branch_mix",
    )(xn, a_out, b_out, c_out, *w_args)
    return outs[0], ((outs[1], outs[2]) if rounding else (w_gates, w_branch))


def _mm_res_norm_body(x_ref, w_ref, res_ref, g_ref, g2_ref, y_ref, *maybe_next, with_next):
    f = jnp.dot(x_ref[...], w_ref[...], preferred_element_type=F32)
    y = res_ref[...] + (f * lax.rsqrt(jnp.mean(f * f, axis=-1, keepdims=True) + RMS_EPS)) * g_ref[...]
    y_ref[...] = y
    if with_next:
        yn = y * lax.rsqrt(jnp.mean(y * y, axis=-1, keepdims=True) + RMS_EPS)
        maybe_next[0][...] = (yn * g2_ref[...]).astype(maybe_next[0].dtype)


def _mm_res_norm(x, w, res, g, g_next=None):
    rows, k = x.shape
    d = w.shape[1]
    with_next = g_next is not None
    tm = _row_tile(rows, 512 if k * d * 2 <= (12 << 20) else 256)
    blk = pl.BlockSpec((tm, d), lambda i: (i, 0))
    row_spec = pl.BlockSpec((1, d), lambda i: (0, 0))
    out_specs = [blk, blk] if with_next else [blk]
    out_shape = [jax.ShapeDtypeStruct((rows, d), F32)]
    if with_next:
        out_shape.append(jax.ShapeDtypeStruct((rows, d), BF16))
    g2 = g_next if with_next else g
    outs = pl.pallas_call(
        functools.partial(_mm_res_norm_body, with_next=with_next),
        grid=(rows // tm,),
        in_specs=[pl.BlockSpec((tm, k), lambda i: (i, 0)),
                  pl.BlockSpec((k, d), lambda i: (0, 0), pipeline_mode=pl.Buffered(1)),
                  blk, row_spec, row_spec],
        out_specs=out_specs,
        out_shape=out_shape,
        compiler_params=_params("arbitrary"),
        name="matmul_res_norm",
    )(x, w, res, g.reshape(1, d), g2.reshape(1, d))
    return outs if with_next else outs[0]


def _up_conv_gate_body(x_ref, wg_ref, wv_ref, eg_ref, ev_ref, cwg_ref, cwv_ref, cbg_ref, cbv_ref,
                       o_ref, tg_ref, tv_ref, *rest, seq_rows, tm, taps, rounding):
    i = pl.program_id(1)
    if rounding:
        wg_bf, wv_bf, keep_g, keep_v = rest
    else:
        keep_g, keep_v = rest
        wg_bf, wv_bf = wg_ref, wv_ref

    @pl.when(i == 0)
    def _():
        if rounding:
            wg_bf[...] = wg_ref[...].astype(BF16)
            wv_bf[...] = wv_ref[...].astype(BF16)
        keep_g[...] = jnp.zeros_like(keep_g)
        keep_v[...] = jnp.zeros_like(keep_v)

    if seq_rows >= tm:
        is_start = (i * tm) % seq_rows == 0
        tn = o_ref.shape[1]
        row8 = lax.broadcasted_iota(jnp.int32, (SUBLANES, tn), 0)

        def first_halo(e_ref, keep):
            start = jnp.zeros((SUBLANES, tn), F32)
            for idx in range(taps - 1):
                start = jnp.where(row8 == SUBLANES - (taps - 1) + idx, e_ref[0, idx:idx + 1, :], start)
            return jnp.where(is_start, start, keep[...])

        def conv_rows(u, halo, cw, cb):
            acc = cb + cw[taps - 1:taps] * u
            for back in range(1, taps):
                acc = acc + cw[taps - 1 - back:taps - back] * _shift_rows(u, halo, back)
            return acc

        halo_g, halo_v = first_halo(eg_ref, keep_g), first_halo(ev_ref, keep_v)
        cwg, cwv, cbg, cbv = cwg_ref[...], cwv_ref[...], cbg_ref[...], cbv_ref[...]
        xt = x_ref[...]
        ug = jnp.dot(xt, wg_bf[...], preferred_element_type=F32)
        uv = jnp.dot(xt, wv_bf[...], preferred_element_type=F32)
        o_ref[...] = (jax.nn.gelu(conv_rows(ug, halo_g, cwg, cbg)) * conv_rows(uv, halo_v, cwv, cbv)).astype(o_ref.dtype)
        for keep, tail_ref, u in ((keep_g, tg_ref, ug), (keep_v, tv_ref, uv)):
            keep[...] = u[tm - SUBLANES:tm]
            tail_ref[...] = u[tm - SUBLANES:tm]
        return

    x = x_ref[...]

    def conv(w_bf, e_ref, cw_ref, cb_ref, keep, tail_ref):
        u = jnp.dot(x, w_bf[...], preferred_element_type=F32)
        cw = cw_ref[...]
        acc = cb_ref[...] + cw[taps - 1:taps] * u
        n_lane_tiles = u.shape[1] // LANES
        t = lax.broadcasted_iota(jnp.int32, u.shape, 0) & (seq_rows - 1)
        for idx in range(taps - 1):
            for c in range(n_lane_tiles):
                keep[c, pl.ds(idx, tm // seq_rows, stride=seq_rows), :] = e_ref[:, idx, c * LANES:(c + 1) * LANES]
        e = jnp.concatenate([keep[c] for c in range(n_lane_tiles)], axis=1)
        for back in range(1, taps):
            up_by = taps - 1 - back
            state = pltpu.roll(e, tm - up_by, 0) if up_by else e
            prev = jnp.where(t < back, state, pltpu.roll(u, back, 0))
            acc = acc + cw[taps - 1 - back:taps - back] * prev
        for c in range(n_lane_tiles):
            keep[c] = u[:, c * LANES:(c + 1) * LANES]
        for idx in range(taps - 1):
            rows_t = pl.ds(seq_rows - (taps - 1) + idx, tm // seq_rows, stride=seq_rows)
            tail_ref[idx] = jnp.concatenate([keep[c, rows_t, :] for c in range(n_lane_tiles)], axis=1)
        return acc

    gate = conv(wg_bf, eg_ref, cwg_ref, cbg_ref, keep_g, tg_ref)
    val = conv(wv_bf, ev_ref, cwv_ref, cbv_ref, keep_v, tv_ref)
    o_ref[...] = (jax.nn.gelu(gate) * val).astype(o_ref.dtype)


def _up_conv_gate(x, w_up, state_rows, conv_w, conv_b, seq_rows):
    rows, d = x.shape
    rounding = not isinstance(w_up, tuple)
    f2 = conv_w.shape[1]
    dff = f2 // 2
    taps = conv_w.shape[0]
    tm = _row_tile(rows if seq_rows == SUBLANES else seq_rows, 1024)
    tn = _col_tile(dff, 512)
    nj = dff // tn
    if seq_rows >= tm:
        assert seq_rows % tm == 0
        st = lambda off: pl.BlockSpec((1, taps - 1, tn), lambda j, i: ((i * tm) // seq_rows, 0, j + off))
        tail = pl.BlockSpec((SUBLANES, tn), lambda j, i: ((i * tm) // seq_rows, j))
        tail_shape = ((rows // seq_rows) * SUBLANES, dff)
        keep_shape = (SUBLANES, tn)
    else:
        assert seq_rows == SUBLANES and tm % seq_rows == 0
        st = lambda off: pl.BlockSpec((tm // seq_rows, taps - 1, tn), lambda j, i: (i, 0, j + off))
        tail = pl.BlockSpec((taps - 1, tm // seq_rows, tn), lambda j, i: (0, i, j))
        tail_shape = (taps - 1, rows // seq_rows, dff)
        keep_shape = (tn // LANES, tm, LANES)
    wt = lambda off: pl.BlockSpec((d, tn), lambda j, i: (0, j + off))
    cw = lambda off: pl.BlockSpec((taps, tn), lambda j, i: (0, j + off))
    cb = lambda off: pl.BlockSpec((1, tn), lambda j, i: (0, j + off))
    out_specs = [pl.BlockSpec((tm, tn), lambda j, i: (i, j)), tail, tail]
    out_shape = [jax.ShapeDtypeStruct((rows, dff), BF16), jax.ShapeDtypeStruct(tail_shape, F32),
                 jax.ShapeDtypeStruct(tail_shape, F32)]
    if rounding:
        w_gate, w_val, w_specs = w_up, w_up, [wt(0), wt(nj)]
        out_specs += [wt(0), wt(0)]
        out_shape += [jax.ShapeDtypeStruct((d, dff), BF16)] * 2
    else:
        (w_gate, w_val), w_specs = w_up, [wt(0), wt(0)]
    outs = pl.pallas_call(
        functools.partial(_up_conv_gate_body, seq_rows=seq_rows, tm=tm, taps=taps, rounding=rounding),
        grid=(nj, rows // tm),
        in_specs=[pl.BlockSpec((tm, d), lambda j, i: (i, 0))] + w_specs + [st(0), st(nj),
                                                                             cw(0), cw(nj), cb(0), cb(nj)],
        out_specs=out_specs,
        out_shape=out_shape,
        scratch_shapes=[pltpu.VMEM(keep_shape, F32), pltpu.VMEM(keep_shape, F32)],
        compiler_params=_params("parallel", "arbitrary"),
        name="up_conv_gate",
    )(x, w_gate, w_val, state_rows, state_rows, conv_w, conv_w, conv_b.reshape(1, f2), conv_b.reshape(1, f2))
    act, tail_g, tail_v = outs[:3]
    w_bf = tuple(outs[3:]) if rounding else w_up
    tail = jnp.concatenate([tail_g, tail_v], axis=-1)
    if seq_rows >= tm:
        return act, tail.reshape(rows // seq_rows, SUBLANES, f2)[:, SUBLANES - (taps - 1):], w_bf
    return act, jnp.swapaxes(tail, 0, 1), w_bf


def _layer(x, shift_prev, wkv0, mem_k, mem_v, conv_prev, prm, want_v_rows):
    batch, seq_len, d = x.shape
    rows = batch * seq_len
    w = prm["width"]
    hd = prm["head_dim"]
    x2 = x.reshape(rows, d)

    xn = _rmsnorm(x2, prm["g_pre_mix"], BF16)
    n_shift = prm["mu_shift"].shape[1]
    proj = _matmul_cached(xn, prm, "w_in", 0, n_shift + 3 * w)
    new_shift = proj.reshape(batch, seq_len, -1)[:, -1, :n_shift]

    a_out, h_new = _rwkv(proj, _pad_state_rows(shift_prev[:, None, :]), jnp.swapaxes(wkv0, -1, -2), batch,
                         seq_len, prm)
    new_wkv = jnp.swapaxes(h_new, -1, -2)

    if seq_len % prm["sgu_chunk"] == 0:
        b_out, v_rows = _sgu(proj, n_shift, prm["w_s"], prm["sgu_bias"], prm, None, want_v_rows)
    else:
        assert seq_len == SUBLANES
        b_out, v_rows = _sgu(proj, n_shift, prm["w_s"], prm["sgu_bias"][:, :SUBLANES], prm, SUBLANES,
                             want_v_rows)
    c_out = _xattn(proj, n_shift + 2 * w, w, mem_k, mem_v, batch, seq_len, prm["xattn_heads"])

    mix, (prm["w_gates"], prm["w_branch"]) = _gate_branch_mix(
        xn, a_out, b_out, c_out, prm["w_gates"], n_shift + 3 * w, prm["w_branch"])
    h, hn = _mm_res_norm(mix, prm["w_out"], x2, prm["g_post_mix"], prm["g_pre_ffn"])

    act, conv_new, prm["w_up"] = _up_conv_gate(hn, prm["w_up"], conv_prev, prm["conv_w"],
                                               prm["conv_b"], seq_len)
    y_out = _mm_res_norm(act, prm["w_down"], h, prm["g_post_ffn"])
    if want_v_rows:
        v_rows = v_rows.reshape(batch, seq_len, w)
    return y_out.reshape(batch, seq_len, d), new_shift, new_wkv, v_rows, conv_new


def _prepare(l, g_pre_mix, w_in, mu_shift, w0, w_decay, a0, w_aaa, w_gate, k_k, k_a, r_k, lnx_w, lnx_b,
             sgu_g, sgu_b, w_s, b_s, w_branch, w_out, g_post_mix, g_pre_ffn, w_up, conv_w, conv_b, w_down,
             g_post_ffn):
    heads, hd = r_k.shape[1], r_k.shape[2]
    w = heads * hd
    d = w_in.shape[1]
    n_shift = mu_shift.shape[1]
    rank_d, rank_a, rank_g = w_decay.shape[1], w_aaa.shape[1], w_gate.shape[1]
    assert rank_d + rank_a == LANES and rank_g == LANES and n_shift == 3 * w + 2 * LANES
    groups, sgu_chunk, _ = w_s.shape[1:]
    row = lambda t: t[l].reshape(1, -1)
    lane_head = jnp.arange(LANES) // hd
    gd = w // groups
    return dict(
        width=w, head_dim=hd, sgu_chunk=sgu_chunk,
        g_pre_mix=g_pre_mix[l],
        w_in=w_in[l], w_gates=w_in[l],
        mu_shift=row(mu_shift), w0=row(w0), a0=row(a0),
        wd_pad=jnp.pad(w_decay[l], ((0, rank_a), (0, 0))),
        wa_pad=jnp.pad(w_aaa[l], ((rank_d, 0), (0, 0))),
        w_gate=w_gate[l],
        k_k=row(k_k), k_a=row(k_a), r_k=row(r_k), lnx_w=row(lnx_w), lnx_b=row(lnx_b),
        ones2=(lane_head[:, None] == lane_head[None, :]).astype(BF16),
        sgu_g=row(sgu_g), sgu_b=row(sgu_b),
        w_s=w_s[l],
        sgu_bias=jnp.broadcast_to(b_s[l][:, :, None], (groups, sgu_chunk, gd)),
        w_branch=w_branch[l], w_out=w_out[l].astype(BF16),
        g_post_mix=g_post_mix[l], g_pre_ffn=g_pre_ffn[l],
        w_up=w_up[l], conv_w=conv_w[l], conv_b=conv_b[l], w_down=w_down[l].astype(BF16),
        g_post_ffn=g_post_ffn[l],
    )


def kernel(x_prompt, x_sample, mem_prompt, state_wkv, state_shift, cache_mem_k, cache_mem_v, state_ffn_conv, g_pre_mix, w_in, mu_shift, w0, w_decay, a0, w_aaa, w_gate, k_k, k_a, r_k, lnx_w, lnx_b, sgu_g, sgu_b, w_s, b_s, g_mem, w_mem_k, w_mem_v, w_branch, w_out, g_post_mix, g_pre_ffn, w_up, conv_w, conv_b, w_down, g_post_ffn):
    depth = w_in.shape[0]
    batch = x_prompt.shape[0]
    mem_len, d = mem_prompt.shape[1], mem_prompt.shape[2]
    xh, xhd = cache_mem_k.shape[3], cache_mem_k.shape[4]
    heads, hd = r_k.shape[1], r_k.shape[2]
    n_shift = mu_shift.shape[1]
    f2 = w_up.shape[2]
    taps = conv_w.shape[1]
    y_p, y_s = x_prompt, x_sample
    outs = [[] for _ in range(9)]
    for l in range(depth):
        prm = _prepare(l, g_pre_mix, w_in, mu_shift, w0, w_decay, a0, w_aaa, w_gate, k_k, k_a, r_k, lnx_w, lnx_b,
                       sgu_g, sgu_b, w_s, b_s, w_branch, w_out, g_post_mix, g_pre_ffn, w_up, conv_w, conv_b,
                       w_down, g_post_ffn)
        mn = _rmsnorm(mem_prompt.reshape(batch * mem_len, d), g_mem[l], BF16)
        prm["xattn_heads"] = xh
        mk_rows = _matmul(mn, w_mem_k[l])[0]
        mv_rows = _matmul(mn, w_mem_v[l])[0]
        mk_p = mk_rows.reshape(batch, mem_len, xh, xhd)
        mv_p = mv_rows.reshape(batch, mem_len, xh, xhd)
        y_p, sh_p, wkv_p, _, cv_p = _layer(
            y_p, jnp.zeros((batch, n_shift), F32), jnp.zeros((batch, heads, hd, hd), F32), mk_rows, mv_rows,
            jnp.zeros((batch, taps - 1, f2), F32), prm, False)
        y_s, sh_s, wkv_s, vr_s, cv_s = _layer(y_s, state_shift[l], state_wkv[l], cache_mem_k[l], cache_mem_v[l],
                                              state_ffn_conv[l], prm, True)
        for lst, val in zip(outs, (wkv_p, sh_p, mk_p, mv_p, cv_p, wkv_s, sh_s, vr_s, cv_s)):
            lst.append(val)
    return (y_p, y_s) + tuple(jnp.stack(lst) for lst in outs)
```

```python
import functools

import jax
import jax.numpy as jnp
from jax import lax
from jax.experimental import pallas as pl
from jax.experimental.pallas import tpu as pltpu

F32 = jnp.float32
BF16 = jnp.bfloat16
HIGHEST = lax.Precision.HIGHEST

LANES = 128
SUBLANES = 8
VMEM_LIMIT_BYTES = 56 * 1024 * 1024

RMS_EPS = 1e-6
LN_EPS = 1e-5
GN_EPS = 64e-5
KK_EPS = 1e-12
RWKV_CHUNK = 64
RWKV_GROUP = 32


def _params(*semantics):
    return pltpu.CompilerParams(dimension_semantics=semantics, vmem_limit_bytes=VMEM_LIMIT_BYTES)


def _row_tile(rows, pref):
    t = min(rows, pref)
    while rows % t:
        t -= SUBLANES
    return t


def _col_tile(n, cap):
    best = n
    for t in range(LANES, min(n, cap) + 1, LANES):
        if n % t == 0:
            best = t
    return best


def _bdot(a, b):
    return jnp.dot(a.astype(BF16), b.astype(BF16), preferred_element_type=F32)


def _bdot_nt(a, b):
    return lax.dot_general(a.astype(BF16), b.astype(BF16), (((1,), (1,)), ((), ())),
                           preferred_element_type=F32)


def _fdot(a, b):
    return jnp.dot(a, b, precision=HIGHEST, preferred_element_type=F32)


def _dot3(a, b):
    a_hi = a.astype(BF16)
    b_hi = b.astype(BF16)
    a_lo = (a - a_hi.astype(F32)).astype(BF16)
    b_lo = (b - b_hi.astype(F32)).astype(BF16)
    dot = lambda x, y: jnp.dot(x, y, preferred_element_type=F32)
    return dot(a_hi, b_hi) + (dot(a_hi, b_lo) + dot(a_lo, b_hi))


def _rmsnorm_body(x_ref, g_ref, o_ref):
    x = x_ref[...]
    y = x * lax.rsqrt(jnp.mean(x * x, axis=-1, keepdims=True) + RMS_EPS)
    o_ref[...] = (y * g_ref[...]).astype(o_ref.dtype)


def _rmsnorm(x, g, out_dtype):
    rows, d = x.shape
    tm = _row_tile(rows, 512)
    return pl.pallas_call(
        _rmsnorm_body,
        grid=(rows // tm,),
        in_specs=[pl.BlockSpec((tm, d), lambda i: (i, 0)), pl.BlockSpec((1, d), lambda i: (0, 0))],
        out_specs=pl.BlockSpec((tm, d), lambda i: (i, 0)),
        out_shape=jax.ShapeDtypeStruct((rows, d), out_dtype),
        compiler_params=_params("parallel"),
        name="rmsnorm",
    )(x, g.reshape(1, d))


def _matmul_round_body(x_ref, w_ref, o_ref, wb_ref):
    @pl.when(pl.program_id(1) == 0)
    def _():
        wb_ref[...] = w_ref[...].astype(BF16)

    o_ref[...] = jnp.dot(x_ref[...], wb_ref[...], preferred_element_type=F32).astype(o_ref.dtype)


def _matmul_body(x_ref, w_ref, o_ref):
    o_ref[...] = jnp.dot(x_ref[...], w_ref[...], preferred_element_type=F32).astype(o_ref.dtype)


def _matmul(x, w, col_start=0, n_cols=None, out_dtype=F32):
    rows, k = x.shape
    n = w.shape[1] - col_start if n_cols is None else n_cols
    assert col_start % LANES == 0 and n % LANES == 0
    rounding = w.dtype != BF16
    tn = _col_tile(n, 1792)
    budget = VMEM_LIMIT_BYTES - (4 << 20)
    tm = _row_tile(rows, 1024)
    w_bytes = 2 * k * tn * (4 + 2) if rounding else 2 * k * tn * 2
    while w_bytes + 2 * tm * k * 2 + 2 * tm * tn * 4 > budget and tm % (2 * SUBLANES) == 0:
        tm //= 2
    w_spec = pl.BlockSpec((pl.Element(k), pl.Element(tn)),
                          lambda j, i: (0, pl.multiple_of(col_start + j * tn, LANES)))
    out_spec = pl.BlockSpec((tm, tn), lambda j, i: (i, j))
    out_shape = jax.ShapeDtypeStruct((rows, n), out_dtype)
    if rounding:
        out_spec = [out_spec, pl.BlockSpec((k, tn), lambda j, i: (0, j))]
        out_shape = [out_shape, jax.ShapeDtypeStruct((k, n), BF16)]
    return pl.pallas_call(
        _matmul_round_body if rounding else _matmul_body,
        grid=(n // tn, rows // tm),
        in_specs=[pl.BlockSpec((tm, k), lambda j, i: (i, 0)), w_spec],
        out_specs=out_spec,
        out_shape=out_shape,
        compiler_params=_params("parallel", "arbitrary"),
        name="matmul",
    )(x, w)


def _matmul_cached(x, prm, name, col_start=0, n_cols=None):
    key = (name, col_start)
    if key in prm:
        return _matmul(x, prm[key])
    out, prm[key] = _matmul(x, prm[name], col_start, n_cols)
    return out


def _shift_rows(x, halo, k):
    rolled = pltpu.roll(x, k, 0)
    row = lax.broadcasted_iota(jnp.int32, (SUBLANES, x.shape[1]), 0)
    top = rolled[0:SUBLANES]
    for j in range(k):
        top = jnp.where(row == j, halo[SUBLANES - k + j:SUBLANES - k + j + 1, :], top)
    return jnp.concatenate([top, rolled[SUBLANES:]], axis=0)


def _head_sum_operand(x):
    hi = x.astype(BF16)
    lo = (x - hi.astype(F32)).astype(BF16)
    tiles = x.shape[1] // LANES
    return jnp.concatenate([part[:, j * LANES:(j + 1) * LANES] for part in (hi, lo) for j in range(tiles)], axis=0)


def _head_sum_result(s, n, tiles):
    return jnp.concatenate([s[j * n:(j + 1) * n] + s[(tiles + j) * n:(tiles + j + 1) * n] for j in range(tiles)],
                           axis=1)


def _head_sum(x, ones_ref):
    s = jnp.dot(_head_sum_operand(x), ones_ref[...], preferred_element_type=F32)
    return _head_sum_result(s, x.shape[0], x.shape[1] // LANES)


def _pad_state_rows(state):
    b, n, f = state.shape
    return jnp.pad(state, ((0, 0), (0, SUBLANES - n), (0, 0))).reshape(b * SUBLANES, f)


def _rwkv_inputs(x, prev, mu_ref, w0_ref, wd_ref, a0_ref, wa_ref, wg_ref, kk_ref, ka_ref, rk_ref, ones_ref, width):
    xs = x + mu_ref[...] * (prev - x)
    w = width
    n = x.shape[0]
    r = xs[:, 0:w]
    k = xs[:, w:2 * w]
    v = xs[:, 2 * w:3 * w]
    x_wa = xs[:, 3 * w:3 * w + LANES]
    x_g = xs[:, 3 * w + LANES:3 * w + 2 * LANES]

    z = -(w0_ref[...] + _dot3(jnp.tanh(x_wa), wd_ref[...]))
    softplus = jnp.maximum(z, 0.0) + jnp.log(1.0 + jnp.exp(-jnp.abs(z)))
    lw = -jnp.exp(-softplus - 0.5)
    a_lr = jax.nn.sigmoid(a0_ref[...] + _bdot(x_wa, wa_ref[...]))
    g = _bdot(jax.nn.sigmoid(x_g), wg_ref[...])

    kk = k * kk_ref[...]
    k_mod = k * (1.0 + (a_lr - 1.0) * ka_ref[...])
    sums = _head_sum(jnp.concatenate([kk * kk, r * k_mod * rk_ref[...]], axis=0), ones_ref)
    kk = kk / jnp.maximum(jnp.sqrt(sums[:n]), KK_EPS)
    return r, lw, k_mod, v, -kk, kk * a_lr, g, sums[n:] * v


def _rwkv_body(rw_ref, sp_ref, mu_ref, w0_ref, wd_ref, a0_ref, wa_ref, wg_ref, kk_ref, ka_ref, rk_ref, ones_ref,
               lnw_ref, lnb_ref, h0_ref, o_ref, hout_ref, h_scr, carry, *, chunk, pairs, seqs, width, hd, n_chunks, group):
    c = pl.program_id(1)
    n_pairs = width // LANES

    @pl.when(c == 0)
    def _():
        zero = jnp.zeros((hd, hd), F32)
        for i in range(seqs):
            for p in range(n_pairs):
                h_scr[i, p] = jnp.concatenate(
                    [jnp.concatenate([h0_ref[i, 2 * p], zero], axis=1),
                     jnp.concatenate([zero, h0_ref[i, 2 * p + 1]], axis=1)], axis=0)
        carry[...] = jnp.zeros_like(carry)

    if n_chunks > 1:
        x = rw_ref[...].reshape(seqs * chunk, rw_ref.shape[2])
        prevs = []
        for i in range(seqs):
            xi = x[i * chunk:(i + 1) * chunk]
            before = jnp.where(c == 0, pltpu.roll(sp_ref[i * SUBLANES:(i + 1) * SUBLANES, :], SUBLANES - 1, 0),
                               carry[i])
            prevs.append(_shift_rows(xi, before, 1))
            carry[i] = xi[chunk - SUBLANES:]
        prev = jnp.concatenate(prevs, axis=0)
    else:
        x = rw_ref[...]
        t = lax.broadcasted_iota(jnp.int32, x.shape, 0) & (chunk - 1)
        prev = jnp.where(t == 0, sp_ref[...], pltpu.roll(x, 1, 0))
    r_all, lw_all, k_all, v_all, a_all, b_all, g_all, bonus_all = _rwkv_inputs(
        x, prev, mu_ref, w0_ref, wd_ref, a0_ref, wa_ref, wg_ref, kk_ref, ka_ref, rk_ref, ones_ref, width)

    cs = chunk
    lanes = LANES * pairs
    heads = 2 * pairs
    n = heads * cs
    log_c = cs.bit_length() - 1
    log_hd = hd.bit_length() - 1
    n_stacks = width // lanes

    tri = (lax.broadcasted_iota(jnp.int32, (cs, cs), 0) >= lax.broadcasted_iota(jnp.int32, (cs, cs), 1)).astype(F32)
    row = lax.broadcasted_iota(jnp.int32, (n, lanes), 0)
    lane = lax.broadcasted_iota(jnp.int32, (n, lanes), 1)
    head_mask = (row >> log_c) == (lane >> log_hd)
    ri = lax.broadcasted_iota(jnp.int32, (n, n), 0)
    ci = lax.broadcasted_iota(jnp.int32, (n, n), 1)
    same_head = (ri >> log_c) == (ci >> log_c)
    rt = ri & (cs - 1)
    ct = ci & (cs - 1)
    strict = same_head & (ct < rt)
    incl = same_head & (ct <= rt)
    eye = (ri == ci).astype(F32)
    diag = (lax.broadcasted_iota(jnp.int32, (LANES, LANES), 0)
            == lax.broadcasted_iota(jnp.int32, (LANES, LANES), 1))

    def stack(x):
        return jnp.where(head_mask, jnp.concatenate([x] * heads, axis=0), 0.0)

    def fold(x):
        out = x[0:cs]
        for hidx in range(1, heads):
            out = out + x[hidx * cs:(hidx + 1) * cs]
        return out

    scaled = []
    for i in range(seqs):
        rs = slice(i * cs, (i + 1) * cs)
        lw = lw_all[rs]
        cum = _fdot(tri, lw)
        total = cum[cs - 1:cs]
        e_cum = jnp.exp(cum)
        e_neg = jnp.exp(-cum)
        e_rest = jnp.exp(total - cum)
        b = b_all[rs]
        k = k_all[rs]
        scaled.append(dict(a=a_all[rs] * jnp.exp(cum - lw), r=r_all[rs] * e_cum, b=b * e_neg, k=k * e_neg,
                           v=v_all[rs], bh=b * e_rest, kh=k * e_rest, total=total))

    def pair_lanes(x, p, off=0):
        return x[:, off + p * LANES:off + (p + 1) * LANES]

    def recur(inst):
        jj = range(len(inst))

        def stacked(name):
            return [stack(scaled[i][name][:, s * lanes:(s + 1) * lanes]) for i, s in inst]

        a_st, r_st, b_st, k_st, v_st, bh_st, kh_st = (stacked(nm) for nm in ("a", "r", "b", "k", "v", "bh", "kh"))

        gram = [_bdot_nt(jnp.concatenate([a_st[j], r_st[j]], axis=0), jnp.concatenate([b_st[j], k_st[j]], axis=0))
                for j in jj]
        l_ab = [jnp.where(strict, gram[j][:n, :n], 0.0) for j in jj]
        m_rb = [jnp.where(incl, gram[j][n:, :n], 0.0) for j in jj]
        l_ak_m_rk = [jnp.concatenate([jnp.where(strict, gram[j][:n, n:], 0.0),
                                      jnp.where(incl, gram[j][n:, n:], 0.0)], axis=0) for j in jj]

        t_inv = [eye + l_ab[j] for j in jj]
        x_pow = [_bdot(l_ab[j], l_ab[j]) for j in jj]
        for _ in range(log_c - 2):
            both = [_bdot(jnp.concatenate([x_pow[j], t_inv[j]], axis=0), x_pow[j]) for j in jj]
            t_inv = [t_inv[j] + both[j][n:] for j in jj]
            x_pow = [both[j][:n] for j in jj]
        t_inv = [t_inv[j] + _bdot(t_inv[j], x_pow[j]) for j in jj]

        lv = [_bdot(l_ak_m_rk[j], v_st[j]) for j in jj]
        aw = [_bdot(t_inv[j], jnp.concatenate([a_st[j], lv[j][:n]], axis=1)) for j in jj]
        rb = [_bdot(m_rb[j], aw[j]) for j in jj]
        r_bar = [fold(r_st[j] + rb[j][:, :lanes]) for j in jj]
        y0 = [fold(rb[j][:, lanes:] + lv[j][n:]) for j in jj]

        jp = [(j, p) for j in jj for p in range(pairs)]
        h = [h_scr[inst[j][0], inst[j][1] * pairs + p] for j, p in jp]
        y_blocks = [_bdot(pair_lanes(r_bar[j], p), h[q]) + pair_lanes(y0[j], p) for q, (j, p) in enumerate(jp)]
        bh_t = [pair_lanes(bh_st[j], p).T for j, p in jp]
        kh_t = [pair_lanes(kh_st[j], p).T for j, p in jp]
        pp = [_bdot(bh_t[q], jnp.concatenate([pair_lanes(aw[j], p), pair_lanes(aw[j], p, lanes)], axis=1))
              for q, (j, p) in enumerate(jp)]
        kv = [_bdot(kh_t[q], pair_lanes(v_st[j], p)) for q, (j, p) in enumerate(jp)]
        for q, (j, p) in enumerate(jp):
            i, s = inst[j]
            decay = pair_lanes(scaled[i]["total"], p, s * lanes)
            phi = jnp.where(diag, jnp.exp(decay), 0.0) + pp[q][:, :LANES]
            h_scr[i, s * pairs + p] = _bdot(phi, h[q]) + (pp[q][:, LANES:] + kv[q])
        return y_blocks

    all_inst = [(i, s) for i in range(seqs) for s in range(n_stacks)]
    yv = []
    for g0 in range(0, len(all_inst), group):
        yv += recur(all_inst[g0:g0 + group])

    per_seq = n_stacks * pairs
    y = jnp.concatenate([jnp.concatenate(yv[i * per_seq:(i + 1) * per_seq], axis=1) for i in range(seqs)], axis=0)
    mean = _head_sum(y, ones_ref) * (1.0 / hd)
    dev = y - mean
    var = _head_sum(dev * dev, ones_ref) * (1.0 / hd)
    yn = dev * lax.rsqrt(var + GN_EPS) * lnw_ref[...] + lnb_ref[...]
    o_ref[...] = ((yn + bonus_all) * g_all).astype(o_ref.dtype).reshape(o_ref.shape)

    @pl.when(c == pl.num_programs(1) - 1)
    def _():
        for i in range(seqs):
            for p in range(n_pairs):
                both = h_scr[i, p]
                hout_ref[i, 2 * p] = both[:hd, :hd]
                hout_ref[i, 2 * p + 1] = both[hd:, hd:]


def _rwkv(proj, shift_rows, h0, batch, seq_len, prm):
    rows = proj.shape[0]
    w, hd = prm["width"], prm["head_dim"]
    n_shift = prm["mu_shift"].shape[1]
    chunk = min(seq_len, RWKV_CHUNK)
    assert seq_len % chunk == 0 and chunk % SUBLANES == 0 and LANES % chunk == 0
    pairs = LANES // (2 * chunk)
    n_chunks = seq_len // chunk
    n_pairs = w // LANES
    target = RWKV_GROUP if n_chunks > 1 else 8
    seqs = 1
    while seqs * (n_pairs // pairs) < target and batch % (2 * seqs) == 0:
        seqs *= 2
    if n_chunks > 1:
        proj = proj.reshape(batch, seq_len, proj.shape[1])
        vec = lambda cols: pl.BlockSpec((seqs, chunk, cols), lambda bi, ci: (bi, ci, 0))
        out_shape = (batch, seq_len, w)
    else:
        vec = lambda cols: pl.BlockSpec((seqs * chunk, cols), lambda bi, ci: (bi, 0))
        out_shape = (rows, w)
    st = pl.BlockSpec((seqs, w // hd, hd, hd), lambda bi, ci: (bi, 0, 0, 0))
    row_spec = lambda c: pl.BlockSpec((1, c), lambda bi, ci: (0, 0))
    full = lambda a: pl.BlockSpec(a.shape, lambda bi, ci: (0, 0))
    a_out, h_new = pl.pallas_call(
        functools.partial(_rwkv_body, chunk=chunk, pairs=pairs, seqs=seqs, width=w, hd=hd, n_chunks=n_chunks,
                          group=RWKV_GROUP),
        grid=(batch // seqs, n_chunks),
        in_specs=[vec(n_shift), pl.BlockSpec((seqs * SUBLANES, n_shift), lambda bi, ci: (bi, 0)),
                  row_spec(n_shift), row_spec(w), full(prm["wd_pad"]), row_spec(w), full(prm["wa_pad"]),
                  full(prm["w_gate"]), row_spec(w), row_spec(w), row_spec(w), full(prm["ones2"]),
                  row_spec(w), row_spec(w), st],
        out_specs=[vec(w), st],
        out_shape=[jax.ShapeDtypeStruct(out_shape, BF16),
                   jax.ShapeDtypeStruct((batch, w // hd, hd, hd), F32)],
        scratch_shapes=[pltpu.VMEM((seqs, n_pairs, LANES, LANES), F32),
                        pltpu.VMEM((seqs, SUBLANES, n_shift), F32)],
        compiler_params=_params("parallel", "arbitrary"),
        name="rwkv",
    )(proj, shift_rows, prm["mu_shift"], prm["w0"], prm["wd_pad"], prm["a0"], prm["wa_pad"], prm["w_gate"],
      prm["k_k"], prm["k_a"], prm["r_k"], prm["ones2"], prm["lnx_w"], prm["lnx_b"], h0)
    return a_out.reshape(rows, w), h_new


def _sgu_body(gu_ref, gv_ref, sg_ref, sb_ref, wm_ref, bias_ref, o_ref, *maybe_v, groups, rows, seq_rows):
    u = jax.nn.gelu(gu_ref[...])
    vf = jax.nn.gelu(gv_ref[...])
    mu = jnp.mean(vf, axis=-1, keepdims=True)
    d = vf - mu
    var = jnp.mean(d * d, axis=-1, keepdims=True)
    v = (d * lax.rsqrt(var + LN_EPS)) * sg_ref[...] + sb_ref[...]
    if maybe_v:
        maybe_v[0][...] = v
    gd = v.shape[1] // groups
    ri = lax.broadcasted_iota(jnp.int32, (rows, rows), 0)
    ci = lax.broadcasted_iota(jnp.int32, (rows, rows), 1)
    causal = ri >= ci
    if seq_rows is not None:
        shift = seq_rows.bit_length() - 1
        causal = causal & ((ri >> shift) == (ci >> shift))
        pos = ((lax.broadcasted_iota(jnp.int32, (rows, seq_rows), 0) & (seq_rows - 1))
               == lax.broadcasted_iota(jnp.int32, (rows, seq_rows), 1)).astype(F32)
    for g in range(groups):
        sl = slice(g * gd, (g + 1) * gd)
        if seq_rows is None:
            w_full, bias = wm_ref[g], bias_ref[g]
        else:
            w_full = lax.dot_general(_fdot(pos, wm_ref[g][:seq_rows, :seq_rows]), pos, (((1,), (1,)), ((), ())),
                                     precision=HIGHEST, preferred_element_type=F32)
            bias = _fdot(pos, bias_ref[g])
        w_causal = jnp.where(causal, w_full, 0.0).astype(BF16)
        for blk in range(v.shape[0] // rows):
            rs = slice(blk * rows, (blk + 1) * rows)
            s = _bdot(w_causal, v[rs, sl]) + bias
            o_ref[rs, sl] = (u[rs, sl] * s).astype(o_ref.dtype)


def _col_window(rows_per_step, width, col0, row_of):
    return pl.BlockSpec((pl.Element(rows_per_step), pl.Element(width)),
                        lambda *idx: (pl.multiple_of(row_of(*idx) * rows_per_step, SUBLANES), col0))


def _sgu(proj, gu_col, w_mix, bias, prm, seq_rows, want_v_rows):
    rows = proj.shape[0]
    w = prm["width"]
    groups, chunk, _ = w_mix.shape
    mix_rows = chunk if seq_rows is None else _row_tile(rows, 2 * LANES)
    cs = mix_rows
    while cs < 512 and rows % (2 * cs) == 0:
        cs *= 2
    row_spec = pl.BlockSpec((1, w), lambda i: (0, 0))
    out = pl.BlockSpec((cs, w), lambda i: (i, 0))
    outs = pl.pallas_call(
        functools.partial(_sgu_body, groups=groups, rows=mix_rows, seq_rows=seq_rows),
        grid=(rows // cs,),
        in_specs=[_col_window(cs, w, gu_col, lambda i: i), _col_window(cs, w, gu_col + w, lambda i: i),
                  row_spec, row_spec,
                  pl.BlockSpec(w_mix.shape, lambda i: (0, 0, 0)), pl.BlockSpec(bias.shape, lambda i: (0, 0, 0))],
        out_specs=[out, out] if want_v_rows else [out],
        out_shape=[jax.ShapeDtypeStruct((rows, w), BF16)] + ([jax.ShapeDtypeStruct((rows, w), F32)]
                                                           if want_v_rows else []),
        compiler_params=_params("parallel"),
        name="sgu",
    )(proj, proj, prm["sgu_g"], prm["sgu_b"], w_mix, bias)
    return (outs[0], outs[1]) if want_v_rows else (outs[0], None)


def _xattn_body(q_ref, k_ref, v_ref, o_ref, *, heads, seqs, tq, m):
    hd = q_ref.shape[1] // heads
    scale = hd ** -0.5
    cases = [(h, i) for h in range(heads) for i in range(seqs)]
    cols = lambda h: slice(h * hd, (h + 1) * hd)
    mem = lambda i: slice(i * m, (i + 1) * m)
    s = [_bdot_nt(q_ref[i * tq:(i + 1) * tq, cols(h)], k_ref[mem(i), cols(h)]) * scale for h, i in cases]
    e = [jnp.exp(x - jnp.max(x, axis=-1, keepdims=True)) for x in s]
    p = [x / jnp.sum(x, axis=-1, keepdims=True) for x in e]
    o = [_bdot(p[c], v_ref[mem(i), cols(h)]) for c, (h, i) in enumerate(cases)]
    for h in range(heads):
        o_ref[:, cols(h)] = jnp.concatenate(o[h * seqs:(h + 1) * seqs], axis=0).astype(o_ref.dtype)


def _xattn_cache_body(q_ref, k_ref, v_ref, o_ref, *, heads, seqs, tq):
    hd = q_ref.shape[1] // heads
    m = k_ref.shape[1]
    n = m * heads
    scale = hd ** -0.5
    row_head = lax.broadcasted_iota(jnp.int32, (heads * tq, n), 0) >> (tq.bit_length() - 1)
    lane_head = lax.broadcasted_iota(jnp.int32, (heads * tq, n), 1) & (heads - 1)
    own = row_head == lane_head
    q = [jnp.concatenate([q_ref[i * tq:(i + 1) * tq, h * hd:(h + 1) * hd] for h in range(heads)], axis=0)
         for i in range(seqs)]
    s = [jnp.where(own, _bdot_nt(q[i], k_ref[i].reshape(n, hd)) * scale, -1e30) for i in range(seqs)]
    e = [jnp.exp(x - jnp.max(x, axis=-1, keepdims=True)) for x in s]
    p = [x / jnp.sum(x, axis=-1, keepdims=True) for x in e]
    o = [_bdot(p[i], v_ref[i].reshape(n, hd)) for i in range(seqs)]
    for h in range(heads):
        o_ref[:, h * hd:(h + 1) * hd] = jnp.concatenate(
            [o[i][h * tq:(h + 1) * tq] for i in range(seqs)], axis=0).astype(o_ref.dtype)


def _xattn(proj, q_col, w, mem_k, mem_v, batch, seq_len, heads):
    rows = proj.shape[0]
    tq = _row_tile(seq_len, 512)
    nq = seq_len // tq
    seqs = max(1, 32 // tq)
    assert batch % seqs == 0 and (seqs == 1 or nq == 1)
    if mem_k.ndim == 2:
        m = mem_k.shape[0] // batch
        kv = pl.BlockSpec((seqs * m, w), lambda bi, qi: (bi, 0))
        body = functools.partial(_xattn_body, heads=heads, seqs=seqs, tq=tq, m=m)
    else:
        assert tq & (tq - 1) == 0 and heads & (heads - 1) == 0
        kv = pl.BlockSpec((seqs,) + mem_k.shape[1:], lambda bi, qi: (bi, 0, 0, 0))
        body = functools.partial(_xattn_cache_body, heads=heads, seqs=seqs, tq=tq)
    return pl.pallas_call(
        body,
        grid=(batch // seqs, nq),
        in_specs=[_col_window(seqs * tq, w, q_col, lambda bi, qi: bi * nq + qi), kv, kv],
        out_specs=pl.BlockSpec((seqs * tq, w), lambda bi, qi: (bi * nq + qi, 0)),
        out_shape=jax.ShapeDtypeStruct((rows, w), BF16),
        compiler_params=_params("parallel", "parallel"),
        name="mem_xattn",
    )(proj, mem_k, mem_v)


def _gate_branch_body(*refs, rounding):
    if rounding:
        x_ref, a_ref, b_ref, c_ref, wg0_ref, wg1_ref, wg2_ref, wb_ref, o_ref, wg_bf, wb_bf = refs

        @pl.when(pl.program_id(1) == 0)
        def _():
            for n, src in enumerate((wg0_ref, wg1_ref, wg2_ref)):
                wg_bf[n] = src[...].astype(BF16)
            wb_bf[...] = wb_ref[...].astype(BF16)
    else:
        x_ref, a_ref, b_ref, c_ref, wg_bf, wb_bf, o_ref = refs

    x = x_ref[...]
    acc = None
    for n, br_ref in enumerate((a_ref, b_ref, c_ref)):
        gate = jax.nn.sigmoid(jnp.dot(x, wg_bf[n], preferred_element_type=F32))
        term = gate * jnp.dot(br_ref[...], wb_bf[n], preferred_element_type=F32)
        acc = term if acc is None else acc + term
    o_ref[...] = acc.astype(o_ref.dtype)


def _gate_branch_mix(xn, a_out, b_out, c_out, w_gates, gate_col0, w_branch):
    rows, k = xn.shape
    w = a_out.shape[1]
    nb, _, d = w_branch.shape
    assert nb == 3
    rounding = w_branch.dtype != BF16
    tm = _row_tile(rows, 1024)
    tn = _col_tile(d, 256)
    nj = d // tn
    act = lambda width: pl.BlockSpec((tm, width), lambda j, i: (i, 0))
    wb_spec = pl.BlockSpec((nb, w, tn), lambda j, i: (0, 0, j))
    wg_spec = pl.BlockSpec((nb, k, tn), lambda j, i: (0, 0, j))
    out_specs = [pl.BlockSpec((tm, tn), lambda j, i: (i, j))]
    out_shape = [jax.ShapeDtypeStruct((rows, d), BF16)]
    if rounding:
        gate_cols = lambda n: pl.BlockSpec(
            (pl.Element(k), pl.Element(tn)),
            lambda j, i: (0, pl.multiple_of(gate_col0 + n * d + j * tn, LANES)))
        w_specs = [gate_cols(0), gate_cols(1), gate_cols(2), wb_spec]
        w_args = (w_gates, w_gates, w_gates, w_branch)
        out_specs += [wg_spec, wb_spec]
        out_shape += [jax.ShapeDtypeStruct((nb, k, d), BF16), jax.ShapeDtypeStruct(w_branch.shape, BF16)]
    else:
        w_specs = [wg_spec, wb_spec]
        w_args = (w_gates, w_branch)
    outs = pl.pallas_call(
        functools.partial(_gate_branch_body, rounding=rounding),
        grid=(nj, rows // tm),
        in_specs=[act(k), act(w), act(w), act(w)] + w_specs,
        out_specs=out_specs,
        out_shape=out_shape,
        compiler_params=_params("parallel", "arbitrary"),
        name="gate_branch_mix",
    )(xn, a_out, b_out, c_out, *w_args)
    return outs[0], ((outs[1], outs[2]) if rounding else (w_gates, w_branch))


def _mm_res_norm_body(x_ref, w_ref, res_ref, g_ref, g2_ref, y_ref, *maybe_next, with_next):
    f = jnp.dot(x_ref[...], w_ref[...], preferred_element_type=F32)
    y = res_ref[...] + (f * lax.rsqrt(jnp.mean(f * f, axis=-1, keepdims=True) + RMS_EPS)) * g_ref[...]
    y_ref[...] = y
    if with_next:
        yn = y * lax.rsqrt(jnp.mean(y * y, axis=-1, keepdims=True) + RMS_EPS)
        maybe_next[0][...] = (yn * g2_ref[...]).astype(maybe_next[0].dtype)


def _mm_res_norm(x, w, res, g, g_next=None):
    rows, k = x.shape
    d = w.shape[1]
    with_next = g_next is not None
    tm = _row_tile(rows, 512 if k * d * 2 <= (12 << 20) else 256)
    blk = pl.BlockSpec((tm, d), lambda i: (i, 0))
    row_spec = pl.BlockSpec((1, d), lambda i: (0, 0))
    out_specs = [blk, blk] if with_next else [blk]
    out_shape = [jax.ShapeDtypeStruct((rows, d), F32)]
    if with_next:
        out_shape.append(jax.ShapeDtypeStruct((rows, d), BF16))
    g2 = g_next if with_next else g
    outs = pl.pallas_call(
        functools.partial(_mm_res_norm_body, with_next=with_next),
        grid=(rows // tm,),
        in_specs=[pl.BlockSpec((tm, k), lambda i: (i, 0)),
                  pl.BlockSpec((k, d), lambda i: (0, 0), pipeline_mode=pl.Buffered(1)),
                  blk, row_spec, row_spec],
        out_specs=out_specs,
        out_shape=out_shape,
        compiler_params=_params("arbitrary"),
        name="matmul_res_norm",
    )(x, w, res, g.reshape(1, d), g2.reshape(1, d))
    return outs if with_next else outs[0]


def _up_conv_gate_body(x_ref, wg_ref, wv_ref, eg_ref, ev_ref, cwg_ref, cwv_ref, cbg_ref, cbv_ref,
                       o_ref, tg_ref, tv_ref, *rest, seq_rows, tm, taps, rounding):
    i = pl.program_id(1)
    if rounding:
        wg_bf, wv_bf, keep_g, keep_v = rest
    else:
        keep_g, keep_v = rest
        wg_bf, wv_bf = wg_ref, wv_ref

    @pl.when(i == 0)
    def _():
        if rounding:
            wg_bf[...] = wg_ref[...].astype(BF16)
            wv_bf[...] = wv_ref[...].astype(BF16)
        keep_g[...] = jnp.zeros_like(keep_g)
        keep_v[...] = jnp.zeros_like(keep_v)

    if seq_rows >= tm:
        is_start = (i * tm) % seq_rows == 0
        tn = o_ref.shape[1]
        row8 = lax.broadcasted_iota(jnp.int32, (SUBLANES, tn), 0)

        def first_halo(e_ref, keep):
            start = jnp.zeros((SUBLANES, tn), F32)
            for idx in range(taps - 1):
                start = jnp.where(row8 == SUBLANES - (taps - 1) + idx, e_ref[0, idx:idx + 1, :], start)
            return jnp.where(is_start, start, keep[...])

        def conv_rows(u, halo, cw, cb):
            acc = cb + cw[taps - 1:taps] * u
            for back in range(1, taps):
                acc = acc + cw[taps - 1 - back:taps - back] * _shift_rows(u, halo, back)
            return acc

        halo_g, halo_v = first_halo(eg_ref, keep_g), first_halo(ev_ref, keep_v)
        cwg, cwv, cbg, cbv = cwg_ref[...], cwv_ref[...], cbg_ref[...], cbv_ref[...]
        xt = x_ref[...]
        ug = jnp.dot(xt, wg_bf[...], preferred_element_type=F32)
        uv = jnp.dot(xt, wv_bf[...], preferred_element_type=F32)
        o_ref[...] = (jax.nn.gelu(conv_rows(ug, halo_g, cwg, cbg)) * conv_rows(uv, halo_v, cwv, cbv)).astype(o_ref.dtype)
        for keep, tail_ref, u in ((keep_g, tg_ref, ug), (keep_v, tv_ref, uv)):
            keep[...] = u[tm - SUBLANES:tm]
            tail_ref[...] = u[tm - SUBLANES:tm]
        return

    x = x_ref[...]

    def conv(w_bf, e_ref, cw_ref, cb_ref, keep, tail_ref):
        u = jnp.dot(x, w_bf[...], preferred_element_type=F32)
        cw = cw_ref[...]
        acc = cb_ref[...] + cw[taps - 1:taps] * u
        n_lane_tiles = u.shape[1] // LANES
        t = lax.broadcasted_iota(jnp.int32, u.shape, 0) & (seq_rows - 1)
        for idx in range(taps - 1):
            for c in range(n_lane_tiles):
                keep[c, pl.ds(idx, tm // seq_rows, stride=seq_rows), :] = e_ref[:, idx, c * LANES:(c + 1) * LANES]
        e = jnp.concatenate([keep[c] for c in range(n_lane_tiles)], axis=1)
        for back in range(1, taps):
            up_by = taps - 1 - back
            state = pltpu.roll(e, tm - up_by, 0) if up_by else e
            prev = jnp.where(t < back, state, pltpu.roll(u, back, 0))
            acc = acc + cw[taps - 1 - back:taps - back] * prev
        for c in range(n_lane_tiles):
            keep[c] = u[:, c * LANES:(c + 1) * LANES]
        for idx in range(taps - 1):
            rows_t = pl.ds(seq_rows - (taps - 1) + idx, tm // seq_rows, stride=seq_rows)
            tail_ref[idx] = jnp.concatenate([keep[c, rows_t, :] for c in range(n_lane_tiles)], axis=1)
        return acc

    gate = conv(wg_bf, eg_ref, cwg_ref, cbg_ref, keep_g, tg_ref)
    val = conv(wv_bf, ev_ref, cwv_ref, cbv_ref, keep_v, tv_ref)
    o_ref[...] = (jax.nn.gelu(gate) * val).astype(o_ref.dtype)


def _up_conv_gate(x, w_up, state_rows, conv_w, conv_b, seq_rows):
    rows, d = x.shape
    rounding = not isinstance(w_up, tuple)
    f2 = conv_w.shape[1]
    dff = f2 // 2
    taps = conv_w.shape[0]
    tm = _row_tile(rows if seq_rows == SUBLANES else seq_rows, 1024)
    tn = _col_tile(dff, 512)
    nj = dff // tn
    if seq_rows >= tm:
        assert seq_rows % tm == 0
        st = lambda off: pl.BlockSpec((1, taps - 1, tn), lambda j, i: ((i * tm) // seq_rows, 0, j + off))
        tail = pl.BlockSpec((SUBLANES, tn), lambda j, i: ((i * tm) // seq_rows, j))
        tail_shape = ((rows // seq_rows) * SUBLANES, dff)
        keep_shape = (SUBLANES, tn)
    else:
        assert seq_rows == SUBLANES and tm % seq_rows == 0
        st = lambda off: pl.BlockSpec((tm // seq_rows, taps - 1, tn), lambda j, i: (i, 0, j + off))
        tail = pl.BlockSpec((taps - 1, tm // seq_rows, tn), lambda j, i: (0, i, j))
        tail_shape = (taps - 1, rows // seq_rows, dff)
        keep_shape = (tn // LANES, tm, LANES)
    wt = lambda off: pl.BlockSpec((d, tn), lambda j, i: (0, j + off))
    cw = lambda off: pl.BlockSpec((taps, tn), lambda j, i: (0, j + off))
    cb = lambda off: pl.BlockSpec((1, tn), lambda j, i: (0, j + off))
    out_specs = [pl.BlockSpec((tm, tn), lambda j, i: (i, j)), tail, tail]
    out_shape = [jax.ShapeDtypeStruct((rows, dff), BF16), jax.ShapeDtypeStruct(tail_shape, F32),
                 jax.ShapeDtypeStruct(tail_shape, F32)]
    if rounding:
        w_gate, w_val, w_specs = w_up, w_up, [wt(0), wt(nj)]
        out_specs += [wt(0), wt(0)]
        out_shape += [jax.ShapeDtypeStruct((d, dff), BF16)] * 2
    else:
        (w_gate, w_val), w_specs = w_up, [wt(0), wt(0)]
    outs = pl.pallas_call(
        functools.partial(_up_conv_gate_body, seq_rows=seq_rows, tm=tm, taps=taps, rounding=rounding),
        grid=(nj, rows // tm),
        in_specs=[pl.BlockSpec((tm, d), lambda j, i: (i, 0))] + w_specs + [st(0), st(nj),
                                                                             cw(0), cw(nj), cb(0), cb(nj)],
        out_specs=out_specs,
        out_shape=out_shape,
        scratch_shapes=[pltpu.VMEM(keep_shape, F32), pltpu.VMEM(keep_shape, F32)],
        compiler_params=_params("parallel", "arbitrary"),
        name="up_conv_gate",
    )(x, w_gate, w_val, state_rows, state_rows, conv_w, conv_w, conv_b.reshape(1, f2), conv_b.reshape(1, f2))
    act, tail_g, tail_v = outs[:3]
    w_bf = tuple(outs[3:]) if rounding else w_up
    tail = jnp.concatenate([tail_g, tail_v], axis=-1)
    if seq_rows >= tm:
        return act, tail.reshape(rows // seq_rows, SUBLANES, f2)[:, SUBLANES - (taps - 1):], w_bf
    return act, jnp.swapaxes(tail, 0, 1), w_bf


def _layer(x, shift_prev, wkv0, mem_k, mem_v, conv_prev, prm, want_v_rows):
    batch, seq_len, d = x.shape
    rows = batch * seq_len
    w = prm["width"]
    hd = prm["head_dim"]
    x2 = x.reshape(rows, d)

    xn = _rmsnorm(x2, prm["g_pre_mix"], BF16)
    n_shift = prm["mu_shift"].shape[1]
    proj = _matmul_cached(xn, prm, "w_in", 0, n_shift + 3 * w)
    new_shift = proj.reshape(batch, seq_len, -1)[:, -1, :n_shift]

    a_out, h_new = _rwkv(proj, _pad_state_rows(shift_prev[:, None, :]), jnp.swapaxes(wkv0, -1, -2), batch,
                         seq_len, prm)
    new_wkv = jnp.swapaxes(h_new, -1, -2)

    if seq_len % prm["sgu_chunk"] == 0:
        b_out, v_rows = _sgu(proj, n_shift, prm["w_s"], prm["sgu_bias"], prm, None, want_v_rows)
    else:
        assert seq_len == SUBLANES
        b_out, v_rows = _sgu(proj, n_shift, prm["w_s"], prm["sgu_bias"][:, :SUBLANES], prm, SUBLANES,
                             want_v_rows)
    c_out = _xattn(proj, n_shift + 2 * w, w, mem_k, mem_v, batch, seq_len, prm["xattn_heads"])

    mix, (prm["w_gates"], prm["w_branch"]) = _gate_branch_mix(
        xn, a_out, b_out, c_out, prm["w_gates"], n_shift + 3 * w, prm["w_branch"])
    h, hn = _mm_res_norm(mix, prm["w_out"], x2, prm["g_post_mix"], prm["g_pre_ffn"])

    act, conv_new, prm["w_up"] = _up_conv_gate(hn, prm["w_up"], conv_prev, prm["conv_w"],
                                               prm["conv_b"], seq_len)
    y_out = _mm_res_norm(act, prm["w_down"], h, prm["g_post_ffn"])
    if want_v_rows:
        v_rows = v_rows.reshape(batch, seq_len, w)
    return y_out.reshape(batch, seq_len, d), new_shift, new_wkv, v_rows, conv_new


def _prepare(l, g_pre_mix, w_in, mu_shift, w0, w_decay, a0, w_aaa, w_gate, k_k, k_a, r_k, lnx_w, lnx_b,
             sgu_g, sgu_b, w_s, b_s, w_branch, w_out, g_post_mix, g_pre_ffn, w_up, conv_w, conv_b, w_down,
             g_post_ffn):
    heads, hd = r_k.shape[1], r_k.shape[2]
    w = heads * hd
    d = w_in.shape[1]
    n_shift = mu_shift.shape[1]
    rank_d, rank_a, rank_g = w_decay.shape[1], w_aaa.shape[1], w_gate.shape[1]
    assert rank_d + rank_a == LANES and rank_g == LANES and n_shift == 3 * w + 2 * LANES
    groups, sgu_chunk, _ = w_s.shape[1:]
    row = lambda t: t[l].reshape(1, -1)
    lane_head = jnp.arange(LANES) // hd
    gd = w // groups
    return dict(
        width=w, head_dim=hd, sgu_chunk=sgu_chunk,
        g_pre_mix=g_pre_mix[l],
        w_in=w_in[l], w_gates=w_in[l],
        mu_shift=row(mu_shift), w0=row(w0), a0=row(a0),
        wd_pad=jnp.pad(w_decay[l], ((0, rank_a), (0, 0))),
        wa_pad=jnp.pad(w_aaa[l], ((rank_d, 0), (0, 0))),
        w_gate=w_gate[l],
        k_k=row(k_k), k_a=row(k_a), r_k=row(r_k), lnx_w=row(lnx_w), lnx_b=row(lnx_b),
        ones2=(lane_head[:, None] == lane_head[None, :]).astype(BF16),
        sgu_g=row(sgu_g), sgu_b=row(sgu_b),
        w_s=w_s[l],
        sgu_bias=jnp.broadcast_to(b_s[l][:, :, None], (groups, sgu_chunk, gd)),
        w_branch=w_branch[l], w_out=w_out[l].astype(BF16),
        g_post_mix=g_post_mix[l], g_pre_ffn=g_pre_ffn[l],
        w_up=w_up[l], conv_w=conv_w[l], conv_b=conv_b[l], w_down=w_down[l].astype(BF16),
        g_post_ffn=g_post_ffn[l],
    )


def kernel(x_prompt, x_sample, mem_prompt, state_wkv, state_shift, cache_mem_k, cache_mem_v, state_ffn_conv, g_pre_mix, w_in, mu_shift, w0, w_decay, a0, w_aaa, w_gate, k_k, k_a, r_k, lnx_w, lnx_b, sgu_g, sgu_b, w_s, b_s, g_mem, w_mem_k, w_mem_v, w_branch, w_out, g_post_mix, g_pre_ffn, w_up, conv_w, conv_b, w_down, g_post_ffn):
    depth = w_in.shape[0]
    batch = x_prompt.shape[0]
    mem_len, d = mem_prompt.shape[1], mem_prompt.shape[2]
    xh, xhd = cache_mem_k.shape[3], cache_mem_k.shape[4]
    heads, hd = r_k.shape[1], r_k.shape[2]
    n_shift = mu_shift.shape[1]
    f2 = w_up.shape[2]
    taps = conv_w.shape[1]
    y_p, y_s = x_prompt, x_sample
    outs = [[] for _ in range(9)]
    for l in range(depth):
        prm = _prepare(l, g_pre_mix, w_in, mu_shift, w0, w_decay, a0, w_aaa, w_gate, k_k, k_a, r_k, lnx_w, lnx_b,
                       sgu_g, sgu_b, w_s, b_s, w_branch, w_out, g_post_mix, g_pre_ffn, w_up, conv_w, conv_b,
                       w_down, g_post_ffn)
        mn = _rmsnorm(mem_prompt.reshape(batch * mem_len, d), g_mem[l], BF16)
        prm["xattn_heads"] = xh
        mk_rows = _matmul(mn, w_mem_k[l])[0]
        mv_rows = _matmul(mn, w_mem_v[l])[0]
        mk_p = mk_rows.reshape(batch, mem_len, xh, xhd)
        mv_p = mv_rows.reshape(batch, mem_len, xh, xhd)
        y_p, sh_p, wkv_p, _, cv_p = _layer(
            y_p, jnp.zeros((batch, n_shift), F32), jnp.zeros((batch, heads, hd, hd), F32), mk_rows, mv_rows,
            jnp.zeros((batch, taps - 1, f2), F32), prm, False)
        y_s, sh_s, wkv_s, vr_s, cv_s = _layer(y_s, state_shift[l], state_wkv[l], cache_mem_k[l], cache_mem_v[l],
                                              state_ffn_conv[l], prm, True)
        for lst, val in zip(outs, (wkv_p, sh_p, mk_p, mv_p, cv_p, wkv_s, sh_s, vr_s, cv_s)):
            lst.append(val)
    return (y_p, y_s) + tuple(jnp.stack(lst) for lst in outs)
```

```python
import functools

import jax
import jax.numpy as jnp
from jax import lax
from jax.experimental import pallas as pl
from jax.experimental.pallas import tpu as pltpu

F32 = jnp.float32
BF16 = jnp.bfloat16
HIGHEST = lax.Precision.HIGHEST

LANES = 128
SUBLANES = 8
VMEM_LIMIT_BYTES = 56 * 1024 * 1024

RMS_EPS = 1e-6
LN_EPS = 1e-5
GN_EPS = 64e-5
KK_EPS = 1e-12
RWKV_CHUNK = 64
RWKV_GROUP = 32


def _params(*semantics):
    return pltpu.CompilerParams(dimension_semantics=semantics, vmem_limit_bytes=VMEM_LIMIT_BYTES)


def _row_tile(rows, pref):
    t = min(rows, pref)
    while rows % t:
        t -= SUBLANES
    return t


def _col_tile(n, cap):
    best = n
    for t in range(LANES, min(n, cap) + 1, LANES):
        if n % t == 0:
            best = t
    return best


def _bdot(a, b):
    return jnp.dot(a.astype(BF16), b.astype(BF16), preferred_element_type=F32)


def _bdot_nt(a, b):
    return lax.dot_general(a.astype(BF16), b.astype(BF16), (((1,), (1,)), ((), ())),
                           preferred_element_type=F32)


def _fdot(a, b):
    return jnp.dot(a, b, precision=HIGHEST, preferred_element_type=F32)


def _dot3(a, b):
    a_hi = a.astype(BF16)
    b_hi = b.astype(BF16)
    a_lo = (a - a_hi.astype(F32)).astype(BF16)
    b_lo = (b - b_hi.astype(F32)).astype(BF16)
    dot = lambda x, y: jnp.dot(x, y, preferred_element_type=F32)
    return dot(a_hi, b_hi) + (dot(a_hi, b_lo) + dot(a_lo, b_hi))


def _rmsnorm_body(x_ref, g_ref, o_ref):
    x = x_ref[...]
    y = x * lax.rsqrt(jnp.mean(x * x, axis=-1, keepdims=True) + RMS_EPS)
    o_ref[...] = (y * g_ref[...]).astype(o_ref.dtype)


def _rmsnorm(x, g, out_dtype):
    rows, d = x.shape
    tm = _row_tile(rows, 512)
    return pl.pallas_call(
        _rmsnorm_body,
        grid=(rows // tm,),
        in_specs=[pl.BlockSpec((tm, d), lambda i: (i, 0)), pl.BlockSpec((1, d), lambda i: (0, 0))],
        out_specs=pl.BlockSpec((tm, d), lambda i: (i, 0)),
        out_shape=jax.ShapeDtypeStruct((rows, d), out_dtype),
        compiler_params=_params("parallel"),
        name="rmsnorm",
    )(x, g.reshape(1, d))


def _matmul_round_body(x_ref, w_ref, o_ref, wb_ref):
    @pl.when(pl.program_id(1) == 0)
    def _():
        wb_ref[...] = w_ref[...].astype(BF16)

    o_ref[...] = jnp.dot(x_ref[...], wb_ref[...], preferred_element_type=F32).astype(o_ref.dtype)


def _matmul_body(x_ref, w_ref, o_ref):
    o_ref[...] = jnp.dot(x_ref[...], w_ref[...], preferred_element_type=F32).astype(o_ref.dtype)


def _matmul(x, w, col_start=0, n_cols=None, out_dtype=F32):
    rows, k = x.shape
    n = w.shape[1] - col_start if n_cols is None else n_cols
    assert col_start % LANES == 0 and n % LANES == 0
    rounding = w.dtype != BF16
    tn = _col_tile(n, 1792)
    budget = VMEM_LIMIT_BYTES - (4 << 20)
    tm = _row_tile(rows, 1024)
    w_bytes = 2 * k * tn * (4 + 2) if rounding else 2 * k * tn * 2
    while w_bytes + 2 * tm * k * 2 + 2 * tm * tn * 4 > budget and tm % (2 * SUBLANES) == 0:
        tm //= 2
    w_spec = pl.BlockSpec((pl.Element(k), pl.Element(tn)),
                          lambda j, i: (0, pl.multiple_of(col_start + j * tn, LANES)))
    out_spec = pl.BlockSpec((tm, tn), lambda j, i: (i, j))
    out_shape = jax.ShapeDtypeStruct((rows, n), out_dtype)
    if rounding:
        out_spec = [out_spec, pl.BlockSpec((k, tn), lambda j, i: (0, j))]
        out_shape = [out_shape, jax.ShapeDtypeStruct((k, n), BF16)]
    return pl.pallas_call(
        _matmul_round_body if rounding else _matmul_body,
        grid=(n // tn, rows // tm),
        in_specs=[pl.BlockSpec((tm, k), lambda j, i: (i, 0)), w_spec],
        out_specs=out_spec,
        out_shape=out_shape,
        compiler_params=_params("parallel", "arbitrary"),
        name="matmul",
    )(x, w)


def _matmul_cached(x, prm, name, col_start=0, n_cols=None):
    key = (name, col_start)
    if key in prm:
        return _matmul(x, prm[key])
    out, prm[key] = _matmul(x, prm[name], col_start, n_cols)
    return out


def _shift_rows(x, halo, k):
    rolled = pltpu.roll(x, k, 0)
    row = lax.broadcasted_iota(jnp.int32, (SUBLANES, x.shape[1]), 0)
    top = rolled[0:SUBLANES]
    for j in range(k):
        top = jnp.where(row == j, halo[SUBLANES - k + j:SUBLANES - k + j + 1, :], top)
    return jnp.concatenate([top, rolled[SUBLANES:]], axis=0)


def _head_sum_operand(x):
    hi = x.astype(BF16)
    lo = (x - hi.astype(F32)).astype(BF16)
    tiles = x.shape[1] // LANES
    return jnp.concatenate([part[:, j * LANES:(j + 1) * LANES] for part in (hi, lo) for j in range(tiles)], axis=0)


def _head_sum_result(s, n, tiles):
    return jnp.concatenate([s[j * n:(j + 1) * n] + s[(tiles + j) * n:(tiles + j + 1) * n] for j in range(tiles)],
                           axis=1)


def _head_sum(x, ones_ref):
    s = jnp.dot(_head_sum_operand(x), ones_ref[...], preferred_element_type=F32)
    return _head_sum_result(s, x.shape[0], x.shape[1] // LANES)


def _pad_state_rows(state):
    b, n, f = state.shape
    return jnp.pad(state, ((0, 0), (0, SUBLANES - n), (0, 0))).reshape(b * SUBLANES, f)


def _rwkv_inputs(x, prev, mu_ref, w0_ref, wd_ref, a0_ref, wa_ref, wg_ref, kk_ref, ka_ref, rk_ref, ones_ref, width):
    xs = x + mu_ref[...] * (prev - x)
    w = width
    n = x.shape[0]
    r = xs[:, 0:w]
    k = xs[:, w:2 * w]
    v = xs[:, 2 * w:3 * w]
    x_wa = xs[:, 3 * w:3 * w + LANES]
    x_g = xs[:, 3 * w + LANES:3 * w + 2 * LANES]

    z = -(w0_ref[...] + _dot3(jnp.tanh(x_wa), wd_ref[...]))
    softplus = jnp.maximum(z, 0.0) + jnp.log(1.0 + jnp.exp(-jnp.abs(z)))
    lw = -jnp.exp(-softplus - 0.5)
    a_lr = jax.nn.sigmoid(a0_ref[...] + _bdot(x_wa, wa_ref[...]))
    g = _bdot(jax.nn.sigmoid(x_g), wg_ref[...])

    kk = k * kk_ref[...]
    k_mod = k * (1.0 + (a_lr - 1.0) * ka_ref[...])
    sums = _head_sum(jnp.concatenate([kk * kk, r * k_mod * rk_ref[...]], axis=0), ones_ref)
    kk = kk / jnp.maximum(jnp.sqrt(sums[:n]), KK_EPS)
    return r, lw, k_mod, v, -kk, kk * a_lr, g, sums[n:] * v


def _rwkv_body(rw_ref, sp_ref, mu_ref, w0_ref, wd_ref, a0_ref, wa_ref, wg_ref, kk_ref, ka_ref, rk_ref, ones_ref,
               lnw_ref, lnb_ref, h0_ref, o_ref, hout_ref, h_scr, carry, *, chunk, pairs, seqs, width, hd, n_chunks, group):
    c = pl.program_id(1)
    n_pairs = width // LANES

    @pl.when(c == 0)
    def _():
        zero = jnp.zeros((hd, hd), F32)
        for i in range(seqs):
            for p in range(n_pairs):
                h_scr[i, p] = jnp.concatenate(
                    [jnp.concatenate([h0_ref[i, 2 * p], zero], axis=1),
                     jnp.concatenate([zero, h0_ref[i, 2 * p + 1]], axis=1)], axis=0)
        carry[...] = jnp.zeros_like(carry)

    if n_chunks > 1:
        x = rw_ref[...].reshape(seqs * chunk, rw_ref.shape[2])
        prevs = []
        for i in range(seqs):
            xi = x[i * chunk:(i + 1) * chunk]
            before = jnp.where(c == 0, pltpu.roll(sp_ref[i * SUBLANES:(i + 1) * SUBLANES, :], SUBLANES - 1, 0),
                               carry[i])
            prevs.append(_shift_rows(xi, before, 1))
            carry[i] = xi[chunk - SUBLANES:]
        prev = jnp.concatenate(prevs, axis=0)
    else:
        x = rw_ref[...]
        t = lax.broadcasted_iota(jnp.int32, x.shape, 0) & (chunk - 1)
        prev = jnp.where(t == 0, sp_ref[...], pltpu.roll(x, 1, 0))
    r_all, lw_all, k_all, v_all, a_all, b_all, g_all, bonus_all = _rwkv_inputs(
        x, prev, mu_ref, w0_ref, wd_ref, a0_ref, wa_ref, wg_ref, kk_ref, ka_ref, rk_ref, ones_ref, width)

    cs = chunk
    lanes = LANES * pairs
    heads = 2 * pairs
    n = heads * cs
    log_c = cs.bit_length() - 1
    log_hd = hd.bit_length() - 1
    n_stacks = width // lanes

    tri = (lax.broadcasted_iota(jnp.int32, (cs, cs), 0) >= lax.broadcasted_iota(jnp.int32, (cs, cs), 1)).astype(F32)
    row = lax.broadcasted_iota(jnp.int32, (n, lanes), 0)
    lane = lax.broadcasted_iota(jnp.int32, (n, lanes), 1)
    head_mask = (row >> log_c) == (lane >> log_hd)
    ri = lax.broadcasted_iota(jnp.int32, (n, n), 0)
    ci = lax.broadcasted_iota(jnp.int32, (n, n), 1)
    same_head = (ri >> log_c) == (ci >> log_c)
    rt = ri & (cs - 1)
    ct = ci & (cs - 1)
    strict = same_head & (ct < rt)
    incl = same_head & (ct <= rt)
    eye = (ri == ci).astype(F32)
    diag = (lax.broadcasted_iota(jnp.int32, (LANES, LANES), 0)
            == lax.broadcasted_iota(jnp.int32, (LANES, LANES), 1))

    def stack(x):
        return jnp.where(head_mask, jnp.concatenate([x] * heads, axis=0), 0.0)

    def fold(x):
        out = x[0:cs]
        for hidx in range(1, heads):
            out = out + x[hidx * cs:(hidx + 1) * cs]
        return out

    scaled = []
    for i in range(seqs):
        rs = slice(i * cs, (i + 1) * cs)
        lw = lw_all[rs]
        cum = _fdot(tri, lw)
        total = cum[cs - 1:cs]
        e_cum = jnp.exp(cum)
        e_neg = jnp.exp(-cum)
        e_rest = jnp.exp(total - cum)
        b = b_all[rs]
        k = k_all[rs]
        scaled.append(dict(a=a_all[rs] * jnp.exp(cum - lw), r=r_all[rs] * e_cum, b=b * e_neg, k=k * e_neg,
                           v=v_all[rs], bh=b * e_rest, kh=k * e_rest, total=total))

    def pair_lanes(x, p, off=0):
        return x[:, off + p * LANES:off + (p + 1) * LANES]

    def recur(inst):
        jj = range(len(inst))

        def stacked(name):
            return [stack(scaled[i][name][:, s * lanes:(s + 1) * lanes]) for i, s in inst]

        a_st, r_st, b_st, k_st, v_st, bh_st, kh_st = (stacked(nm) for nm in ("a", "r", "b", "k", "v", "bh", "kh"))

        gram = [_bdot_nt(jnp.concatenate([a_st[j], r_st[j]], axis=0), jnp.concatenate([b_st[j], k_st[j]], axis=0))
                for j in jj]
        l_ab = [jnp.where(strict, gram[j][:n, :n], 0.0) for j in jj]
        m_rb = [jnp.where(incl, gram[j][n:, :n], 0.0) for j in jj]
        l_ak_m_rk = [jnp.concatenate([jnp.where(strict, gram[j][:n, n:], 0.0),
                                      jnp.where(incl, gram[j][n:, n:], 0.0)], axis=0) for j in jj]

        t_inv = [eye + l_ab[j] for j in jj]
        x_pow = [_bdot(l_ab[j], l_ab[j]) for j in jj]
        for _ in range(log_c - 2):
            both = [_bdot(jnp.concatenate([x_pow[j], t_inv[j]], axis=0), x_pow[j]) for j in jj]
            t_inv = [t_inv[j] + both[j][n:] for j in jj]
            x_pow = [both[j][:n] for j in jj]
        t_inv = [t_inv[j] + _bdot(t_inv[j], x_pow[j]) for j in jj]

        lv = [_bdot(l_ak_m_rk[j], v_st[j]) for j in jj]
        aw = [_bdot(t_inv[j], jnp.concatenate([a_st[j], lv[j][:n]], axis=1)) for j in jj]
        rb = [_bdot(m_rb[j], aw[j]) for j in jj]
        r_bar = [fold(r_st[j] + rb[j][:, :lanes]) for j in jj]
        y0 = [fold(rb[j][:, lanes:] + lv[j][n:]) for j in jj]

        jp = [(j, p) for j in jj for p in range(pairs)]
        h = [h_scr[inst[j][0], inst[j][1] * pairs + p] for j, p in jp]
        y_blocks = [_bdot(pair_lanes(r_bar[j], p), h[q]) + pair_lanes(y0[j], p) for q, (j, p) in enumerate(jp)]
        bh_t = [pair_lanes(bh_st[j], p).T for j, p in jp]
        kh_t = [pair_lanes(kh_st[j], p).T for j, p in jp]
        pp = [_bdot(bh_t[q], jnp.concatenate([pair_lanes(aw[j], p), pair_lanes(aw[j], p, lanes)], axis=1))
              for q, (j, p) in enumerate(jp)]
        kv = [_bdot(kh_t[q], pair_lanes(v_st[j], p)) for q, (j, p) in enumerate(jp)]
        for q, (j, p) in enumerate(jp):
            i, s = inst[j]
            decay = pair_lanes(scaled[i]["total"], p, s * lanes)
            phi = jnp.where(diag, jnp.exp(decay), 0.0) + pp[q][:, :LANES]
            h_scr[i, s * pairs + p] = _bdot(phi, h[q]) + (pp[q][:, LANES:] + kv[q])
        return y_blocks

    all_inst = [(i, s) for i in range(seqs) for s in range(n_stacks)]
    yv = []
    for g0 in range(0, len(all_inst), group):
        yv += recur(all_inst[g0:g0 + group])

    per_seq = n_stacks * pairs
    y = jnp.concatenate([jnp.concatenate(yv[i * per_seq:(i + 1) * per_seq], axis=1) for i in range(seqs)], axis=0)
    mean = _head_sum(y, ones_ref) * (1.0 / hd)
    dev = y - mean
    var = _head_sum(dev * dev, ones_ref) * (1.0 / hd)
    yn = dev * lax.rsqrt(var + GN_EPS) * lnw_ref[...] + lnb_ref[...]
    o_ref[...] = ((yn + bonus_all) * g_all).astype(o_ref.dtype).reshape(o_ref.shape)

    @pl.when(c == pl.num_programs(1) - 1)
    def _():
        for i in range(seqs):
            for p in range(n_pairs):
                both = h_scr[i, p]
                hout_ref[i, 2 * p] = both[:hd, :hd]
                hout_ref[i, 2 * p + 1] = both[hd:, hd:]


def _rwkv(proj, shift_rows, h0, batch, seq_len, prm):
    rows = proj.shape[0]
    w, hd = prm["width"], prm["head_dim"]
    n_shift = prm["mu_shift"].shape[1]
    chunk = min(seq_len, RWKV_CHUNK)
    assert seq_len % chunk == 0 and chunk % SUBLANES == 0 and LANES % chunk == 0
    pairs = LANES // (2 * chunk)
    n_chunks = seq_len // chunk
    n_pairs = w // LANES
    target = RWKV_GROUP if n_chunks > 1 else 8
    seqs = 1
    while seqs * (n_pairs // pairs) < target and batch % (2 * seqs) == 0:
        seqs *= 2
    if n_chunks > 1:
        proj = proj.reshape(batch, seq_len, proj.shape[1])
        vec = lambda cols: pl.BlockSpec((seqs, chunk, cols), lambda bi, ci: (bi, ci, 0))
        out_shape = (batch, seq_len, w)
    else:
        vec = lambda cols: pl.BlockSpec((seqs * chunk, cols), lambda bi, ci: (bi, 0))
        out_shape = (rows, w)
    st = pl.BlockSpec((seqs, w // hd, hd, hd), lambda bi, ci: (bi, 0, 0, 0))
    row_spec = lambda c: pl.BlockSpec((1, c), lambda bi, ci: (0, 0))
    full = lambda a: pl.BlockSpec(a.shape, lambda bi, ci: (0, 0))
    a_out, h_new = pl.pallas_call(
        functools.partial(_rwkv_body, chunk=chunk, pairs=pairs, seqs=seqs, width=w, hd=hd, n_chunks=n_chunks,
                          group=RWKV_GROUP),
        grid=(batch // seqs, n_chunks),
        in_specs=[vec(n_shift), pl.BlockSpec((seqs * SUBLANES, n_shift), lambda bi, ci: (bi, 0)),
                  row_spec(n_shift), row_spec(w), full(prm["wd_pad"]), row_spec(w), full(prm["wa_pad"]),
                  full(prm["w_gate"]), row_spec(w), row_spec(w), row_spec(w), full(prm["ones2"]),
                  row_spec(w), row_spec(w), st],
        out_specs=[vec(w), st],
        out_shape=[jax.ShapeDtypeStruct(out_shape, BF16),
                   jax.ShapeDtypeStruct((batch, w // hd, hd, hd), F32)],
        scratch_shapes=[pltpu.VMEM((seqs, n_pairs, LANES, LANES), F32),
                        pltpu.VMEM((seqs, SUBLANES, n_shift), F32)],
        compiler_params=_params("parallel", "arbitrary"),
        name="rwkv",
    )(proj, shift_rows, prm["mu_shift"], prm["w0"], prm["wd_pad"], prm["a0"], prm["wa_pad"], prm["w_gate"],
      prm["k_k"], prm["k_a"], prm["r_k"], prm["ones2"], prm["lnx_w"], prm["lnx_b"], h0)
    return a_out.reshape(rows, w), h_new


def _sgu_body(gu_ref, gv_ref, sg_ref, sb_ref, wm_ref, bias_ref, o_ref, *maybe_v, groups, rows, seq_rows):
    u = jax.nn.gelu(gu_ref[...])
    vf = jax.nn.gelu(gv_ref[...])
    mu = jnp.mean(vf, axis=-1, keepdims=True)
    d = vf - mu
    var = jnp.mean(d * d, axis=-1, keepdims=True)
    v = (d * lax.rsqrt(var + LN_EPS)) * sg_ref[...] + sb_ref[...]
    if maybe_v:
        maybe_v[0][...] = v
    gd = v.shape[1] // groups
    ri = lax.broadcasted_iota(jnp.int32, (rows, rows), 0)
    ci = lax.broadcasted_iota(jnp.int32, (rows, rows), 1)
    causal = ri >= ci
    if seq_rows is not None:
        shift = seq_rows.bit_length() - 1
        causal = causal & ((ri >> shift) == (ci >> shift))
        pos = ((lax.broadcasted_iota(jnp.int32, (rows, seq_rows), 0) & (seq_rows - 1))
               == lax.broadcasted_iota(jnp.int32, (rows, seq_rows), 1)).astype(F32)
    for g in range(groups):
        sl = slice(g * gd, (g + 1) * gd)
        if seq_rows is None:
            w_full, bias = wm_ref[g], bias_ref[g]
        else:
            w_full = lax.dot_general(_fdot(pos, wm_ref[g][:seq_rows, :seq_rows]), pos, (((1,), (1,)), ((), ())),
                                     precision=HIGHEST, preferred_element_type=F32)
            bias = _fdot(pos, bias_ref[g])
        w_causal = jnp.where(causal, w_full, 0.0).astype(BF16)
        for blk in range(v.shape[0] // rows):
            rs = slice(blk * rows, (blk + 1) * rows)
            s = _bdot(w_causal, v[rs, sl]) + bias
            o_ref[rs, sl] = (u[rs, sl] * s).astype(o_ref.dtype)


def _col_window(rows_per_step, width, col0, row_of):
    return pl.BlockSpec((pl.Element(rows_per_step), pl.Element(width)),
                        lambda *idx: (pl.multiple_of(row_of(*idx) * rows_per_step, SUBLANES), col0))


def _sgu(proj, gu_col, w_mix, bias, prm, seq_rows, want_v_rows):
    rows = proj.shape[0]
    w = prm["width"]
    groups, chunk, _ = w_mix.shape
    mix_rows = chunk if seq_rows is None else _row_tile(rows, 2 * LANES)
    cs = mix_rows
    while cs < 512 and rows % (2 * cs) == 0:
        cs *= 2
    row_spec = pl.BlockSpec((1, w), lambda i: (0, 0))
    out = pl.BlockSpec((cs, w), lambda i: (i, 0))
    outs = pl.pallas_call(
        functools.partial(_sgu_body, groups=groups, rows=mix_rows, seq_rows=seq_rows),
        grid=(rows // cs,),
        in_specs=[_col_window(cs, w, gu_col, lambda i: i), _col_window(cs, w, gu_col + w, lambda i: i),
                  row_spec, row_spec,
                  pl.BlockSpec(w_mix.shape, lambda i: (0, 0, 0)), pl.BlockSpec(bias.shape, lambda i: (0, 0, 0))],
        out_specs=[out, out] if want_v_rows else [out],
        out_shape=[jax.ShapeDtypeStruct((rows, w), BF16)] + ([jax.ShapeDtypeStruct((rows, w), F32)]
                                                           if want_v_rows else []),
        compiler_params=_params("parallel"),
        name="sgu",
    )(proj, proj, prm["sgu_g"], prm["sgu_b"], w_mix, bias)
    return (outs[0], outs[1]) if want_v_rows else (outs[0], None)


def _xattn_body(q_ref, k_ref, v_ref, o_ref, *, heads, seqs, tq, m):
    hd = q_ref.shape[1] // heads
    scale = hd ** -0.5
    cases = [(h, i) for h in range(heads) for i in range(seqs)]
    cols = lambda h: slice(h * hd, (h + 1) * hd)
    mem = lambda i: slice(i * m, (i + 1) * m)
    s = [_bdot_nt(q_ref[i * tq:(i + 1) * tq, cols(h)], k_ref[mem(i), cols(h)]) * scale for h, i in cases]
    e = [jnp.exp(x - jnp.max(x, axis=-1, keepdims=True)) for x in s]
    p = [x / jnp.sum(x, axis=-1, keepdims=True) for x in e]
    o = [_bdot(p[c], v_ref[mem(i), cols(h)]) for c, (h, i) in enumerate(cases)]
    for h in range(heads):
        o_ref[:, cols(h)] = jnp.concatenate(o[h * seqs:(h + 1) * seqs], axis=0).astype(o_ref.dtype)


def _xattn_cache_body(q_ref, k_ref, v_ref, o_ref, *, heads, seqs, tq):
    hd = q_ref.shape[1] // heads
    m = k_ref.shape[1]
    n = m * heads
    scale = hd ** -0.5
    row_head = lax.broadcasted_iota(jnp.int32, (heads * tq, n), 0) >> (tq.bit_length() - 1)
    lane_head = lax.broadcasted_iota(jnp.int32, (heads * tq, n), 1) & (heads - 1)
    own = row_head == lane_head
    q = [jnp.concatenate([q_ref[i * tq:(i + 1) * tq, h * hd:(h + 1) * hd] for h in range(heads)], axis=0)
         for i in range(seqs)]
    s = [jnp.where(own, _bdot_nt(q[i], k_ref[i].reshape(n, hd)) * scale, -1e30) for i in range(seqs)]
    e = [jnp.exp(x - jnp.max(x, axis=-1, keepdims=True)) for x in s]
    p = [x / jnp.sum(x, axis=-1, keepdims=True) for x in e]
    o = [_bdot(p[i], v_ref[i].reshape(n, hd)) for i in range(seqs)]
    for h in range(heads):
        o_ref[:, h * hd:(h + 1) * hd] = jnp.concatenate(
            [o[i][h * tq:(h + 1) * tq] for i in range(seqs)], axis=0).astype(o_ref.dtype)


def _xattn(proj, q_col, w, mem_k, mem_v, batch, seq_len, heads):
    rows = proj.shape[0]
    tq = _row_tile(seq_len, 512)
    nq = seq_len // tq
    seqs = max(1, 32 // tq)
    assert batch % seqs == 0 and (seqs == 1 or nq == 1)
    if mem_k.ndim == 2:
        m = mem_k.shape[0] // batch
        kv = pl.BlockSpec((seqs * m, w), lambda bi, qi: (bi, 0))
        body = functools.partial(_xattn_body, heads=heads, seqs=seqs, tq=tq, m=m)
    else:
        assert tq & (tq - 1) == 0 and heads & (heads - 1) == 0
        kv = pl.BlockSpec((seqs,) + mem_k.shape[1:], lambda bi, qi: (bi, 0, 0, 0))
        body = functools.partial(_xattn_cache_body, heads=heads, seqs=seqs, tq=tq)
    return pl.pallas_call(
        body,
        grid=(batch // seqs, nq),
        in_specs=[_col_window(seqs * tq, w, q_col, lambda bi, qi: bi * nq + qi), kv, kv],
        out_specs=pl.BlockSpec((seqs * tq, w), lambda bi, qi: (bi * nq + qi, 0)),
        out_shape=jax.ShapeDtypeStruct((rows, w), BF16),
        compiler_params=_params("parallel", "parallel"),
        name="mem_xattn",
    )(proj, mem_k, mem_v)


def _slab_spec(weight, grid_steps, step_of):
    k, d = weight.shape
    if weight.dtype == BF16 or k % grid_steps or (k // grid_steps) % (2 * SUBLANES):
        return None
    return pl.BlockSpec((k // grid_steps, d), lambda j, i: (step_of(j, i), 0))


def _gate_branch_body(*refs, rounding, with_next):
    if with_next:
        *refs, next_bf = refs
        next_ref = refs.pop(8)
        next_bf[...] = next_ref[...].astype(BF16)
    if rounding:
        x_ref, a_ref, b_ref, c_ref, wg0_ref, wg1_ref, wg2_ref, wb_ref, o_ref, wg_bf, wb_bf = refs

        @pl.when(pl.program_id(1) == 0)
        def _():
            for n, src in enumerate((wg0_ref, wg1_ref, wg2_ref)):
                wg_bf[n] = src[...].astype(BF16)
            wb_bf[...] = wb_ref[...].astype(BF16)
    else:
        x_ref, a_ref, b_ref, c_ref, wg_bf, wb_bf, o_ref = refs

    x = x_ref[...]
    acc = None
    for n, br_ref in enumerate((a_ref, b_ref, c_ref)):
        gate = jax.nn.sigmoid(jnp.dot(x, wg_bf[n], preferred_element_type=F32))
        term = gate * jnp.dot(br_ref[...], wb_bf[n], preferred_element_type=F32)
        acc = term if acc is None else acc + term
    o_ref[...] = acc.astype(o_ref.dtype)


def _gate_branch_mix(xn, a_out, b_out, c_out, w_gates, gate_col0, w_branch, w_next):
    rows, k = xn.shape
    w = a_out.shape[1]
    nb, _, d = w_branch.shape
    assert nb == 3
    rounding = w_branch.dtype != BF16
    tm = _row_tile(rows, 1024)
    tn = _col_tile(d, 256)
    nj = d // tn
    act = lambda width: pl.BlockSpec((tm, width), lambda j, i: (i, 0))
    wb_spec = pl.BlockSpec((nb, w, tn), lambda j, i: (0, 0, j))
    wg_spec = pl.BlockSpec((nb, k, tn), lambda j, i: (0, 0, j))
    out_specs = [pl.BlockSpec((tm, tn), lambda j, i: (i, j))]
    out_shape = [jax.ShapeDtypeStruct((rows, d), BF16)]
    if rounding:
        gate_cols = lambda n: pl.BlockSpec(
            (pl.Element(k), pl.Element(tn)),
            lambda j, i: (0, pl.multiple_of(gate_col0 + n * d + j * tn, LANES)))
        w_specs = [gate_cols(0), gate_cols(1), gate_cols(2), wb_spec]
        w_args = (w_gates, w_gates, w_gates, w_branch)
        out_specs += [wg_spec, wb_spec]
        out_shape += [jax.ShapeDtypeStruct((nb, k, d), BF16), jax.ShapeDtypeStruct(w_branch.shape, BF16)]
    else:
        w_specs = [wg_spec, wb_spec]
        w_args = (w_gates, w_branch)
    n_i = rows // tm
    slab = _slab_spec(w_next, nj * n_i, lambda j, i: j * n_i + i) if rounding else None
    if slab is not None:
        w_specs, w_args = w_specs + [slab], w_args + (w_next,)
        out_specs.append(slab)
        out_shape.append(jax.ShapeDtypeStruct(w_next.shape, BF16))
    outs = pl.pallas_call(
        functools.partial(_gate_branch_body, rounding=rounding, with_next=slab is not None),
        grid=(nj, n_i),
        in_specs=[act(k), act(w), act(w), act(w)] + w_specs,
        out_specs=out_specs,
        out_shape=out_shape,
        compiler_params=_params("parallel", "arbitrary"),
        name="gate_branch_mix",
    )(xn, a_out, b_out, c_out, *w_args)
    w_next_bf = outs[-1] if slab is not None else w_next.astype(BF16)
    return outs[0], ((outs[1], outs[2]) if rounding else (w_gates, w_branch)), w_next_bf


def _mm_res_norm_body(x_ref, w_ref, res_ref, g_ref, g2_ref, y_ref, *maybe_next, with_next):
    f = jnp.dot(x_ref[...], w_ref[...], preferred_element_type=F32)
    y = res_ref[...] + (f * lax.rsqrt(jnp.mean(f * f, axis=-1, keepdims=True) + RMS_EPS)) * g_ref[...]
    y_ref[...] = y
    if with_next:
        yn = y * lax.rsqrt(jnp.mean(y * y, axis=-1, keepdims=True) + RMS_EPS)
        maybe_next[0][...] = (yn * g2_ref[...]).astype(maybe_next[0].dtype)


def _mm_res_norm(x, w, res, g, g_next=None):
    rows, k = x.shape
    d = w.shape[1]
    with_next = g_next is not None
    tm = _row_tile(rows, 512 if k * d * 2 <= (12 << 20) else 256)
    blk = pl.BlockSpec((tm, d), lambda i: (i, 0))
    row_spec = pl.BlockSpec((1, d), lambda i: (0, 0))
    out_specs = [blk, blk] if with_next else [blk]
    out_shape = [jax.ShapeDtypeStruct((rows, d), F32)]
    if with_next:
        out_shape.append(jax.ShapeDtypeStruct((rows, d), BF16))
    g2 = g_next if with_next else g
    outs = pl.pallas_call(
        functools.partial(_mm_res_norm_body, with_next=with_next),
        grid=(rows // tm,),
        in_specs=[pl.BlockSpec((tm, k), lambda i: (i, 0)),
                  pl.BlockSpec((k, d), lambda i: (0, 0), pipeline_mode=pl.Buffered(1)),
                  blk, row_spec, row_spec],
        out_specs=out_specs,
        out_shape=out_shape,
        compiler_params=_params("arbitrary"),
        name="matmul_res_norm",
    )(x, w, res, g.reshape(1, d), g2.reshape(1, d))
    return outs if with_next else outs[0]


def _up_conv_gate_body(x_ref, wg_ref, wv_ref, eg_ref, ev_ref, cwg_ref, cwv_ref, cbg_ref, cbv_ref,
                       *rest, seq_rows, tm, taps, rounding, with_next):
    i = pl.program_id(1)
    rest = list(rest)
    if with_next:
        next_ref = rest.pop(0)
        next_bf = rest.pop(-3)
        next_bf[...] = next_ref[...].astype(BF16)
    o_ref, tg_ref, tv_ref = rest[:3]
    keep_g, keep_v = rest[-2:]
    wg_bf, wv_bf = rest[3:5] if rounding else (wg_ref, wv_ref)

    @pl.when(i == 0)
    def _():
        if rounding:
            wg_bf[...] = wg_ref[...].astype(BF16)
            wv_bf[...] = wv_ref[...].astype(BF16)
        keep_g[...] = jnp.zeros_like(keep_g)
        keep_v[...] = jnp.zeros_like(keep_v)

    if seq_rows >= tm:
        is_start = (i * tm) % seq_rows == 0
        tn = o_ref.shape[1]
        row8 = lax.broadcasted_iota(jnp.int32, (SUBLANES, tn), 0)

        def first_halo(e_ref, keep):
            start = jnp.zeros((SUBLANES, tn), F32)
            for idx in range(taps - 1):
                start = jnp.where(row8 == SUBLANES - (taps - 1) + idx, e_ref[0, idx:idx + 1, :], start)
            return jnp.where(is_start, start, keep[...])

        def conv_rows(u, halo, cw, cb):
            acc = cb + cw[taps - 1:taps] * u
            for back in range(1, taps):
                acc = acc + cw[taps - 1 - back:taps - back] * _shift_rows(u, halo, back)
            return acc

        halo_g, halo_v = first_halo(eg_ref, keep_g), first_halo(ev_ref, keep_v)
        cwg, cwv, cbg, cbv = cwg_ref[...], cwv_ref[...], cbg_ref[...], cbv_ref[...]
        xt = x_ref[...]
        ug = jnp.dot(xt, wg_bf[...], preferred_element_type=F32)
        uv = jnp.dot(xt, wv_bf[...], preferred_element_type=F32)
        o_ref[...] = (jax.nn.gelu(conv_rows(ug, halo_g, cwg, cbg)) * conv_rows(uv, halo_v, cwv, cbv)).astype(o_ref.dtype)
        for keep, tail_ref, u in ((keep_g, tg_ref, ug), (keep_v, tv_ref, uv)):
            keep[...] = u[tm - SUBLANES:tm]
            tail_ref[...] = u[tm - SUBLANES:tm]
        return

    x = x_ref[...]

    def conv(w_bf, e_ref, cw_ref, cb_ref, keep, tail_ref):
        u = jnp.dot(x, w_bf[...], preferred_element_type=F32)
        cw = cw_ref[...]
        acc = cb_ref[...] + cw[taps - 1:taps] * u
        n_lane_tiles = u.shape[1] // LANES
        t = lax.broadcasted_iota(jnp.int32, u.shape, 0) & (seq_rows - 1)
        for idx in range(taps - 1):
            for c in range(n_lane_tiles):
                keep[c, pl.ds(idx, tm // seq_rows, stride=seq_rows), :] = e_ref[:, idx, c * LANES:(c + 1) * LANES]
        e = jnp.concatenate([keep[c] for c in range(n_lane_tiles)], axis=1)
        for back in range(1, taps):
            up_by = taps - 1 - back
            state = pltpu.roll(e, tm - up_by, 0) if up_by else e
            prev = jnp.where(t < back, state, pltpu.roll(u, back, 0))
            acc = acc + cw[taps - 1 - back:taps - back] * prev
        for c in range(n_lane_tiles):
            keep[c] = u[:, c * LANES:(c + 1) * LANES]
        for idx in range(taps - 1):
            rows_t = pl.ds(seq_rows - (taps - 1) + idx, tm // seq_rows, stride=seq_rows)
            tail_ref[idx] = jnp.concatenate([keep[c, rows_t, :] for c in range(n_lane_tiles)], axis=1)
        return acc

    gate = conv(wg_bf, eg_ref, cwg_ref, cbg_ref, keep_g, tg_ref)
    val = conv(wv_bf, ev_ref, cwv_ref, cbv_ref, keep_v, tv_ref)
    o_ref[...] = (jax.nn.gelu(gate) * val).astype(o_ref.dtype)


def _up_conv_gate(x, w_up, state_rows, conv_w, conv_b, seq_rows, w_next):
    rows, d = x.shape
    rounding = not isinstance(w_up, tuple)
    f2 = conv_w.shape[1]
    dff = f2 // 2
    taps = conv_w.shape[0]
    tm = _row_tile(rows if seq_rows == SUBLANES else seq_rows, 1024)
    tn = _col_tile(dff, 512)
    nj = dff // tn
    if seq_rows >= tm:
        assert seq_rows % tm == 0
        st = lambda off: pl.BlockSpec((1, taps - 1, tn), lambda j, i: ((i * tm) // seq_rows, 0, j + off))
        tail = pl.BlockSpec((SUBLANES, tn), lambda j, i: ((i * tm) // seq_rows, j))
        tail_shape = ((rows // seq_rows) * SUBLANES, dff)
        keep_shape = (SUBLANES, tn)
    else:
        assert seq_rows == SUBLANES and tm % seq_rows == 0
        st = lambda off: pl.BlockSpec((tm // seq_rows, taps - 1, tn), lambda j, i: (i, 0, j + off))
        tail = pl.BlockSpec((taps - 1, tm // seq_rows, tn), lambda j, i: (0, i, j))
        tail_shape = (taps - 1, rows // seq_rows, dff)
        keep_shape = (tn // LANES, tm, LANES)
    wt = lambda off: pl.BlockSpec((d, tn), lambda j, i: (0, j + off))
    cw = lambda off: pl.BlockSpec((taps, tn), lambda j, i: (0, j + off))
    cb = lambda off: pl.BlockSpec((1, tn), lambda j, i: (0, j + off))
    out_specs = [pl.BlockSpec((tm, tn), lambda j, i: (i, j)), tail, tail]
    out_shape = [jax.ShapeDtypeStruct((rows, dff), BF16), jax.ShapeDtypeStruct(tail_shape, F32),
                 jax.ShapeDtypeStruct(tail_shape, F32)]
    if rounding:
        w_gate, w_val, w_specs = w_up, w_up, [wt(0), wt(nj)]
        out_specs += [wt(0), wt(0)]
        out_shape += [jax.ShapeDtypeStruct((d, dff), BF16)] * 2
    else:
        (w_gate, w_val), w_specs = w_up, [wt(0), wt(0)]
    n_i = rows // tm
    slab = _slab_spec(w_next, nj * n_i, lambda j, i: j * n_i + i) if rounding else None
    extra_specs, extra_args = ([slab], (w_next,)) if slab is not None else ([], ())
    if slab is not None:
        out_specs.append(slab)
        out_shape.append(jax.ShapeDtypeStruct(w_next.shape, BF16))
    outs = pl.pallas_call(
        functools.partial(_up_conv_gate_body, seq_rows=seq_rows, tm=tm, taps=taps, rounding=rounding,
                          with_next=slab is not None),
        grid=(nj, n_i),
        in_specs=[pl.BlockSpec((tm, d), lambda j, i: (i, 0))] + w_specs + [st(0), st(nj), cw(0), cw(nj), cb(0),
                                                                             cb(nj)] + extra_specs,
        out_specs=out_specs,
        out_shape=out_shape,
        scratch_shapes=[pltpu.VMEM(keep_shape, F32), pltpu.VMEM(keep_shape, F32)],
        compiler_params=_params("parallel", "arbitrary"),
        name="up_conv_gate",
    )(x, w_gate, w_val, state_rows, state_rows, conv_w, conv_w, conv_b.reshape(1, f2), conv_b.reshape(1, f2),
      *extra_args)
    act, tail_g, tail_v = outs[:3]
    w_bf = tuple(outs[3:5]) if rounding else w_up
    w_next_bf = outs[-1] if slab is not None else w_next.astype(BF16)
    tail = jnp.concatenate([tail_g, tail_v], axis=-1)
    if seq_rows >= tm:
        return act, tail.reshape(rows // seq_rows, SUBLANES, f2)[:, SUBLANES - (taps - 1):], w_bf, w_next_bf
    return act, jnp.swapaxes(tail, 0, 1), w_bf, w_next_bf


def _layer(x, shift_prev, wkv0, mem_k, mem_v, conv_prev, prm, want_v_rows):
    batch, seq_len, d = x.shape
    rows = batch * seq_len
    w = prm["width"]
    hd = prm["head_dim"]
    x2 = x.reshape(rows, d)

    xn = _rmsnorm(x2, prm["g_pre_mix"], BF16)
    n_shift = prm["mu_shift"].shape[1]
    proj = _matmul_cached(xn, prm, "w_in", 0, n_shift + 3 * w)
    new_shift = proj.reshape(batch, seq_len, -1)[:, -1, :n_shift]

    a_out, h_new = _rwkv(proj, _pad_state_rows(shift_prev[:, None, :]), jnp.swapaxes(wkv0, -1, -2), batch,
                         seq_len, prm)
    new_wkv = jnp.swapaxes(h_new, -1, -2)

    if seq_len % prm["sgu_chunk"] == 0:
        b_out, v_rows = _sgu(proj, n_shift, prm["w_s"], prm["sgu_bias"], prm, None, want_v_rows)
    else:
        assert seq_len == SUBLANES
        b_out, v_rows = _sgu(proj, n_shift, prm["w_s"], prm["sgu_bias"][:, :SUBLANES], prm, SUBLANES,
                             want_v_rows)
    c_out = _xattn(proj, n_shift + 2 * w, w, mem_k, mem_v, batch, seq_len, prm["xattn_heads"])

    mix, (prm["w_gates"], prm["w_branch"]), prm["w_out"] = _gate_branch_mix(
        xn, a_out, b_out, c_out, prm["w_gates"], n_shift + 3 * w, prm["w_branch"], prm["w_out"])
    h, hn = _mm_res_norm(mix, prm["w_out"], x2, prm["g_post_mix"], prm["g_pre_ffn"])

    act, conv_new, prm["w_up"], prm["w_down"] = _up_conv_gate(hn, prm["w_up"], conv_prev, prm["conv_w"],
                                                              prm["conv_b"], seq_len, prm["w_down"])
    y_out = _mm_res_norm(act, prm["w_down"], h, prm["g_post_ffn"])
    if want_v_rows:
        v_rows = v_rows.reshape(batch, seq_len, w)
    return y_out.reshape(batch, seq_len, d), new_shift, new_wkv, v_rows, conv_new


def _prepare(l, g_pre_mix, w_in, mu_shift, w0, w_decay, a0, w_aaa, w_gate, k_k, k_a, r_k, lnx_w, lnx_b,
             sgu_g, sgu_b, w_s, b_s, w_branch, w_out, g_post_mix, g_pre_ffn, w_up, conv_w, conv_b, w_down,
             g_post_ffn):
    heads, hd = r_k.shape[1], r_k.shape[2]
    w = heads * hd
    d = w_in.shape[1]
    n_shift = mu_shift.shape[1]
    rank_d, rank_a, rank_g = w_decay.shape[1], w_aaa.shape[1], w_gate.shape[1]
    assert rank_d + rank_a == LANES and rank_g == LANES and n_shift == 3 * w + 2 * LANES
    groups, sgu_chunk, _ = w_s.shape[1:]
    row = lambda t: t[l].reshape(1, -1)
    lane_head = jnp.arange(LANES) // hd
    gd = w // groups
    return dict(
        width=w, head_dim=hd, sgu_chunk=sgu_chunk,
        g_pre_mix=g_pre_mix[l],
        w_in=w_in[l], w_gates=w_in[l],
        mu_shift=row(mu_shift), w0=row(w0), a0=row(a0),
        wd_pad=jnp.pad(w_decay[l], ((0, rank_a), (0, 0))),
        wa_pad=jnp.pad(w_aaa[l], ((rank_d, 0), (0, 0))),
        w_gate=w_gate[l],
        k_k=row(k_k), k_a=row(k_a), r_k=row(r_k), lnx_w=row(lnx_w), lnx_b=row(lnx_b),
        ones2=(lane_head[:, None] == lane_head[None, :]).astype(BF16),
        sgu_g=row(sgu_g), sgu_b=row(sgu_b),
        w_s=w_s[l],
        sgu_bias=jnp.broadcast_to(b_s[l][:, :, None], (groups, sgu_chunk, gd)),
        w_branch=w_branch[l], w_out=w_out[l],
        g_post_mix=g_post_mix[l], g_pre_ffn=g_pre_ffn[l],
        w_up=w_up[l], conv_w=conv_w[l], conv_b=conv_b[l], w_down=w_down[l],
        g_post_ffn=g_post_ffn[l],
    )


def kernel(x_prompt, x_sample, mem_prompt, state_wkv, state_shift, cache_mem_k, cache_mem_v, state_ffn_conv, g_pre_mix, w_in, mu_shift, w0, w_decay, a0, w_aaa, w_gate, k_k, k_a, r_k, lnx_w, lnx_b, sgu_g, sgu_b, w_s, b_s, g_mem, w_mem_k, w_mem_v, w_branch, w_out, g_post_mix, g_pre_ffn, w_up, conv_w, conv_b, w_down, g_post_ffn):
    depth = w_in.shape[0]
    batch = x_prompt.shape[0]
    mem_len, d = mem_prompt.shape[1], mem_prompt.shape[2]
    xh, xhd = cache_mem_k.shape[3], cache_mem_k.shape[4]
    heads, hd = r_k.shape[1], r_k.shape[2]
    n_shift = mu_shift.shape[1]
    f2 = w_up.shape[2]
    taps = conv_w.shape[1]
    y_p, y_s = x_prompt, x_sample
    outs = [[] for _ in range(9)]
    for l in range(depth):
        prm = _prepare(l, g_pre_mix, w_in, mu_shift, w0, w_decay, a0, w_aaa, w_gate, k_k, k_a, r_k, lnx_w, lnx_b,
                       sgu_g, sgu_b, w_s, b_s, w_branch, w_out, g_post_mix, g_pre_ffn, w_up, conv_w, conv_b,
                       w_down, g_post_ffn)
        mn = _rmsnorm(mem_prompt.reshape(batch * mem_len, d), g_mem[l], BF16)
        prm["xattn_heads"] = xh
        mk_rows = _matmul(mn, w_mem_k[l])[0]
        mv_rows = _matmul(mn, w_mem_v[l])[0]
        mk_p = mk_rows.reshape(batch, mem_len, xh, xhd)
        mv_p = mv_rows.reshape(batch, mem_len, xh, xhd)
        y_p, sh_p, wkv_p, _, cv_p = _layer(
            y_p, jnp.zeros((batch, n_shift), F32), jnp.zeros((batch, heads, hd, hd), F32), mk_rows, mv_rows,
            jnp.zeros((batch, taps - 1, f2), F32), prm, False)
        y_s, sh_s, wkv_s, vr_s, cv_s = _layer(y_s, state_shift[l], state_wkv[l], cache_mem_k[l], cache_mem_v[l],
                                              state_ffn_conv[l], prm, True)
        for lst, val in zip(outs, (wkv_p, sh_p, mk_p, mv_p, cv_p, wkv_s, sh_s, vr_s, cv_s)):
            lst.append(val)
    return (y_p, y_s) + tuple(jnp.stack(lst) for lst in outs)
```

```python
import functools

import jax
import jax.numpy as jnp
from jax import lax
from jax.experimental import pallas as pl
from jax.experimental.pallas import tpu as pltpu

F32 = jnp.float32
BF16 = jnp.bfloat16
HIGHEST = lax.Precision.HIGHEST

LANES = 128
SUBLANES = 8
VMEM_LIMIT_BYTES = 56 * 1024 * 1024

RMS_EPS = 1e-6
LN_EPS = 1e-5
GN_EPS = 64e-5
KK_EPS = 1e-12
RWKV_CHUNK = 64
RWKV_GROUP = 32


def _params(*semantics):
    return pltpu.CompilerParams(dimension_semantics=semantics, vmem_limit_bytes=VMEM_LIMIT_BYTES)


def _row_tile(rows, pref):
    t = min(rows, pref)
    while rows % t:
        t -= SUBLANES
    return t


def _col_tile(n, cap):
    best = n
    for t in range(LANES, min(n, cap) + 1, LANES):
        if n % t == 0:
            best = t
    return best


def _bdot(a, b):
    return jnp.dot(a.astype(BF16), b.astype(BF16), preferred_element_type=F32)


def _bdot_nt(a, b):
    return lax.dot_general(a.astype(BF16), b.astype(BF16), (((1,), (1,)), ((), ())),
                           preferred_element_type=F32)


def _fdot(a, b):
    return jnp.dot(a, b, precision=HIGHEST, preferred_element_type=F32)


def _dot3(a, b):
    a_hi = a.astype(BF16)
    b_hi = b.astype(BF16)
    a_lo = (a - a_hi.astype(F32)).astype(BF16)
    b_lo = (b - b_hi.astype(F32)).astype(BF16)
    dot = lambda x, y: jnp.dot(x, y, preferred_element_type=F32)
    return dot(a_hi, b_hi) + (dot(a_hi, b_lo) + dot(a_lo, b_hi))


def _rmsnorm_body(x_ref, g_ref, o_ref):
    x = x_ref[...]
    y = x * lax.rsqrt(jnp.mean(x * x, axis=-1, keepdims=True) + RMS_EPS)
    o_ref[...] = (y * g_ref[...]).astype(o_ref.dtype)


def _rmsnorm(x, g, out_dtype):
    rows, d = x.shape
    tm = _row_tile(rows, 1024)
    return pl.pallas_call(
        _rmsnorm_body,
        grid=(rows // tm,),
        in_specs=[pl.BlockSpec((tm, d), lambda i: (i, 0)), pl.BlockSpec((1, d), lambda i: (0, 0))],
        out_specs=pl.BlockSpec((tm, d), lambda i: (i, 0)),
        out_shape=jax.ShapeDtypeStruct((rows, d), out_dtype),
        compiler_params=_params("parallel"),
        name="rmsnorm",
    )(x, g.reshape(1, d))


def _matmul_round_body(x_ref, w_ref, o_ref, wb_ref):
    @pl.when(pl.program_id(1) == 0)
    def _():
        wb_ref[...] = w_ref[...].astype(BF16)

    o_ref[...] = jnp.dot(x_ref[...], wb_ref[...], preferred_element_type=F32).astype(o_ref.dtype)


def _matmul_body(x_ref, w_ref, o_ref):
    o_ref[...] = jnp.dot(x_ref[...], w_ref[...], preferred_element_type=F32).astype(o_ref.dtype)


def _matmul(x, w, col_start=0, n_cols=None, out_dtype=F32):
    rows, k = x.shape
    n = w.shape[1] - col_start if n_cols is None else n_cols
    assert col_start % LANES == 0 and n % LANES == 0
    rounding = w.dtype != BF16
    tn = _col_tile(n, 1792)
    budget = VMEM_LIMIT_BYTES - (4 << 20)
    tm = _row_tile(rows, 1024)
    w_bytes = 2 * k * tn * (4 + 2) if rounding else 2 * k * tn * 2
    while w_bytes + 2 * tm * k * 2 + 2 * tm * tn * 4 > budget and tm % (2 * SUBLANES) == 0:
        tm //= 2
    w_spec = pl.BlockSpec((pl.Element(k), pl.Element(tn)),
                          lambda j, i: (0, pl.multiple_of(col_start + j * tn, LANES)))
    out_spec = pl.BlockSpec((tm, tn), lambda j, i: (i, j))
    out_shape = jax.ShapeDtypeStruct((rows, n), out_dtype)
    if rounding:
        out_spec = [out_spec, pl.BlockSpec((k, tn), lambda j, i: (0, j))]
        out_shape = [out_shape, jax.ShapeDtypeStruct((k, n), BF16)]
    return pl.pallas_call(
        _matmul_round_body if rounding else _matmul_body,
        grid=(n // tn, rows // tm),
        in_specs=[pl.BlockSpec((tm, k), lambda j, i: (i, 0)), w_spec],
        out_specs=out_spec,
        out_shape=out_shape,
        compiler_params=_params("parallel", "arbitrary"),
        name="matmul",
    )(x, w)


def _matmul_cached(x, prm, name, col_start=0, n_cols=None):
    key = (name, col_start)
    if key in prm:
        return _matmul(x, prm[key])
    out, prm[key] = _matmul(x, prm[name], col_start, n_cols)
    return out


def _shift_rows(x, halo, k):
    rolled = pltpu.roll(x, k, 0)
    row = lax.broadcasted_iota(jnp.int32, (SUBLANES, x.shape[1]), 0)
    top = rolled[0:SUBLANES]
    for j in range(k):
        top = jnp.where(row == j, halo[SUBLANES - k + j:SUBLANES - k + j + 1, :], top)
    return jnp.concatenate([top, rolled[SUBLANES:]], axis=0)


def _head_sum_operand(x):
    hi = x.astype(BF16)
    lo = (x - hi.astype(F32)).astype(BF16)
    tiles = x.shape[1] // LANES
    return jnp.concatenate([part[:, j * LANES:(j + 1) * LANES] for part in (hi, lo) for j in range(tiles)], axis=0)


def _head_sum_result(s, n, tiles):
    return jnp.concatenate([s[j * n:(j + 1) * n] + s[(tiles + j) * n:(tiles + j + 1) * n] for j in range(tiles)],
                           axis=1)


def _head_sum(x, ones_ref):
    s = jnp.dot(_head_sum_operand(x), ones_ref[...], preferred_element_type=F32)
    return _head_sum_result(s, x.shape[0], x.shape[1] // LANES)


def _pad_state_rows(state):
    b, n, f = state.shape
    return jnp.pad(state, ((0, 0), (0, SUBLANES - n), (0, 0))).reshape(b * SUBLANES, f)


def _rwkv_inputs(x, prev, mu_ref, w0_ref, wd_ref, a0_ref, wa_ref, wg_ref, kk_ref, ka_ref, rk_ref, ones_ref, width):
    xs = x + mu_ref[...] * (prev - x)
    w = width
    n = x.shape[0]
    r = xs[:, 0:w]
    k = xs[:, w:2 * w]
    v = xs[:, 2 * w:3 * w]
    x_wa = xs[:, 3 * w:3 * w + LANES]
    x_g = xs[:, 3 * w + LANES:3 * w + 2 * LANES]

    z = -(w0_ref[...] + _dot3(jnp.tanh(x_wa), wd_ref[...]))
    softplus = jnp.maximum(z, 0.0) + jnp.log(1.0 + jnp.exp(-jnp.abs(z)))
    lw = -jnp.exp(-softplus - 0.5)
    a_lr = jax.nn.sigmoid(a0_ref[...] + _bdot(x_wa, wa_ref[...]))
    g = _bdot(jax.nn.sigmoid(x_g), wg_ref[...])

    kk = k * kk_ref[...]
    k_mod = k * (1.0 + (a_lr - 1.0) * ka_ref[...])
    sums = _head_sum(jnp.concatenate([kk * kk, r * k_mod * rk_ref[...]], axis=0), ones_ref)
    kk = kk / jnp.maximum(jnp.sqrt(sums[:n]), KK_EPS)
    return r, lw, k_mod, v, -kk, kk * a_lr, g, sums[n:] * v


def _rwkv_body(rw_ref, sp_ref, mu_ref, w0_ref, wd_ref, a0_ref, wa_ref, wg_ref, kk_ref, ka_ref, rk_ref, ones_ref,
               lnw_ref, lnb_ref, h0_ref, o_ref, hout_ref, h_scr, carry, *, chunk, pairs, seqs, width, hd, n_chunks, group):
    c = pl.program_id(1)
    n_pairs = width // LANES

    @pl.when(c == 0)
    def _():
        zero = jnp.zeros((hd, hd), F32)
        for i in range(seqs):
            for p in range(n_pairs):
                h_scr[i, p] = jnp.concatenate(
                    [jnp.concatenate([h0_ref[i, 2 * p], zero], axis=1),
                     jnp.concatenate([zero, h0_ref[i, 2 * p + 1]], axis=1)], axis=0)
        carry[...] = jnp.zeros_like(carry)

    if n_chunks > 1:
        x = rw_ref[...].reshape(seqs * chunk, rw_ref.shape[2])
        prevs = []
        for i in range(seqs):
            xi = x[i * chunk:(i + 1) * chunk]
            before = jnp.where(c == 0, pltpu.roll(sp_ref[i * SUBLANES:(i + 1) * SUBLANES, :], SUBLANES - 1, 0),
                               carry[i])
            prevs.append(_shift_rows(xi, before, 1))
            carry[i] = xi[chunk - SUBLANES:]
        prev = jnp.concatenate(prevs, axis=0)
    else:
        x = rw_ref[...]
        t = lax.broadcasted_iota(jnp.int32, x.shape, 0) & (chunk - 1)
        prev = jnp.where(t == 0, sp_ref[...], pltpu.roll(x, 1, 0))
    r_all, lw_all, k_all, v_all, a_all, b_all, g_all, bonus_all = _rwkv_inputs(
        x, prev, mu_ref, w0_ref, wd_ref, a0_ref, wa_ref, wg_ref, kk_ref, ka_ref, rk_ref, ones_ref, width)

    cs = chunk
    lanes = LANES * pairs
    heads = 2 * pairs
    n = heads * cs
    log_c = cs.bit_length() - 1
    log_hd = hd.bit_length() - 1
    n_stacks = width // lanes

    tri = (lax.broadcasted_iota(jnp.int32, (cs, cs), 0) >= lax.broadcasted_iota(jnp.int32, (cs, cs), 1)).astype(F32)
    row = lax.broadcasted_iota(jnp.int32, (n, lanes), 0)
    lane = lax.broadcasted_iota(jnp.int32, (n, lanes), 1)
    head_mask = (row >> log_c) == (lane >> log_hd)
    ri = lax.broadcasted_iota(jnp.int32, (n, n), 0)
    ci = lax.broadcasted_iota(jnp.int32, (n, n), 1)
    same_head = (ri >> log_c) == (ci >> log_c)
    rt = ri & (cs - 1)
    ct = ci & (cs - 1)
    strict = same_head & (ct < rt)
    incl = same_head & (ct <= rt)
    eye = (ri == ci).astype(F32)
    diag = (lax.broadcasted_iota(jnp.int32, (LANES, LANES), 0)
            == lax.broadcasted_iota(jnp.int32, (LANES, LANES), 1))

    def stack(x):
        return jnp.where(head_mask, jnp.concatenate([x] * heads, axis=0), 0.0)

    def fold(x):
        out = x[0:cs]
        for hidx in range(1, heads):
            out = out + x[hidx * cs:(hidx + 1) * cs]
        return out

    scaled = []
    for i in range(seqs):
        rs = slice(i * cs, (i + 1) * cs)
        lw = lw_all[rs]
        cum = _fdot(tri, lw)
        total = cum[cs - 1:cs]
        e_cum = jnp.exp(cum)
        e_neg = jnp.exp(-cum)
        e_rest = jnp.exp(total - cum)
        b = b_all[rs]
        k = k_all[rs]
        scaled.append(dict(a=a_all[rs] * jnp.exp(cum - lw), r=r_all[rs] * e_cum, b=b * e_neg, k=k * e_neg,
                           v=v_all[rs], bh=b * e_rest, kh=k * e_rest, total=total))

    def pair_lanes(x, p, off=0):
        return x[:, off + p * LANES:off + (p + 1) * LANES]

    def recur(inst):
        jj = range(len(inst))

        def stacked(name):
            return [stack(scaled[i][name][:, s * lanes:(s + 1) * lanes]) for i, s in inst]

        a_st, r_st, b_st, k_st, v_st, bh_st, kh_st = (stacked(nm) for nm in ("a", "r", "b", "k", "v", "bh", "kh"))

        gram = [_bdot_nt(jnp.concatenate([a_st[j], r_st[j]], axis=0), jnp.concatenate([b_st[j], k_st[j]], axis=0))
                for j in jj]
        l_ab = [jnp.where(strict, gram[j][:n, :n], 0.0) for j in jj]
        m_rb = [jnp.where(incl, gram[j][n:, :n], 0.0) for j in jj]
        l_ak_m_rk = [jnp.concatenate([jnp.where(strict, gram[j][:n, n:], 0.0),
                                      jnp.where(incl, gram[j][n:, n:], 0.0)], axis=0) for j in jj]

        t_inv = [eye + l_ab[j] for j in jj]
        x_pow = [_bdot(l_ab[j], l_ab[j]) for j in jj]
        for _ in range(log_c - 2):
            both = [_bdot(jnp.concatenate([x_pow[j], t_inv[j]], axis=0), x_pow[j]) for j in jj]
            t_inv = [t_inv[j] + both[j][n:] for j in jj]
            x_pow = [both[j][:n] for j in jj]
        t_inv = [t_inv[j] + _bdot(t_inv[j], x_pow[j]) for j in jj]

        lv = [_bdot(l_ak_m_rk[j], v_st[j]) for j in jj]
        aw = [_bdot(t_inv[j], jnp.concatenate([a_st[j], lv[j][:n]], axis=1)) for j in jj]
        rb = [_bdot(m_rb[j], aw[j]) for j in jj]
        r_bar = [fold(r_st[j] + rb[j][:, :lanes]) for j in jj]
        y0 = [fold(rb[j][:, lanes:] + lv[j][n:]) for j in jj]

        jp = [(j, p) for j in jj for p in range(pairs)]
        h = [h_scr[inst[j][0], inst[j][1] * pairs + p] for j, p in jp]
        y_blocks = [_bdot(pair_lanes(r_bar[j], p), h[q]) + pair_lanes(y0[j], p) for q, (j, p) in enumerate(jp)]
        bh_t = [pair_lanes(bh_st[j], p).T for j, p in jp]
        kh_t = [pair_lanes(kh_st[j], p).T for j, p in jp]
        pp = [_bdot(bh_t[q], jnp.concatenate([pair_lanes(aw[j], p), pair_lanes(aw[j], p, lanes)], axis=1))
              for q, (j, p) in enumerate(jp)]
        kv = [_bdot(kh_t[q], pair_lanes(v_st[j], p)) for q, (j, p) in enumerate(jp)]
        for q, (j, p) in enumerate(jp):
            i, s = inst[j]
            decay = pair_lanes(scaled[i]["total"], p, s * lanes)
            phi = jnp.where(diag, jnp.exp(decay), 0.0) + pp[q][:, :LANES]
            h_scr[i, s * pairs + p] = _bdot(phi, h[q]) + (pp[q][:, LANES:] + kv[q])
        return y_blocks

    all_inst = [(i, s) for i in range(seqs) for s in range(n_stacks)]
    yv = []
    for g0 in range(0, len(all_inst), group):
        yv += recur(all_inst[g0:g0 + group])

    per_seq = n_stacks * pairs
    y = jnp.concatenate([jnp.concatenate(yv[i * per_seq:(i + 1) * per_seq], axis=1) for i in range(seqs)], axis=0)
    mean = _head_sum(y, ones_ref) * (1.0 / hd)
    dev = y - mean
    var = _head_sum(dev * dev, ones_ref) * (1.0 / hd)
    yn = dev * lax.rsqrt(var + GN_EPS) * lnw_ref[...] + lnb_ref[...]
    o_ref[...] = ((yn + bonus_all) * g_all).astype(o_ref.dtype).reshape(o_ref.shape)

    @pl.when(c == pl.num_programs(1) - 1)
    def _():
        for i in range(seqs):
            for p in range(n_pairs):
                both = h_scr[i, p]
                hout_ref[i, 2 * p] = both[:hd, :hd]
                hout_ref[i, 2 * p + 1] = both[hd:, hd:]


def _rwkv(proj, shift_rows, h0, batch, seq_len, prm):
    rows = proj.shape[0]
    w, hd = prm["width"], prm["head_dim"]
    n_shift = prm["mu_shift"].shape[1]
    chunk = min(seq_len, RWKV_CHUNK)
    assert seq_len % chunk == 0 and chunk % SUBLANES == 0 and LANES % chunk == 0
    pairs = LANES // (2 * chunk)
    n_chunks = seq_len // chunk
    n_pairs = w // LANES
    target = RWKV_GROUP if n_chunks > 1 else 8
    seqs = 1
    while seqs * (n_pairs // pairs) < target and batch % (2 * seqs) == 0:
        seqs *= 2
    if n_chunks > 1:
        proj = proj.reshape(batch, seq_len, proj.shape[1])
        vec = lambda cols: pl.BlockSpec((seqs, chunk, cols), lambda bi, ci: (bi, ci, 0))
        out_shape = (batch, seq_len, w)
    else:
        vec = lambda cols: pl.BlockSpec((seqs * chunk, cols), lambda bi, ci: (bi, 0))
        out_shape = (rows, w)
    st = pl.BlockSpec((seqs, w // hd, hd, hd), lambda bi, ci: (bi, 0, 0, 0))
    row_spec = lambda c: pl.BlockSpec((1, c), lambda bi, ci: (0, 0))
    full = lambda a: pl.BlockSpec(a.shape, lambda bi, ci: (0, 0))
    a_out, h_new = pl.pallas_call(
        functools.partial(_rwkv_body, chunk=chunk, pairs=pairs, seqs=seqs, width=w, hd=hd, n_chunks=n_chunks,
                          group=RWKV_GROUP),
        grid=(batch // seqs, n_chunks),
        in_specs=[vec(n_shift), pl.BlockSpec((seqs * SUBLANES, n_shift), lambda bi, ci: (bi, 0)),
                  row_spec(n_shift), row_spec(w), full(prm["wd_pad"]), row_spec(w), full(prm["wa_pad"]),
                  full(prm["w_gate"]), row_spec(w), row_spec(w), row_spec(w), full(prm["ones2"]),
                  row_spec(w), row_spec(w), st],
        out_specs=[vec(w), st],
        out_shape=[jax.ShapeDtypeStruct(out_shape, BF16),
                   jax.ShapeDtypeStruct((batch, w // hd, hd, hd), F32)],
        scratch_shapes=[pltpu.VMEM((seqs, n_pairs, LANES, LANES), F32),
                        pltpu.VMEM((seqs, SUBLANES, n_shift), F32)],
        compiler_params=_params("parallel", "arbitrary"),
        name="rwkv",
    )(proj, shift_rows, prm["mu_shift"], prm["w0"], prm["wd_pad"], prm["a0"], prm["wa_pad"], prm["w_gate"],
      prm["k_k"], prm["k_a"], prm["r_k"], prm["ones2"], prm["lnx_w"], prm["lnx_b"], h0)
    return a_out.reshape(rows, w), h_new


def _sgu_body(gu_ref, gv_ref, sg_ref, sb_ref, wm_ref, bias_ref, o_ref, *maybe_v, groups, rows, seq_rows):
    u = jax.nn.gelu(gu_ref[...])
    vf = jax.nn.gelu(gv_ref[...])
    mu = jnp.mean(vf, axis=-1, keepdims=True)
    d = vf - mu
    var = jnp.mean(d * d, axis=-1, keepdims=True)
    v = (d * lax.rsqrt(var + LN_EPS)) * sg_ref[...] + sb_ref[...]
    if maybe_v:
        maybe_v[0][...] = v
    gd = v.shape[1] // groups
    ri = lax.broadcasted_iota(jnp.int32, (rows, rows), 0)
    ci = lax.broadcasted_iota(jnp.int32, (rows, rows), 1)
    causal = ri >= ci
    if seq_rows is not None:
        shift = seq_rows.bit_length() - 1
        causal = causal & ((ri >> shift) == (ci >> shift))
        pos = ((lax.broadcasted_iota(jnp.int32, (rows, seq_rows), 0) & (seq_rows - 1))
               == lax.broadcasted_iota(jnp.int32, (rows, seq_rows), 1)).astype(F32)
    for g in range(groups):
        sl = slice(g * gd, (g + 1) * gd)
        if seq_rows is None:
            w_full, bias = wm_ref[g], bias_ref[g]
        else:
            w_full = lax.dot_general(_fdot(pos, wm_ref[g][:seq_rows, :seq_rows]), pos, (((1,), (1,)), ((), ())),
                                     precision=HIGHEST, preferred_element_type=F32)
            bias = _fdot(pos, bias_ref[g])
        w_causal = jnp.where(causal, w_full, 0.0).astype(BF16)
        for blk in range(v.shape[0] // rows):
            rs = slice(blk * rows, (blk + 1) * rows)
            s = _bdot(w_causal, v[rs, sl]) + bias
            o_ref[rs, sl] = (u[rs, sl] * s).astype(o_ref.dtype)


def _col_window(rows_per_step, width, col0, row_of):
    return pl.BlockSpec((pl.Element(rows_per_step), pl.Element(width)),
                        lambda *idx: (pl.multiple_of(row_of(*idx) * rows_per_step, SUBLANES), col0))


def _sgu(proj, gu_col, w_mix, bias, prm, seq_rows, want_v_rows):
    rows = proj.shape[0]
    w = prm["width"]
    groups, chunk, _ = w_mix.shape
    mix_rows = chunk if seq_rows is None else _row_tile(rows, 2 * LANES)
    cs = mix_rows
    while cs < 1024 and rows % (2 * cs) == 0:
        cs *= 2
    row_spec = pl.BlockSpec((1, w), lambda i: (0, 0))
    out = pl.BlockSpec((cs, w), lambda i: (i, 0))
    outs = pl.pallas_call(
        functools.partial(_sgu_body, groups=groups, rows=mix_rows, seq_rows=seq_rows),
        grid=(rows // cs,),
        in_specs=[_col_window(cs, w, gu_col, lambda i: i), _col_window(cs, w, gu_col + w, lambda i: i),
                  row_spec, row_spec,
                  pl.BlockSpec(w_mix.shape, lambda i: (0, 0, 0)), pl.BlockSpec(bias.shape, lambda i: (0, 0, 0))],
        out_specs=[out, out] if want_v_rows else [out],
        out_shape=[jax.ShapeDtypeStruct((rows, w), BF16)] + ([jax.ShapeDtypeStruct((rows, w), F32)]
                                                           if want_v_rows else []),
        compiler_params=_params("parallel"),
        name="sgu",
    )(proj, proj, prm["sgu_g"], prm["sgu_b"], w_mix, bias)
    return (outs[0], outs[1]) if want_v_rows else (outs[0], None)


def _xattn_body(q_ref, k_ref, v_ref, o_ref, *, heads, seqs, tq, m):
    hd = q_ref.shape[1] // heads
    scale = hd ** -0.5
    cases = [(h, i) for h in range(heads) for i in range(seqs)]
    cols = lambda h: slice(h * hd, (h + 1) * hd)
    mem = lambda i: slice(i * m, (i + 1) * m)
    s = [_bdot_nt(q_ref[i * tq:(i + 1) * tq, cols(h)], k_ref[mem(i), cols(h)]) * scale for h, i in cases]
    e = [jnp.exp(x - jnp.max(x, axis=-1, keepdims=True)) for x in s]
    p = [x / jnp.sum(x, axis=-1, keepdims=True) for x in e]
    o = [_bdot(p[c], v_ref[mem(i), cols(h)]) for c, (h, i) in enumerate(cases)]
    for h in range(heads):
        o_ref[:, cols(h)] = jnp.concatenate(o[h * seqs:(h + 1) * seqs], axis=0).astype(o_ref.dtype)


def _xattn_cache_body(q_ref, k_ref, v_ref, o_ref, *, heads, seqs, tq):
    hd = q_ref.shape[1] // heads
    m = k_ref.shape[1]
    n = m * heads
    scale = hd ** -0.5
    row_head = lax.broadcasted_iota(jnp.int32, (heads * tq, n), 0) >> (tq.bit_length() - 1)
    lane_head = lax.broadcasted_iota(jnp.int32, (heads * tq, n), 1) & (heads - 1)
    own = row_head == lane_head
    q = [jnp.concatenate([q_ref[i * tq:(i + 1) * tq, h * hd:(h + 1) * hd] for h in range(heads)], axis=0)
         for i in range(seqs)]
    s = [jnp.where(own, _bdot_nt(q[i], k_ref[i].reshape(n, hd)) * scale, -1e30) for i in range(seqs)]
    e = [jnp.exp(x - jnp.max(x, axis=-1, keepdims=True)) for x in s]
    p = [x / jnp.sum(x, axis=-1, keepdims=True) for x in e]
    o = [_bdot(p[i], v_ref[i].reshape(n, hd)) for i in range(seqs)]
    for h in range(heads):
        o_ref[:, h * hd:(h + 1) * hd] = jnp.concatenate(
            [o[i][h * tq:(h + 1) * tq] for i in range(seqs)], axis=0).astype(o_ref.dtype)


def _xattn(proj, q_col, w, mem_k, mem_v, batch, seq_len, heads):
    rows = proj.shape[0]
    tq = _row_tile(seq_len, 1024)
    nq = seq_len // tq
    seqs = max(1, 32 // tq)
    assert batch % seqs == 0 and (seqs == 1 or nq == 1)
    if mem_k.ndim == 2:
        m = mem_k.shape[0] // batch
        kv = pl.BlockSpec((seqs * m, w), lambda bi, qi: (bi, 0))
        body = functools.partial(_xattn_body, heads=heads, seqs=seqs, tq=tq, m=m)
    else:
        assert tq & (tq - 1) == 0 and heads & (heads - 1) == 0
        kv = pl.BlockSpec((seqs,) + mem_k.shape[1:], lambda bi, qi: (bi, 0, 0, 0))
        body = functools.partial(_xattn_cache_body, heads=heads, seqs=seqs, tq=tq)
    return pl.pallas_call(
        body,
        grid=(batch // seqs, nq),
        in_specs=[_col_window(seqs * tq, w, q_col, lambda bi, qi: bi * nq + qi), kv, kv],
        out_specs=pl.BlockSpec((seqs * tq, w), lambda bi, qi: (bi * nq + qi, 0)),
        out_shape=jax.ShapeDtypeStruct((rows, w), BF16),
        compiler_params=_params("parallel", "parallel"),
        name="mem_xattn",
    )(proj, mem_k, mem_v)


def _slab_spec(weight, grid_steps, step_of):
    k, d = weight.shape
    if weight.dtype == BF16 or k % grid_steps or (k // grid_steps) % (2 * SUBLANES):
        return None
    return pl.BlockSpec((k // grid_steps, d), lambda j, i: (step_of(j, i), 0))


def _gate_branch_body(*refs, rounding, with_next):
    if with_next:
        *refs, next_bf = refs
        next_ref = refs.pop(8)
        next_bf[...] = next_ref[...].astype(BF16)
    if rounding:
        x_ref, a_ref, b_ref, c_ref, wg0_ref, wg1_ref, wg2_ref, wb_ref, o_ref, wg_bf, wb_bf = refs

        @pl.when(pl.program_id(1) == 0)
        def _():
            for n, src in enumerate((wg0_ref, wg1_ref, wg2_ref)):
                wg_bf[n] = src[...].astype(BF16)
            wb_bf[...] = wb_ref[...].astype(BF16)
    else:
        x_ref, a_ref, b_ref, c_ref, wg_bf, wb_bf, o_ref = refs

    x = x_ref[...]
    acc = None
    for n, br_ref in enumerate((a_ref, b_ref, c_ref)):
        gate = jax.nn.sigmoid(jnp.dot(x, wg_bf[n], preferred_element_type=F32))
        term = gate * jnp.dot(br_ref[...], wb_bf[n], preferred_element_type=F32)
        acc = term if acc is None else acc + term
    o_ref[...] = acc.astype(o_ref.dtype)


def _gate_branch_mix(xn, a_out, b_out, c_out, w_gates, gate_col0, w_branch, w_next):
    rows, k = xn.shape
    w = a_out.shape[1]
    nb, _, d = w_branch.shape
    assert nb == 3
    rounding = w_branch.dtype != BF16
    tm = _row_tile(rows, 1024)
    tn = _col_tile(d, 256)
    nj = d // tn
    act = lambda width: pl.BlockSpec((tm, width), lambda j, i: (i, 0))
    wb_spec = pl.BlockSpec((nb, w, tn), lambda j, i: (0, 0, j))
    wg_spec = pl.BlockSpec((nb, k, tn), lambda j, i: (0, 0, j))
    out_specs = [pl.BlockSpec((tm, tn), lambda j, i: (i, j))]
    out_shape = [jax.ShapeDtypeStruct((rows, d), BF16)]
    if rounding:
        gate_cols = lambda n: pl.BlockSpec(
            (pl.Element(k), pl.Element(tn)),
            lambda j, i: (0, pl.multiple_of(gate_col0 + n * d + j * tn, LANES)))
        w_specs = [gate_cols(0), gate_cols(1), gate_cols(2), wb_spec]
        w_args = (w_gates, w_gates, w_gates, w_branch)
        out_specs += [wg_spec, wb_spec]
        out_shape += [jax.ShapeDtypeStruct((nb, k, d), BF16), jax.ShapeDtypeStruct(w_branch.shape, BF16)]
    else:
        w_specs = [wg_spec, wb_spec]
        w_args = (w_gates, w_branch)
    n_i = rows // tm
    slab = _slab_spec(w_next, nj * n_i, lambda j, i: j * n_i + i) if rounding else None
    if slab is not None:
        w_specs, w_args = w_specs + [slab], w_args + (w_next,)
        out_specs.append(slab)
        out_shape.append(jax.ShapeDtypeStruct(w_next.shape, BF16))
    outs = pl.pallas_call(
        functools.partial(_gate_branch_body, rounding=rounding, with_next=slab is not None),
        grid=(nj, n_i),
        in_specs=[act(k), act(w), act(w), act(w)] + w_specs,
        out_specs=out_specs,
        out_shape=out_shape,
        compiler_params=_params("parallel", "arbitrary"),
        name="gate_branch_mix",
    )(xn, a_out, b_out, c_out, *w_args)
    w_next_bf = outs[-1] if slab is not None else w_next.astype(BF16)
    return outs[0], ((outs[1], outs[2]) if rounding else (w_gates, w_branch)), w_next_bf


def _mm_res_norm_body(x_ref, w_ref, res_ref, g_ref, g2_ref, y_ref, *maybe_next, with_next):
    f = jnp.dot(x_ref[...], w_ref[...], preferred_element_type=F32)
    y = res_ref[...] + (f * lax.rsqrt(jnp.mean(f * f, axis=-1, keepdims=True) + RMS_EPS)) * g_ref[...]
    y_ref[...] = y
    if with_next:
        yn = y * lax.rsqrt(jnp.mean(y * y, axis=-1, keepdims=True) + RMS_EPS)
        maybe_next[0][...] = (yn * g2_ref[...]).astype(maybe_next[0].dtype)


def _mm_res_norm(x, w, res, g, g_next=None):
    rows, k = x.shape
    d = w.shape[1]
    with_next = g_next is not None
    tm = _row_tile(rows, 512 if k * d * 2 <= (12 << 20) else 256)
    blk = pl.BlockSpec((tm, d), lambda i: (i, 0))
    row_spec = pl.BlockSpec((1, d), lambda i: (0, 0))
    out_specs = [blk, blk] if with_next else [blk]
    out_shape = [jax.ShapeDtypeStruct((rows, d), F32)]
    if with_next:
        out_shape.append(jax.ShapeDtypeStruct((rows, d), BF16))
    g2 = g_next if with_next else g
    outs = pl.pallas_call(
        functools.partial(_mm_res_norm_body, with_next=with_next),
        grid=(rows // tm,),
        in_specs=[pl.BlockSpec((tm, k), lambda i: (i, 0)),
                  pl.BlockSpec((k, d), lambda i: (0, 0), pipeline_mode=pl.Buffered(1)),
                  blk, row_spec, row_spec],
        out_specs=out_specs,
        out_shape=out_shape,
        compiler_params=_params("arbitrary"),
        name="matmul_res_norm",
    )(x, w, res, g.reshape(1, d), g2.reshape(1, d))
    return outs if with_next else outs[0]


def _up_conv_gate_body(x_ref, wg_ref, wv_ref, eg_ref, ev_ref, cwg_ref, cwv_ref, cbg_ref, cbv_ref,
                       *rest, seq_rows, tm, taps, rounding, with_next):
    i = pl.program_id(1)
    rest = list(rest)
    if with_next:
        next_ref = rest.pop(0)
        next_bf = rest.pop(-3)
        next_bf[...] = next_ref[...].astype(BF16)
    o_ref, tg_ref, tv_ref = rest[:3]
    keep_g, keep_v = rest[-2:]
    wg_bf, wv_bf = rest[3:5] if rounding else (wg_ref, wv_ref)

    @pl.when(i == 0)
    def _():
        if rounding:
            wg_bf[...] = wg_ref[...].astype(BF16)
            wv_bf[...] = wv_ref[...].astype(BF16)
        keep_g[...] = jnp.zeros_like(keep_g)
        keep_v[...] = jnp.zeros_like(keep_v)

    if seq_rows >= tm:
        is_start = (i * tm) % seq_rows == 0
        tn = o_ref.shape[1]
        row8 = lax.broadcasted_iota(jnp.int32, (SUBLANES, tn), 0)

        def first_halo(e_ref, keep):
            start = jnp.zeros((SUBLANES, tn), F32)
            for idx in range(taps - 1):
                start = jnp.where(row8 == SUBLANES - (taps - 1) + idx, e_ref[0, idx:idx + 1, :], start)
            return jnp.where(is_start, start, keep[...])

        def conv_rows(u, halo, cw, cb):
            acc = cb + cw[taps - 1:taps] * u
            for back in range(1, taps):
                acc = acc + cw[taps - 1 - back:taps - back] * _shift_rows(u, halo, back)
            return acc

        halo_g, halo_v = first_halo(eg_ref, keep_g), first_halo(ev_ref, keep_v)
        cwg, cwv, cbg, cbv = cwg_ref[...], cwv_ref[...], cbg_ref[...], cbv_ref[...]
        xt = x_ref[...]
        ug = jnp.dot(xt, wg_bf[...], preferred_element_type=F32)
        uv = jnp.dot(xt, wv_bf[...], preferred_element_type=F32)
        o_ref[...] = (jax.nn.gelu(conv_rows(ug, halo_g, cwg, cbg)) * conv_rows(uv, halo_v, cwv, cbv)).astype(o_ref.dtype)
        for keep, tail_ref, u in ((keep_g, tg_ref, ug), (keep_v, tv_ref, uv)):
            keep[...] = u[tm - SUBLANES:tm]
            tail_ref[...] = u[tm - SUBLANES:tm]
        return

    x = x_ref[...]

    def conv(w_bf, e_ref, cw_ref, cb_ref, keep, tail_ref):
        u = jnp.dot(x, w_bf[...], preferred_element_type=F32)
        cw = cw_ref[...]
        acc = cb_ref[...] + cw[taps - 1:taps] * u
        n_lane_tiles = u.shape[1] // LANES
        t = lax.broadcasted_iota(jnp.int32, u.shape, 0) & (seq_rows - 1)
        for idx in range(taps - 1):
            for c in range(n_lane_tiles):
                keep[c, pl.ds(idx, tm // seq_rows, stride=seq_rows), :] = e_ref[:, idx, c * LANES:(c + 1) * LANES]
        e = jnp.concatenate([keep[c] for c in range(n_lane_tiles)], axis=1)
        for back in range(1, taps):
            up_by = taps - 1 - back
            state = pltpu.roll(e, tm - up_by, 0) if up_by else e
            prev = jnp.where(t < back, state, pltpu.roll(u, back, 0))
            acc = acc + cw[taps - 1 - back:taps - back] * prev
        for c in range(n_lane_tiles):
            keep[c] = u[:, c * LANES:(c + 1) * LANES]
        for idx in range(taps - 1):
            rows_t = pl.ds(seq_rows - (taps - 1) + idx, tm // seq_rows, stride=seq_rows)
            tail_ref[idx] = jnp.concatenate([keep[c, rows_t, :] for c in range(n_lane_tiles)], axis=1)
        return acc

    gate = conv(wg_bf, eg_ref, cwg_ref, cbg_ref, keep_g, tg_ref)
    val = conv(wv_bf, ev_ref, cwv_ref, cbv_ref, keep_v, tv_ref)
    o_ref[...] = (jax.nn.gelu(gate) * val).astype(o_ref.dtype)


def _up_conv_gate(x, w_up, state_rows, conv_w, conv_b, seq_rows, w_next):
    rows, d = x.shape
    rounding = not isinstance(w_up, tuple)
    f2 = conv_w.shape[1]
    dff = f2 // 2
    taps = conv_w.shape[0]
    tm = _row_tile(rows if seq_rows == SUBLANES else seq_rows, 1024)
    tn = _col_tile(dff, 512)
    nj = dff // tn
    if seq_rows >= tm:
        assert seq_rows % tm == 0
        st = lambda off: pl.BlockSpec((1, taps - 1, tn), lambda j, i: ((i * tm) // seq_rows, 0, j + off))
        tail = pl.BlockSpec((SUBLANES, tn), lambda j, i: ((i * tm) // seq_rows, j))
        tail_shape = ((rows // seq_rows) * SUBLANES, dff)
        keep_shape = (SUBLANES, tn)
    else:
        assert seq_rows == SUBLANES and tm % seq_rows == 0
        st = lambda off: pl.BlockSpec((tm // seq_rows, taps - 1, tn), lambda j, i: (i, 0, j + off))
        tail = pl.BlockSpec((taps - 1, tm // seq_rows, tn), lambda j, i: (0, i, j))
        tail_shape = (taps - 1, rows // seq_rows, dff)
        keep_shape = (tn // LANES, tm, LANES)
    wt = lambda off: pl.BlockSpec((d, tn), lambda j, i: (0, j + off))
    cw = lambda off: pl.BlockSpec((taps, tn), lambda j, i: (0, j + off))
    cb = lambda off: pl.BlockSpec((1, tn), lambda j, i: (0, j + off))
    out_specs = [pl.BlockSpec((tm, tn), lambda j, i: (i, j)), tail, tail]
    out_shape = [jax.ShapeDtypeStruct((rows, dff), BF16), jax.ShapeDtypeStruct(tail_shape, F32),
                 jax.ShapeDtypeStruct(tail_shape, F32)]
    if rounding:
        w_gate, w_val, w_specs = w_up, w_up, [wt(0), wt(nj)]
        out_specs += [wt(0), wt(0)]
        out_shape += [jax.ShapeDtypeStruct((d, dff), BF16)] * 2
    else:
        (w_gate, w_val), w_specs = w_up, [wt(0), wt(0)]
    n_i = rows // tm
    slab = _slab_spec(w_next, nj * n_i, lambda j, i: j * n_i + i) if rounding else None
    extra_specs, extra_args = ([slab], (w_next,)) if slab is not None else ([], ())
    if slab is not None:
        out_specs.append(slab)
        out_shape.append(jax.ShapeDtypeStruct(w_next.shape, BF16))
    outs = pl.pallas_call(
        functools.partial(_up_conv_gate_body, seq_rows=seq_rows, tm=tm, taps=taps, rounding=rounding,
                          with_next=slab is not None),
        grid=(nj, n_i),
        in_specs=[pl.BlockSpec((tm, d), lambda j, i: (i, 0))] + w_specs + [st(0), st(nj), cw(0), cw(nj), cb(0),
                                                                             cb(nj)] + extra_specs,
        out_specs=out_specs,
        out_shape=out_shape,
        scratch_shapes=[pltpu.VMEM(keep_shape, F32), pltpu.VMEM(keep_shape, F32)],
        compiler_params=_params("parallel", "arbitrary"),
        name="up_conv_gate",
    )(x, w_gate, w_val, state_rows, state_rows, conv_w, conv_w, conv_b.reshape(1, f2), conv_b.reshape(1, f2),
      *extra_args)
    act, tail_g, tail_v = outs[:3]
    w_bf = tuple(outs[3:5]) if rounding else w_up
    w_next_bf = outs[-1] if slab is not None else w_next.astype(BF16)
    tail = jnp.concatenate([tail_g, tail_v], axis=-1)
    if seq_rows >= tm:
        return act, tail.reshape(rows // seq_rows, SUBLANES, f2)[:, SUBLANES - (taps - 1):], w_bf, w_next_bf
    return act, jnp.swapaxes(tail, 0, 1), w_bf, w_next_bf


def _layer(x, shift_prev, wkv0, mem_k, mem_v, conv_prev, prm, want_v_rows):
    batch, seq_len, d = x.shape
    rows = batch * seq_len
    w = prm["width"]
    hd = prm["head_dim"]
    x2 = x.reshape(rows, d)

    xn = _rmsnorm(x2, prm["g_pre_mix"], BF16)
    n_shift = prm["mu_shift"].shape[1]
    proj = _matmul_cached(xn, prm, "w_in", 0, n_shift + 3 * w)
    new_shift = proj.reshape(batch, seq_len, -1)[:, -1, :n_shift]

    a_out, h_new = _rwkv(proj, _pad_state_rows(shift_prev[:, None, :]), jnp.swapaxes(wkv0, -1, -2), batch,
                         seq_len, prm)
    new_wkv = jnp.swapaxes(h_new, -1, -2)

    if seq_len % prm["sgu_chunk"] == 0:
        b_out, v_rows = _sgu(proj, n_shift, prm["w_s"], prm["sgu_bias"], prm, None, want_v_rows)
    else:
        assert seq_len == SUBLANES
        b_out, v_rows = _sgu(proj, n_shift, prm["w_s"], prm["sgu_bias"][:, :SUBLANES], prm, SUBLANES,
                             want_v_rows)
    c_out = _xattn(proj, n_shift + 2 * w, w, mem_k, mem_v, batch, seq_len, prm["xattn_heads"])

    mix, (prm["w_gates"], prm["w_branch"]), prm["w_out"] = _gate_branch_mix(
        xn, a_out, b_out, c_out, prm["w_gates"], n_shift + 3 * w, prm["w_branch"], prm["w_out"])
    h, hn = _mm_res_norm(mix, prm["w_out"], x2, prm["g_post_mix"], prm["g_pre_ffn"])

    act, conv_new, prm["w_up"], prm["w_down"] = _up_conv_gate(hn, prm["w_up"], conv_prev, prm["conv_w"],
                                                              prm["conv_b"], seq_len, prm["w_down"])
    y_out = _mm_res_norm(act, prm["w_down"], h, prm["g_post_ffn"])
    if want_v_rows:
        v_rows = v_rows.reshape(batch, seq_len, w)
    return y_out.reshape(batch, seq_len, d), new_shift, new_wkv, v_rows, conv_new


def _prepare(l, g_pre_mix, w_in, mu_shift, w0, w_decay, a0, w_aaa, w_gate, k_k, k_a, r_k, lnx_w, lnx_b,
             sgu_g, sgu_b, w_s, b_s, w_branch, w_out, g_post_mix, g_pre_ffn, w_up, conv_w, conv_b, w_down,
             g_post_ffn):
    heads, hd = r_k.shape[1], r_k.shape[2]
    w = heads * hd
    d = w_in.shape[1]
    n_shift = mu_shift.shape[1]
    rank_d, rank_a, rank_g = w_decay.shape[1], w_aaa.shape[1], w_gate.shape[1]
    assert rank_d + rank_a == LANES and rank_g == LANES and n_shift == 3 * w + 2 * LANES
    groups, sgu_chunk, _ = w_s.shape[1:]
    row = lambda t: t[l].reshape(1, -1)
    lane_head = jnp.arange(LANES) // hd
    gd = w // groups
    return dict(
        width=w, head_dim=hd, sgu_chunk=sgu_chunk,
        g_pre_mix=g_pre_mix[l],
        w_in=w_in[l], w_gates=w_in[l],
        mu_shift=row(mu_shift), w0=row(w0), a0=row(a0),
        wd_pad=jnp.pad(w_decay[l], ((0, rank_a), (0, 0))),
        wa_pad=jnp.pad(w_aaa[l], ((rank_d, 0), (0, 0))),
        w_gate=w_gate[l],
        k_k=row(k_k), k_a=row(k_a), r_k=row(r_k), lnx_w=row(lnx_w), lnx_b=row(lnx_b),
        ones2=(lane_head[:, None] == lane_head[None, :]).astype(BF16),
        sgu_g=row(sgu_g), sgu_b=row(sgu_b),
        w_s=w_s[l],
        sgu_bias=jnp.broadcast_to(b_s[l][:, :, None], (groups, sgu_chunk, gd)),
        w_branch=w_branch[l], w_out=w_out[l],
        g_post_mix=g_post_mix[l], g_pre_ffn=g_pre_ffn[l],
        w_up=w_up[l], conv_w=conv_w[l], conv_b=conv_b[l], w_down=w_down[l],
        g_post_ffn=g_post_ffn[l],
    )


def kernel(x_prompt, x_sample, mem_prompt, state_wkv, state_shift, cache_mem_k, cache_mem_v, state_ffn_conv, g_pre_mix, w_in, mu_shift, w0, w_decay, a0, w_aaa, w_gate, k_k, k_a, r_k, lnx_w, lnx_b, sgu_g, sgu_b, w_s, b_s, g_mem, w_mem_k, w_mem_v, w_branch, w_out, g_post_mix, g_pre_ffn, w_up, conv_w, conv_b, w_down, g_post_ffn):
    depth = w_in.shape[0]
    batch = x_prompt.shape[0]
    mem_len, d = mem_prompt.shape[1], mem_prompt.shape[2]
    xh, xhd = cache_mem_k.shape[3], cache_mem_k.shape[4]
    heads, hd = r_k.shape[1], r_k.shape[2]
    n_shift = mu_shift.shape[1]
    f2 = w_up.shape[2]
    taps = conv_w.shape[1]
    y_p, y_s = x_prompt, x_sample
    outs = [[] for _ in range(9)]
    for l in range(depth):
        prm = _prepare(l, g_pre_mix, w_in, mu_shift, w0, w_decay, a0, w_aaa, w_gate, k_k, k_a, r_k, lnx_w, lnx_b,
                       sgu_g, sgu_b, w_s, b_s, w_branch, w_out, g_post_mix, g_pre_ffn, w_up, conv_w, conv_b,
                       w_down, g_post_ffn)
        mn = _rmsnorm(mem_prompt.reshape(batch * mem_len, d), g_mem[l], BF16)
        prm["xattn_heads"] = xh
        mk_rows = _matmul(mn, w_mem_k[l])[0]
        mv_rows = _matmul(mn, w_mem_v[l])[0]
        mk_p = mk_rows.reshape(batch, mem_len, xh, xhd)
        mv_p = mv_rows.reshape(batch, mem_len, xh, xhd)
        y_p, sh_p, wkv_p, _, cv_p = _layer(
            y_p, jnp.zeros((batch, n_shift), F32), jnp.zeros((batch, heads, hd, hd), F32), mk_rows, mv_rows,
            jnp.zeros((batch, taps - 1, f2), F32), prm, False)
        y_s, sh_s, wkv_s, vr_s, cv_s = _layer(y_s, state_shift[l], state_wkv[l], cache_mem_k[l], cache_mem_v[l],
                                              state_ffn_conv[l], prm, True)
        for lst, val in zip(outs, (wkv_p, sh_p, mk_p, mv_p, cv_p, wkv_s, sh_s, vr_s, cv_s)):
            lst.append(val)
    return (y_p, y_s) + tuple(jnp.stack(lst) for lst in outs)
```

```python
import functools

import jax
import jax.numpy as jnp
from jax import lax
from jax.experimental import pallas as pl
from jax.experimental.pallas import tpu as pltpu

F32 = jnp.float32
BF16 = jnp.bfloat16
HIGHEST = lax.Precision.HIGHEST

LANES = 128
SUBLANES = 8
VMEM_LIMIT_BYTES = 56 * 1024 * 1024

RMS_EPS = 1e-6
LN_EPS = 1e-5
GN_EPS = 64e-5
KK_EPS = 1e-12
RWKV_CHUNK = 64
RWKV_GROUP = 32


def _params(*semantics):
    return pltpu.CompilerParams(dimension_semantics=semantics, vmem_limit_bytes=VMEM_LIMIT_BYTES)


def _row_tile(rows, pref):
    t = min(rows, pref)
    while rows % t:
        t -= SUBLANES
    return t


def _col_tile(n, cap):
    best = n
    for t in range(LANES, min(n, cap) + 1, LANES):
        if n % t == 0:
            best = t
    return best


def _bdot(a, b):
    return jnp.dot(a.astype(BF16), b.astype(BF16), preferred_element_type=F32)


def _bdot_nt(a, b):
    return lax.dot_general(a.astype(BF16), b.astype(BF16), (((1,), (1,)), ((), ())),
                           preferred_element_type=F32)


def _fdot(a, b):
    return jnp.dot(a, b, precision=HIGHEST, preferred_element_type=F32)


def _dot3(a, b):
    a_hi = a.astype(BF16)
    b_hi = b.astype(BF16)
    a_lo = (a - a_hi.astype(F32)).astype(BF16)
    b_lo = (b - b_hi.astype(F32)).astype(BF16)
    dot = lambda x, y: jnp.dot(x, y, preferred_element_type=F32)
    return dot(a_hi, b_hi) + (dot(a_hi, b_lo) + dot(a_lo, b_hi))


def _rmsnorm_body(x_ref, g_ref, o_ref):
    x = x_ref[...]
    y = x * lax.rsqrt(jnp.mean(x * x, axis=-1, keepdims=True) + RMS_EPS)
    o_ref[...] = (y * g_ref[...]).astype(o_ref.dtype)


def _rmsnorm(x, g, out_dtype):
    rows, d = x.shape
    tm = _row_tile(rows, 1024)
    return pl.pallas_call(
        _rmsnorm_body,
        grid=(rows // tm,),
        in_specs=[pl.BlockSpec((tm, d), lambda i: (i, 0)), pl.BlockSpec((1, d), lambda i: (0, 0))],
        out_specs=pl.BlockSpec((tm, d), lambda i: (i, 0)),
        out_shape=jax.ShapeDtypeStruct((rows, d), out_dtype),
        compiler_params=_params("parallel"),
        name="rmsnorm",
    )(x, g.reshape(1, d))


def _matmul_round_body(x_ref, w_ref, o_ref, wb_ref):
    @pl.when(pl.program_id(1) == 0)
    def _():
        wb_ref[...] = w_ref[...].astype(BF16)

    o_ref[...] = jnp.dot(x_ref[...], wb_ref[...], preferred_element_type=F32).astype(o_ref.dtype)


def _matmul_body(x_ref, w_ref, o_ref):
    o_ref[...] = jnp.dot(x_ref[...], w_ref[...], preferred_element_type=F32).astype(o_ref.dtype)


def _matmul(x, w, col_start=0, n_cols=None, out_dtype=F32):
    rows, k = x.shape
    n = w.shape[1] - col_start if n_cols is None else n_cols
    assert col_start % LANES == 0 and n % LANES == 0
    rounding = w.dtype != BF16
    tn = _col_tile(n, 1792)
    budget = VMEM_LIMIT_BYTES - (4 << 20)
    tm = _row_tile(rows, 1024)
    w_bytes = 2 * k * tn * (4 + 2) if rounding else 2 * k * tn * 2
    while w_bytes + 2 * tm * k * 2 + 2 * tm * tn * 4 > budget and tm % (2 * SUBLANES) == 0:
        tm //= 2
    w_spec = pl.BlockSpec((pl.Element(k), pl.Element(tn)),
                          lambda j, i: (0, pl.multiple_of(col_start + j * tn, LANES)))
    out_spec = pl.BlockSpec((tm, tn), lambda j, i: (i, j))
    out_shape = jax.ShapeDtypeStruct((rows, n), out_dtype)
    if rounding:
        out_spec = [out_spec, pl.BlockSpec((k, tn), lambda j, i: (0, j))]
        out_shape = [out_shape, jax.ShapeDtypeStruct((k, n), BF16)]
    return pl.pallas_call(
        _matmul_round_body if rounding else _matmul_body,
        grid=(n // tn, rows // tm),
        in_specs=[pl.BlockSpec((tm, k), lambda j, i: (i, 0)), w_spec],
        out_specs=out_spec,
        out_shape=out_shape,
        compiler_params=_params("parallel", "arbitrary"),
        name="matmul",
    )(x, w)


def _matmul_pair_body(x_ref, wa_ref, wb_ref, oa_ref, ob_ref):
    x = x_ref[...]
    oa_ref[...] = jnp.dot(x, wa_ref[...].astype(BF16), preferred_element_type=F32)
    ob_ref[...] = jnp.dot(x, wb_ref[...].astype(BF16), preferred_element_type=F32)


def _matmul_pair(x, wa, wb):
    rows, k = x.shape
    n = wa.shape[1]
    assert wa.shape == wb.shape
    tm = _row_tile(rows, 512)
    w_spec = pl.BlockSpec((k, n), lambda i: (0, 0))
    out_spec = pl.BlockSpec((tm, n), lambda i: (i, 0))
    return pl.pallas_call(
        _matmul_pair_body,
        grid=(rows // tm,),
        in_specs=[pl.BlockSpec((tm, k), lambda i: (i, 0)), w_spec, w_spec],
        out_specs=[out_spec, out_spec],
        out_shape=[jax.ShapeDtypeStruct((rows, n), F32)] * 2,
        compiler_params=_params("arbitrary"),
        name="matmul_pair",
    )(x, wa, wb)


def _matmul_cached(x, prm, name, col_start=0, n_cols=None):
    key = (name, col_start)
    if key in prm:
        return _matmul(x, prm[key])
    out, prm[key] = _matmul(x, prm[name], col_start, n_cols)
    return out


def _shift_rows(x, halo, k):
    rolled = pltpu.roll(x, k, 0)
    row = lax.broadcasted_iota(jnp.int32, (SUBLANES, x.shape[1]), 0)
    top = rolled[0:SUBLANES]
    for j in range(k):
        top = jnp.where(row == j, halo[SUBLANES - k + j:SUBLANES - k + j + 1, :], top)
    return jnp.concatenate([top, rolled[SUBLANES:]], axis=0)


def _head_sum_operand(x):
    hi = x.astype(BF16)
    lo = (x - hi.astype(F32)).astype(BF16)
    tiles = x.shape[1] // LANES
    return jnp.concatenate([part[:, j * LANES:(j + 1) * LANES] for part in (hi, lo) for j in range(tiles)], axis=0)


def _head_sum_result(s, n, tiles):
    return jnp.concatenate([s[j * n:(j + 1) * n] + s[(tiles + j) * n:(tiles + j + 1) * n] for j in range(tiles)],
                           axis=1)


def _head_sum(x, ones_ref):
    s = jnp.dot(_head_sum_operand(x), ones_ref[...], preferred_element_type=F32)
    return _head_sum_result(s, x.shape[0], x.shape[1] // LANES)


def _pad_state_rows(state):
    b, n, f = state.shape
    return jnp.pad(state, ((0, 0), (0, SUBLANES - n), (0, 0))).reshape(b * SUBLANES, f)


def _rwkv_inputs(x, prev, mu_ref, w0_ref, wd_ref, a0_ref, wa_ref, wg_ref, kk_ref, ka_ref, rk_ref, ones_ref, width):
    xs = x + mu_ref[...] * (prev - x)
    w = width
    n = x.shape[0]
    r = xs[:, 0:w]
    k = xs[:, w:2 * w]
    v = xs[:, 2 * w:3 * w]
    x_wa = xs[:, 3 * w:3 * w + LANES]
    x_g = xs[:, 3 * w + LANES:3 * w + 2 * LANES]

    z = -(w0_ref[...] + _dot3(jnp.tanh(x_wa), wd_ref[...]))
    softplus = jnp.maximum(z, 0.0) + jnp.log(1.0 + jnp.exp(-jnp.abs(z)))
    lw = -jnp.exp(-softplus - 0.5)
    a_lr = jax.nn.sigmoid(a0_ref[...] + _bdot(x_wa, wa_ref[...]))
    g = _bdot(jax.nn.sigmoid(x_g), wg_ref[...])

    kk = k * kk_ref[...]
    k_mod = k * (1.0 + (a_lr - 1.0) * ka_ref[...])
    sums = _head_sum(jnp.concatenate([kk * kk, r * k_mod * rk_ref[...]], axis=0), ones_ref)
    kk = kk / jnp.maximum(jnp.sqrt(sums[:n]), KK_EPS)
    return r, lw, k_mod, v, -kk, kk * a_lr, g, sums[n:] * v


def _rwkv_body(rw_ref, sp_ref, mu_ref, w0_ref, wd_ref, a0_ref, wa_ref, wg_ref, kk_ref, ka_ref, rk_ref, ones_ref,
               lnw_ref, lnb_ref, h0_ref, o_ref, hout_ref, h_scr, carry, *, chunk, pairs, seqs, width, hd, n_chunks, group):
    c = pl.program_id(1)
    n_pairs = width // LANES

    @pl.when(c == 0)
    def _():
        zero = jnp.zeros((hd, hd), F32)
        for i in range(seqs):
            for p in range(n_pairs):
                h_scr[i, p] = jnp.concatenate(
                    [jnp.concatenate([h0_ref[i, 2 * p], zero], axis=1),
                     jnp.concatenate([zero, h0_ref[i, 2 * p + 1]], axis=1)], axis=0)
        carry[...] = jnp.zeros_like(carry)

    if n_chunks > 1:
        x = rw_ref[...].reshape(seqs * chunk, rw_ref.shape[2])
        prevs = []
        for i in range(seqs):
            xi = x[i * chunk:(i + 1) * chunk]
            before = jnp.where(c == 0, pltpu.roll(sp_ref[i * SUBLANES:(i + 1) * SUBLANES, :], SUBLANES - 1, 0),
                               carry[i])
            prevs.append(_shift_rows(xi, before, 1))
            carry[i] = xi[chunk - SUBLANES:]
        prev = jnp.concatenate(prevs, axis=0)
    else:
        x = rw_ref[...]
        t = lax.broadcasted_iota(jnp.int32, x.shape, 0) & (chunk - 1)
        prev = jnp.where(t == 0, sp_ref[...], pltpu.roll(x, 1, 0))
    r_all, lw_all, k_all, v_all, a_all, b_all, g_all, bonus_all = _rwkv_inputs(
        x, prev, mu_ref, w0_ref, wd_ref, a0_ref, wa_ref, wg_ref, kk_ref, ka_ref, rk_ref, ones_ref, width)

    cs = chunk
    lanes = LANES * pairs
    heads = 2 * pairs
    n = heads * cs
    log_c = cs.bit_length() - 1
    log_hd = hd.bit_length() - 1
    n_stacks = width // lanes

    tri = (lax.broadcasted_iota(jnp.int32, (cs, cs), 0) >= lax.broadcasted_iota(jnp.int32, (cs, cs), 1)).astype(F32)
    row = lax.broadcasted_iota(jnp.int32, (n, lanes), 0)
    lane = lax.broadcasted_iota(jnp.int32, (n, lanes), 1)
    head_mask = (row >> log_c) == (lane >> log_hd)
    ri = lax.broadcasted_iota(jnp.int32, (n, n), 0)
    ci = lax.broadcasted_iota(jnp.int32, (n, n), 1)
    same_head = (ri >> log_c) == (ci >> log_c)
    rt = ri & (cs - 1)
    ct = ci & (cs - 1)
    strict = same_head & (ct < rt)
    incl = same_head & (ct <= rt)
    eye = (ri == ci).astype(F32)
    diag = (lax.broadcasted_iota(jnp.int32, (LANES, LANES), 0)
            == lax.broadcasted_iota(jnp.int32, (LANES, LANES), 1))

    def stack(x):
        return jnp.where(head_mask, jnp.concatenate([x] * heads, axis=0), 0.0)

    def fold(x):
        out = x[0:cs]
        for hidx in range(1, heads):
            out = out + x[hidx * cs:(hidx + 1) * cs]
        return out

    scaled = []
    for i in range(seqs):
        rs = slice(i * cs, (i + 1) * cs)
        lw = lw_all[rs]
        cum = _fdot(tri, lw)
        total = cum[cs - 1:cs]
        e_cum = jnp.exp(cum)
        e_neg = jnp.exp(-cum)
        e_rest = jnp.exp(total - cum)
        b = b_all[rs]
        k = k_all[rs]
        scaled.append(dict(a=a_all[rs] * jnp.exp(cum - lw), r=r_all[rs] * e_cum, b=b * e_neg, k=k * e_neg,
                           v=v_all[rs], bh=b * e_rest, kh=k * e_rest, total=total))

    def pair_lanes(x, p, off=0):
        return x[:, off + p * LANES:off + (p + 1) * LANES]

    def recur(inst):
        jj = range(len(inst))

        def stacked(name):
            return [stack(scaled[i][name][:, s * lanes:(s + 1) * lanes]) for i, s in inst]

        a_st, r_st, b_st, k_st, v_st, bh_st, kh_st = (stacked(nm) for nm in ("a", "r", "b", "k", "v", "bh", "kh"))

        gram = [_bdot_nt(jnp.concatenate([a_st[j], r_st[j]], axis=0), jnp.concatenate([b_st[j], k_st[j]], axis=0))
                for j in jj]
        l_ab = [jnp.where(strict, gram[j][:n, :n], 0.0) for j in jj]
        m_rb = [jnp.where(incl, gram[j][n:, :n], 0.0) for j in jj]
        l_ak_m_rk = [jnp.concatenate([jnp.where(strict, gram[j][:n, n:], 0.0),
                                      jnp.where(incl, gram[j][n:, n:], 0.0)], axis=0) for j in jj]

        t_inv = [eye + l_ab[j] for j in jj]
        x_pow = [_bdot(l_ab[j], l_ab[j]) for j in jj]
        for _ in range(log_c - 2):
            both = [_bdot(jnp.concatenate([x_pow[j], t_inv[j]], axis=0), x_pow[j]) for j in jj]
            t_inv = [t_inv[j] + both[j][n:] for j in jj]
            x_pow = [both[j][:n] for j in jj]
        t_inv = [t_inv[j] + _bdot(t_inv[j], x_pow[j]) for j in jj]

        lv = [_bdot(l_ak_m_rk[j], v_st[j]) for j in jj]
        aw = [_bdot(t_inv[j], jnp.concatenate([a_st[j], lv[j][:n]], axis=1)) for j in jj]
        rb = [_bdot(m_rb[j], aw[j]) for j in jj]
        r_bar = [fold(r_st[j] + rb[j][:, :lanes]) for j in jj]
        y0 = [fold(rb[j][:, lanes:] + lv[j][n:]) for j in jj]

        jp = [(j, p) for j in jj for p in range(pairs)]
        h = [h_scr[inst[j][0], inst[j][1] * pairs + p] for j, p in jp]
        y_blocks = [_bdot(pair_lanes(r_bar[j], p), h[q]) + pair_lanes(y0[j], p) for q, (j, p) in enumerate(jp)]
        bh_t = [pair_lanes(bh_st[j], p).T for j, p in jp]
        kh_t = [pair_lanes(kh_st[j], p).T for j, p in jp]
        pp = [_bdot(bh_t[q], jnp.concatenate([pair_lanes(aw[j], p), pair_lanes(aw[j], p, lanes)], axis=1))
              for q, (j, p) in enumerate(jp)]
        kv = [_bdot(kh_t[q], pair_lanes(v_st[j], p)) for q, (j, p) in enumerate(jp)]
        for q, (j, p) in enumerate(jp):
            i, s = inst[j]
            decay = pair_lanes(scaled[i]["total"], p, s * lanes)
            phi = jnp.where(diag, jnp.exp(decay), 0.0) + pp[q][:, :LANES]
            h_scr[i, s * pairs + p] = _bdot(phi, h[q]) + (pp[q][:, LANES:] + kv[q])
        return y_blocks

    all_inst = [(i, s) for i in range(seqs) for s in range(n_stacks)]
    yv = []
    for g0 in range(0, len(all_inst), group):
        yv += recur(all_inst[g0:g0 + group])

    per_seq = n_stacks * pairs
    y = jnp.concatenate([jnp.concatenate(yv[i * per_seq:(i + 1) * per_seq], axis=1) for i in range(seqs)], axis=0)
    mean = _head_sum(y, ones_ref) * (1.0 / hd)
    dev = y - mean
    var = _head_sum(dev * dev, ones_ref) * (1.0 / hd)
    yn = dev * lax.rsqrt(var + GN_EPS) * lnw_ref[...] + lnb_ref[...]
    o_ref[...] = ((yn + bonus_all) * g_all).astype(o_ref.dtype).reshape(o_ref.shape)

    @pl.when(c == pl.num_programs(1) - 1)
    def _():
        for i in range(seqs):
            for p in range(n_pairs):
                both = h_scr[i, p]
                hout_ref[i, 2 * p] = both[:hd, :hd]
                hout_ref[i, 2 * p + 1] = both[hd:, hd:]


def _rwkv(proj, shift_rows, h0, batch, seq_len, prm):
    rows = proj.shape[0]
    w, hd = prm["width"], prm["head_dim"]
    n_shift = prm["mu_shift"].shape[1]
    chunk = min(seq_len, RWKV_CHUNK)
    assert seq_len % chunk == 0 and chunk % SUBLANES == 0 and LANES % chunk == 0
    pairs = LANES // (2 * chunk)
    n_chunks = seq_len // chunk
    n_pairs = w // LANES
    target = RWKV_GROUP if n_chunks > 1 else 8
    seqs = 1
    while seqs * (n_pairs // pairs) < target and batch % (2 * seqs) == 0:
        seqs *= 2
    if n_chunks > 1:
        proj = proj.reshape(batch, seq_len, proj.shape[1])
        vec = lambda cols: pl.BlockSpec((seqs, chunk, cols), lambda bi, ci: (bi, ci, 0))
        out_shape = (batch, seq_len, w)
    else:
        vec = lambda cols: pl.BlockSpec((seqs * chunk, cols), lambda bi, ci: (bi, 0))
        out_shape = (rows, w)
    st = pl.BlockSpec((seqs, w // hd, hd, hd), lambda bi, ci: (bi, 0, 0, 0))
    row_spec = lambda c: pl.BlockSpec((1, c), lambda bi, ci: (0, 0))
    full = lambda a: pl.BlockSpec(a.shape, lambda bi, ci: (0, 0))
    a_out, h_new = pl.pallas_call(
        functools.partial(_rwkv_body, chunk=chunk, pairs=pairs, seqs=seqs, width=w, hd=hd, n_chunks=n_chunks,
                          group=RWKV_GROUP),
        grid=(batch // seqs, n_chunks),
        in_specs=[vec(n_shift), pl.BlockSpec((seqs * SUBLANES, n_shift), lambda bi, ci: (bi, 0)),
                  row_spec(n_shift), row_spec(w), full(prm["wd_pad"]), row_spec(w), full(prm["wa_pad"]),
                  full(prm["w_gate"]), row_spec(w), row_spec(w), row_spec(w), full(prm["ones2"]),
                  row_spec(w), row_spec(w), st],
        out_specs=[vec(w), st],
        out_shape=[jax.ShapeDtypeStruct(out_shape, BF16),
                   jax.ShapeDtypeStruct((batch, w // hd, hd, hd), F32)],
        scratch_shapes=[pltpu.VMEM((seqs, n_pairs, LANES, LANES), F32),
                        pltpu.VMEM((seqs, SUBLANES, n_shift), F32)],
        compiler_params=_params("parallel", "arbitrary"),
        name="rwkv",
    )(proj, shift_rows, prm["mu_shift"], prm["w0"], prm["wd_pad"], prm["a0"], prm["wa_pad"], prm["w_gate"],
      prm["k_k"], prm["k_a"], prm["r_k"], prm["ones2"], prm["lnx_w"], prm["lnx_b"], h0)
    return a_out.reshape(rows, w), h_new


def _sgu_body(gu_ref, gv_ref, sg_ref, sb_ref, wm_ref, bias_ref, o_ref, *maybe_v, groups, rows, seq_rows):
    u = jax.nn.gelu(gu_ref[...])
    vf = jax.nn.gelu(gv_ref[...])
    mu = jnp.mean(vf, axis=-1, keepdims=True)
    d = vf - mu
    var = jnp.mean(d * d, axis=-1, keepdims=True)
    v = (d * lax.rsqrt(var + LN_EPS)) * sg_ref[...] + sb_ref[...]
    if maybe_v:
        maybe_v[0][...] = v
    gd = v.shape[1] // groups
    ri = lax.broadcasted_iota(jnp.int32, (rows, rows), 0)
    ci = lax.broadcasted_iota(jnp.int32, (rows, rows), 1)
    causal = ri >= ci
    if seq_rows is not None:
        shift = seq_rows.bit_length() - 1
        causal = causal & ((ri >> shift) == (ci >> shift))
        pos = ((lax.broadcasted_iota(jnp.int32, (rows, seq_rows), 0) & (seq_rows - 1))
               == lax.broadcasted_iota(jnp.int32, (rows, seq_rows), 1)).astype(F32)
    for g in range(groups):
        sl = slice(g * gd, (g + 1) * gd)
        if seq_rows is None:
            w_full, bias = wm_ref[g], bias_ref[g]
        else:
            w_full = lax.dot_general(_fdot(pos, wm_ref[g][:seq_rows, :seq_rows]), pos, (((1,), (1,)), ((), ())),
                                     precision=HIGHEST, preferred_element_type=F32)
            bias = _fdot(pos, bias_ref[g])
        w_causal = jnp.where(causal, w_full, 0.0).astype(BF16)
        for blk in range(v.shape[0] // rows):
            rs = slice(blk * rows, (blk + 1) * rows)
            s = _bdot(w_causal, v[rs, sl]) + bias
            o_ref[rs, sl] = (u[rs, sl] * s).astype(o_ref.dtype)


def _col_window(rows_per_step, width, col0, row_of):
    return pl.BlockSpec((pl.Element(rows_per_step), pl.Element(width)),
                        lambda *idx: (pl.multiple_of(row_of(*idx) * rows_per_step, SUBLANES), col0))


def _sgu(proj, gu_col, w_mix, bias, prm, seq_rows, want_v_rows):
    rows = proj.shape[0]
    w = prm["width"]
    groups, chunk, _ = w_mix.shape
    mix_rows = chunk if seq_rows is None else _row_tile(rows, 2 * LANES)
    cs = mix_rows
    while cs < 1024 and rows % (2 * cs) == 0:
        cs *= 2
    row_spec = pl.BlockSpec((1, w), lambda i: (0, 0))
    out = pl.BlockSpec((cs, w), lambda i: (i, 0))
    outs = pl.pallas_call(
        functools.partial(_sgu_body, groups=groups, rows=mix_rows, seq_rows=seq_rows),
        grid=(rows // cs,),
        in_specs=[_col_window(cs, w, gu_col, lambda i: i), _col_window(cs, w, gu_col + w, lambda i: i),
                  row_spec, row_spec,
                  pl.BlockSpec(w_mix.shape, lambda i: (0, 0, 0)), pl.BlockSpec(bias.shape, lambda i: (0, 0, 0))],
        out_specs=[out, out] if want_v_rows else [out],
        out_shape=[jax.ShapeDtypeStruct((rows, w), BF16)] + ([jax.ShapeDtypeStruct((rows, w), F32)]
                                                           if want_v_rows else []),
        compiler_params=_params("parallel"),
        name="sgu",
    )(proj, proj, prm["sgu_g"], prm["sgu_b"], w_mix, bias)
    return (outs[0], outs[1]) if want_v_rows else (outs[0], None)


def _xattn_body(q_ref, k_ref, v_ref, o_ref, *, heads, seqs, tq, m):
    hd = q_ref.shape[1] // heads
    scale = hd ** -0.5
    cases = [(h, i) for h in range(heads) for i in range(seqs)]
    cols = lambda h: slice(h * hd, (h + 1) * hd)
    mem = lambda i: slice(i * m, (i + 1) * m)
    s = [_bdot_nt(q_ref[i * tq:(i + 1) * tq, cols(h)], k_ref[mem(i), cols(h)]) * scale for h, i in cases]
    e = [jnp.exp(x - jnp.max(x, axis=-1, keepdims=True)) for x in s]
    p = [x / jnp.sum(x, axis=-1, keepdims=True) for x in e]
    o = [_bdot(p[c], v_ref[mem(i), cols(h)]) for c, (h, i) in enumerate(cases)]
    for h in range(heads):
        o_ref[:, cols(h)] = jnp.concatenate(o[h * seqs:(h + 1) * seqs], axis=0).astype(o_ref.dtype)


def _xattn_cache_body(q_ref, k_ref, v_ref, o_ref, *, heads, seqs, tq):
    hd = q_ref.shape[1] // heads
    m = k_ref.shape[1]
    n = m * heads
    scale = hd ** -0.5
    row_head = lax.broadcasted_iota(jnp.int32, (heads * tq, n), 0) >> (tq.bit_length() - 1)
    lane_head = lax.broadcasted_iota(jnp.int32, (heads * tq, n), 1) & (heads - 1)
    own = row_head == lane_head
    q = [jnp.concatenate([q_ref[i * tq:(i + 1) * tq, h * hd:(h + 1) * hd] for h in range(heads)], axis=0)
         for i in range(seqs)]
    s = [jnp.where(own, _bdot_nt(q[i], k_ref[i].reshape(n, hd)) * scale, -1e30) for i in range(seqs)]
    e = [jnp.exp(x - jnp.max(x, axis=-1, keepdims=True)) for x in s]
    p = [x / jnp.sum(x, axis=-1, keepdims=True) for x in e]
    o = [_bdot(p[i], v_ref[i].reshape(n, hd)) for i in range(seqs)]
    for h in range(heads):
        o_ref[:, h * hd:(h + 1) * hd] = jnp.concatenate(
            [o[i][h * tq:(h + 1) * tq] for i in range(seqs)], axis=0).astype(o_ref.dtype)


def _xattn(proj, q_col, w, mem_k, mem_v, batch, seq_len, heads):
    rows = proj.shape[0]
    tq = _row_tile(seq_len, 1024)
    nq = seq_len // tq
    seqs = max(1, 32 // tq)
    assert batch % seqs == 0 and (seqs == 1 or nq == 1)
    if mem_k.ndim == 2:
        m = mem_k.shape[0] // batch
        kv = pl.BlockSpec((seqs * m, w), lambda bi, qi: (bi, 0))
        body = functools.partial(_xattn_body, heads=heads, seqs=seqs, tq=tq, m=m)
    else:
        assert tq & (tq - 1) == 0 and heads & (heads - 1) == 0
        kv = pl.BlockSpec((seqs,) + mem_k.shape[1:], lambda bi, qi: (bi, 0, 0, 0))
        body = functools.partial(_xattn_cache_body, heads=heads, seqs=seqs, tq=tq)
    return pl.pallas_call(
        body,
        grid=(batch // seqs, nq),
        in_specs=[_col_window(seqs * tq, w, q_col, lambda bi, qi: bi * nq + qi), kv, kv],
        out_specs=pl.BlockSpec((seqs * tq, w), lambda bi, qi: (bi * nq + qi, 0)),
        out_shape=jax.ShapeDtypeStruct((rows, w), BF16),
        compiler_params=_params("parallel", "parallel"),
        name="mem_xattn",
    )(proj, mem_k, mem_v)


def _slab_spec(weight, grid_steps, step_of):
    k, d = weight.shape
    if weight.dtype == BF16 or k % grid_steps or (k // grid_steps) % (2 * SUBLANES):
        return None
    return pl.BlockSpec((k // grid_steps, d), lambda j, i: (step_of(j, i), 0))


def _gate_branch_body(*refs, rounding, with_next):
    if with_next:
        *refs, next_bf = refs
        next_ref = refs.pop(8)
        next_bf[...] = next_ref[...].astype(BF16)
    if rounding:
        x_ref, a_ref, b_ref, c_ref, wg0_ref, wg1_ref, wg2_ref, wb_ref, o_ref, wg_bf, wb_bf = refs

        @pl.when(pl.program_id(1) == 0)
        def _():
            for n, src in enumerate((wg0_ref, wg1_ref, wg2_ref)):
                wg_bf[n] = src[...].astype(BF16)
            wb_bf[...] = wb_ref[...].astype(BF16)
    else:
        x_ref, a_ref, b_ref, c_ref, wg_bf, wb_bf, o_ref = refs

    x = x_ref[...]
    acc = None
    for n, br_ref in enumerate((a_ref, b_ref, c_ref)):
        gate = jax.nn.sigmoid(jnp.dot(x, wg_bf[n], preferred_element_type=F32))
        term = gate * jnp.dot(br_ref[...], wb_bf[n], preferred_element_type=F32)
        acc = term if acc is None else acc + term
    o_ref[...] = acc.astype(o_ref.dtype)


def _gate_branch_mix(xn, a_out, b_out, c_out, w_gates, gate_col0, w_branch, w_next):
    rows, k = xn.shape
    w = a_out.shape[1]
    nb, _, d = w_branch.shape
    assert nb == 3
    rounding = w_branch.dtype != BF16
    tm = _row_tile(rows, 1024)
    tn = _col_tile(d, 256)
    nj = d // tn
    act = lambda width: pl.BlockSpec((tm, width), lambda j, i: (i, 0))
    wb_spec = pl.BlockSpec((nb, w, tn), lambda j, i: (0, 0, j))
    wg_spec = pl.BlockSpec((nb, k, tn), lambda j, i: (0, 0, j))
    out_specs = [pl.BlockSpec((tm, tn), lambda j, i: (i, j))]
    out_shape = [jax.ShapeDtypeStruct((rows, d), BF16)]
    if rounding:
        gate_cols = lambda n: pl.BlockSpec(
            (pl.Element(k), pl.Element(tn)),
            lambda j, i: (0, pl.multiple_of(gate_col0 + n * d + j * tn, LANES)))
        w_specs = [gate_cols(0), gate_cols(1), gate_cols(2), wb_spec]
        w_args = (w_gates, w_gates, w_gates, w_branch)
        out_specs += [wg_spec, wb_spec]
        out_shape += [jax.ShapeDtypeStruct((nb, k, d), BF16), jax.ShapeDtypeStruct(w_branch.shape, BF16)]
    else:
        w_specs = [wg_spec, wb_spec]
        w_args = (w_gates, w_branch)
    n_i = rows // tm
    slab = _slab_spec(w_next, nj * n_i, lambda j, i: j * n_i + i) if rounding else None
    if slab is not None:
        w_specs, w_args = w_specs + [slab], w_args + (w_next,)
        out_specs.append(slab)
        out_shape.append(jax.ShapeDtypeStruct(w_next.shape, BF16))
    outs = pl.pallas_call(
        functools.partial(_gate_branch_body, rounding=rounding, with_next=slab is not None),
        grid=(nj, n_i),
        in_specs=[act(k), act(w), act(w), act(w)] + w_specs,
        out_specs=out_specs,
        out_shape=out_shape,
        compiler_params=_params("parallel", "arbitrary"),
        name="gate_branch_mix",
    )(xn, a_out, b_out, c_out, *w_args)
    w_next_bf = outs[-1] if slab is not None else w_next.astype(BF16)
    return outs[0], ((outs[1], outs[2]) if rounding else (w_gates, w_branch)), w_next_bf


def _mm_res_norm_body(x_ref, w_ref, res_ref, g_ref, g2_ref, y_ref, *maybe_next, with_next):
    f = jnp.dot(x_ref[...], w_ref[...], preferred_element_type=F32)
    y = res_ref[...] + (f * lax.rsqrt(jnp.mean(f * f, axis=-1, keepdims=True) + RMS_EPS)) * g_ref[...]
    y_ref[...] = y
    if with_next:
        yn = y * lax.rsqrt(jnp.mean(y * y, axis=-1, keepdims=True) + RMS_EPS)
        maybe_next[0][...] = (yn * g2_ref[...]).astype(maybe_next[0].dtype)


def _mm_res_norm(x, w, res, g, g_next=None):
    rows, k = x.shape
    d = w.shape[1]
    with_next = g_next is not None
    tm = _row_tile(rows, 512)
    blk = pl.BlockSpec((tm, d), lambda i: (i, 0))
    row_spec = pl.BlockSpec((1, d), lambda i: (0, 0))
    out_specs = [blk, blk] if with_next else [blk]
    out_shape = [jax.ShapeDtypeStruct((rows, d), F32)]
    if with_next:
        out_shape.append(jax.ShapeDtypeStruct((rows, d), BF16))
    g2 = g_next if with_next else g
    outs = pl.pallas_call(
        functools.partial(_mm_res_norm_body, with_next=with_next),
        grid=(rows // tm,),
        in_specs=[pl.BlockSpec((tm, k), lambda i: (i, 0)),
                  pl.BlockSpec((k, d), lambda i: (0, 0), pipeline_mode=pl.Buffered(1)),
                  blk, row_spec, row_spec],
        out_specs=out_specs,
        out_shape=out_shape,
        compiler_params=_params("arbitrary"),
        name="matmul_res_norm",
    )(x, w, res, g.reshape(1, d), g2.reshape(1, d))
    return outs if with_next else outs[0]


def _up_conv_gate_body(x_ref, wg_ref, wv_ref, eg_ref, ev_ref, cwg_ref, cwv_ref, cbg_ref, cbv_ref,
                       *rest, seq_rows, tm, taps, rounding, with_next):
    i = pl.program_id(1)
    rest = list(rest)
    if with_next:
        next_ref = rest.pop(0)
        next_bf = rest.pop(-3)
        next_bf[...] = next_ref[...].astype(BF16)
    o_ref, tg_ref, tv_ref = rest[:3]
    keep_g, keep_v = rest[-2:]
    wg_bf, wv_bf = rest[3:5] if rounding else (wg_ref, wv_ref)

    @pl.when(i == 0)
    def _():
        if rounding:
            wg_bf[...] = wg_ref[...].astype(BF16)
            wv_bf[...] = wv_ref[...].astype(BF16)
        keep_g[...] = jnp.zeros_like(keep_g)
        keep_v[...] = jnp.zeros_like(keep_v)

    if seq_rows >= tm:
        is_start = (i * tm) % seq_rows == 0
        tn = o_ref.shape[1]
        row8 = lax.broadcasted_iota(jnp.int32, (SUBLANES, tn), 0)

        def first_halo(e_ref, keep):
            start = jnp.zeros((SUBLANES, tn), F32)
            for idx in range(taps - 1):
                start = jnp.where(row8 == SUBLANES - (taps - 1) + idx, e_ref[0, idx:idx + 1, :], start)
            return jnp.where(is_start, start, keep[...])

        def conv_rows(u, halo, cw, cb):
            acc = cb + cw[taps - 1:taps] * u
            for back in range(1, taps):
                acc = acc + cw[taps - 1 - back:taps - back] * _shift_rows(u, halo, back)
            return acc

        halo_g, halo_v = first_halo(eg_ref, keep_g), first_halo(ev_ref, keep_v)
        cwg, cwv, cbg, cbv = cwg_ref[...], cwv_ref[...], cbg_ref[...], cbv_ref[...]
        xt = x_ref[...]
        ug = jnp.dot(xt, wg_bf[...], preferred_element_type=F32)
        uv = jnp.dot(xt, wv_bf[...], preferred_element_type=F32)
        o_ref[...] = (jax.nn.gelu(conv_rows(ug, halo_g, cwg, cbg)) * conv_rows(uv, halo_v, cwv, cbv)).astype(o_ref.dtype)
        for keep, tail_ref, u in ((keep_g, tg_ref, ug), (keep_v, tv_ref, uv)):
            keep[...] = u[tm - SUBLANES:tm]
            tail_ref[...] = u[tm - SUBLANES:tm]
        return

    x = x_ref[...]

    def conv(w_bf, e_ref, cw_ref, cb_ref, keep, tail_ref):
        u = jnp.dot(x, w_bf[...], preferred_element_type=F32)
        cw = cw_ref[...]
        acc = cb_ref[...] + cw[taps - 1:taps] * u
        n_lane_tiles = u.shape[1] // LANES
        t = lax.broadcasted_iota(jnp.int32, u.shape, 0) & (seq_rows - 1)
        for idx in range(taps - 1):
            for c in range(n_lane_tiles):
                keep[c, pl.ds(idx, tm // seq_rows, stride=seq_rows), :] = e_ref[:, idx, c * LANES:(c + 1) * LANES]
        e = jnp.concatenate([keep[c] for c in range(n_lane_tiles)], axis=1)
        for back in range(1, taps):
            up_by = taps - 1 - back
            state = pltpu.roll(e, tm - up_by, 0) if up_by else e
            prev = jnp.where(t < back, state, pltpu.roll(u, back, 0))
            acc = acc + cw[taps - 1 - back:taps - back] * prev
        for c in range(n_lane_tiles):
            keep[c] = u[:, c * LANES:(c + 1) * LANES]
        for idx in range(taps - 1):
            rows_t = pl.ds(seq_rows - (taps - 1) + idx, tm // seq_rows, stride=seq_rows)
            tail_ref[idx] = jnp.concatenate([keep[c, rows_t, :] for c in range(n_lane_tiles)], axis=1)
        return acc

    gate = conv(wg_bf, eg_ref, cwg_ref, cbg_ref, keep_g, tg_ref)
    val = conv(wv_bf, ev_ref, cwv_ref, cbv_ref, keep_v, tv_ref)
    o_ref[...] = (jax.nn.gelu(gate) * val).astype(o_ref.dtype)


def _up_conv_gate(x, w_up, state_rows, conv_w, conv_b, seq_rows, w_next):
    rows, d = x.shape
    rounding = not isinstance(w_up, tuple)
    f2 = conv_w.shape[1]
    dff = f2 // 2
    taps = conv_w.shape[0]
    tm = _row_tile(rows if seq_rows == SUBLANES else seq_rows, 1024)
    tn = _col_tile(dff, 512)
    nj = dff // tn
    if seq_rows >= tm:
        assert seq_rows % tm == 0
        st = lambda off: pl.BlockSpec((1, taps - 1, tn), lambda j, i: ((i * tm) // seq_rows, 0, j + off))
        tail = pl.BlockSpec((SUBLANES, tn), lambda j, i: ((i * tm) // seq_rows, j))
        tail_shape = ((rows // seq_rows) * SUBLANES, dff)
        keep_shape = (SUBLANES, tn)
    else:
        assert seq_rows == SUBLANES and tm % seq_rows == 0
        st = lambda off: pl.BlockSpec((tm // seq_rows, taps - 1, tn), lambda j, i: (i, 0, j + off))
        tail = pl.BlockSpec((taps - 1, tm // seq_rows, tn), lambda j, i: (0, i, j))
        tail_shape = (taps - 1, rows // seq_rows, dff)
        keep_shape = (tn // LANES, tm, LANES)
    wt = lambda off: pl.BlockSpec((d, tn), lambda j, i: (0, j + off))
    cw = lambda off: pl.BlockSpec((taps, tn), lambda j, i: (0, j + off))
    cb = lambda off: pl.BlockSpec((1, tn), lambda j, i: (0, j + off))
    out_specs = [pl.BlockSpec((tm, tn), lambda j, i: (i, j)), tail, tail]
    out_shape = [jax.ShapeDtypeStruct((rows, dff), BF16), jax.ShapeDtypeStruct(tail_shape, F32),
                 jax.ShapeDtypeStruct(tail_shape, F32)]
    if rounding:
        w_gate, w_val, w_specs = w_up, w_up, [wt(0), wt(nj)]
        out_specs += [wt(0), wt(0)]
        out_shape += [jax.ShapeDtypeStruct((d, dff), BF16)] * 2
    else:
        (w_gate, w_val), w_specs = w_up, [wt(0), wt(0)]
    n_i = rows // tm
    slab = _slab_spec(w_next, nj * n_i, lambda j, i: j * n_i + i) if rounding else None
    extra_specs, extra_args = ([slab], (w_next,)) if slab is not None else ([], ())
    if slab is not None:
        out_specs.append(slab)
        out_shape.append(jax.ShapeDtypeStruct(w_next.shape, BF16))
    outs = pl.pallas_call(
        functools.partial(_up_conv_gate_body, seq_rows=seq_rows, tm=tm, taps=taps, rounding=rounding,
                          with_next=slab is not None),
        grid=(nj, n_i),
        in_specs=[pl.BlockSpec((tm, d), lambda j, i: (i, 0))] + w_specs + [st(0), st(nj), cw(0), cw(nj), cb(0),
                                                                             cb(nj)] + extra_specs,
        out_specs=out_specs,
        out_shape=out_shape,
        scratch_shapes=[pltpu.VMEM(keep_shape, F32), pltpu.VMEM(keep_shape, F32)],
        compiler_params=_params("parallel", "arbitrary"),
        name="up_conv_gate",
    )(x, w_gate, w_val, state_rows, state_rows, conv_w, conv_w, conv_b.reshape(1, f2), conv_b.reshape(1, f2),
      *extra_args)
    act, tail_g, tail_v = outs[:3]
    w_bf = tuple(outs[3:5]) if rounding else w_up
    w_next_bf = outs[-1] if slab is not None else w_next.astype(BF16)
    tail = jnp.concatenate([tail_g, tail_v], axis=-1)
    if seq_rows >= tm:
        return act, tail.reshape(rows // seq_rows, SUBLANES, f2)[:, SUBLANES - (taps - 1):], w_bf, w_next_bf
    return act, jnp.swapaxes(tail, 0, 1), w_bf, w_next_bf


def _layer(x, shift_prev, wkv0, mem_k, mem_v, conv_prev, prm, want_v_rows):
    batch, seq_len, d = x.shape
    rows = batch * seq_len
    w = prm["width"]
    hd = prm["head_dim"]
    x2 = x.reshape(rows, d)

    xn = _rmsnorm(x2, prm["g_pre_mix"], BF16)
    n_shift = prm["mu_shift"].shape[1]
    proj = _matmul_cached(xn, prm, "w_in", 0, n_shift + 3 * w)
    new_shift = proj.reshape(batch, seq_len, -1)[:, -1, :n_shift]

    a_out, h_new = _rwkv(proj, _pad_state_rows(shift_prev[:, None, :]), jnp.swapaxes(wkv0, -1, -2), batch,
                         seq_len, prm)
    new_wkv = jnp.swapaxes(h_new, -1, -2)

    if seq_len % prm["sgu_chunk"] == 0:
        b_out, v_rows = _sgu(proj, n_shift, prm["w_s"], prm["sgu_bias"], prm, None, want_v_rows)
    else:
        assert seq_len == SUBLANES
        b_out, v_rows = _sgu(proj, n_shift, prm["w_s"], prm["sgu_bias"][:, :SUBLANES], prm, SUBLANES,
                             want_v_rows)
    c_out = _xattn(proj, n_shift + 2 * w, w, mem_k, mem_v, batch, seq_len, prm["xattn_heads"])

    mix, (prm["w_gates"], prm["w_branch"]), prm["w_out"] = _gate_branch_mix(
        xn, a_out, b_out, c_out, prm["w_gates"], n_shift + 3 * w, prm["w_branch"], prm["w_out"])
    h, hn = _mm_res_norm(mix, prm["w_out"], x2, prm["g_post_mix"], prm["g_pre_ffn"])

    act, conv_new, prm["w_up"], prm["w_down"] = _up_conv_gate(hn, prm["w_up"], conv_prev, prm["conv_w"],
                                                              prm["conv_b"], seq_len, prm["w_down"])
    y_out = _mm_res_norm(act, prm["w_down"], h, prm["g_post_ffn"])
    if want_v_rows:
        v_rows = v_rows.reshape(batch, seq_len, w)
    return y_out.reshape(batch, seq_len, d), new_shift, new_wkv, v_rows, conv_new


def _prepare(l, g_pre_mix, w_in, mu_shift, w0, w_decay, a0, w_aaa, w_gate, k_k, k_a, r_k, lnx_w, lnx_b,
             sgu_g, sgu_b, w_s, b_s, w_branch, w_out, g_post_mix, g_pre_ffn, w_up, conv_w, conv_b, w_down,
             g_post_ffn):
    heads, hd = r_k.shape[1], r_k.shape[2]
    w = heads * hd
    d = w_in.shape[1]
    n_shift = mu_shift.shape[1]
    rank_d, rank_a, rank_g = w_decay.shape[1], w_aaa.shape[1], w_gate.shape[1]
    assert rank_d + rank_a == LANES and rank_g == LANES and n_shift == 3 * w + 2 * LANES
    groups, sgu_chunk, _ = w_s.shape[1:]
    row = lambda t: t[l].reshape(1, -1)
    lane_head = jnp.arange(LANES) // hd
    gd = w // groups
    return dict(
        width=w, head_dim=hd, sgu_chunk=sgu_chunk,
        g_pre_mix=g_pre_mix[l],
        w_in=w_in[l], w_gates=w_in[l],
        mu_shift=row(mu_shift), w0=row(w0), a0=row(a0),
        wd_pad=jnp.pad(w_decay[l], ((0, rank_a), (0, 0))),
        wa_pad=jnp.pad(w_aaa[l], ((rank_d, 0), (0, 0))),
        w_gate=w_gate[l],
        k_k=row(k_k), k_a=row(k_a), r_k=row(r_k), lnx_w=row(lnx_w), lnx_b=row(lnx_b),
        ones2=(lane_head[:, None] == lane_head[None, :]).astype(BF16),
        sgu_g=row(sgu_g), sgu_b=row(sgu_b),
        w_s=w_s[l],
        sgu_bias=jnp.broadcast_to(b_s[l][:, :, None], (groups, sgu_chunk, gd)),
        w_branch=w_branch[l], w_out=w_out[l],
        g_post_mix=g_post_mix[l], g_pre_ffn=g_pre_ffn[l],
        w_up=w_up[l], conv_w=conv_w[l], conv_b=conv_b[l], w_down=w_down[l],
        g_post_ffn=g_post_ffn[l],
    )


def kernel(x_prompt, x_sample, mem_prompt, state_wkv, state_shift, cache_mem_k, cache_mem_v, state_ffn_conv, g_pre_mix, w_in, mu_shift, w0, w_decay, a0, w_aaa, w_gate, k_k, k_a, r_k, lnx_w, lnx_b, sgu_g, sgu_b, w_s, b_s, g_mem, w_mem_k, w_mem_v, w_branch, w_out, g_post_mix, g_pre_ffn, w_up, conv_w, conv_b, w_down, g_post_ffn):
    depth = w_in.shape[0]
    batch = x_prompt.shape[0]
    mem_len, d = mem_prompt.shape[1], mem_prompt.shape[2]
    xh, xhd = cache_mem_k.shape[3], cache_mem_k.shape[4]
    heads, hd = r_k.shape[1], r_k.shape[2]
    n_shift = mu_shift.shape[1]
    f2 = w_up.shape[2]
    taps = conv_w.shape[1]
    y_p, y_s = x_prompt, x_sample
    outs = [[] for _ in range(9)]
    for l in range(depth):
        prm = _prepare(l, g_pre_mix, w_in, mu_shift, w0, w_decay, a0, w_aaa, w_gate, k_k, k_a, r_k, lnx_w, lnx_b,
                       sgu_g, sgu_b, w_s, b_s, w_branch, w_out, g_post_mix, g_pre_ffn, w_up, conv_w, conv_b,
                       w_down, g_post_ffn)
        mn = _rmsnorm(mem_prompt.reshape(batch * mem_len, d), g_mem[l], BF16)
        prm["xattn_heads"] = xh
        mk_rows, mv_rows = _matmul_pair(mn, w_mem_k[l], w_mem_v[l])
        mk_p = mk_rows.reshape(batch, mem_len, xh, xhd)
        mv_p = mv_rows.reshape(batch, mem_len, xh, xhd)
        y_p, sh_p, wkv_p, _, cv_p = _layer(
            y_p, jnp.zeros((batch, n_shift), F32), jnp.zeros((batch, heads, hd, hd), F32), mk_rows, mv_rows,
            jnp.zeros((batch, taps - 1, f2), F32), prm, False)
        y_s, sh_s, wkv_s, vr_s, cv_s = _layer(y_s, state_shift[l], state_wkv[l], cache_mem_k[l], cache_mem_v[l],
                                              state_ffn_conv[l], prm, True)
        for lst, val in zip(outs, (wkv_p, sh_p, mk_p, mv_p, cv_p, wkv_s, sh_s, vr_s, cv_s)):
            lst.append(val)
    return (y_p, y_s) + tuple(jnp.stack(lst) for lst in outs)
```
